```python
import math
import jax, jax.numpy as jnp
from jax import lax
import numpy as np

D_MODEL = 2048
BATCH = 2
SEQ = 4096
DEPTH = 4
DEC_BATCH = 8
DEC_SEQ = 1
PAST_LEN = 16384
PAGE_SIZE = 128

RET_HEADS = 8
RET_DK = D_MODEL // RET_HEADS
RET_DV = 2 * RET_DK
RET_CHUNK = 128
NSA_HEADS = 16
NSA_HD = D_MODEL // NSA_HEADS
NSA_KV = 4
NSA_HPG = NSA_HEADS // NSA_KV
CMP_BLOCK = 32
CMP_STRIDE = 16
CMP_HIDDEN = 4 * NSA_HD
SEL_BLOCK = 64
N_SEL = 16
WINDOW = 512
Q_BLOCK = 128
N_KV_SLOTS = 4
N_EXPERTS = 16
N_GROUPS = 4
EXPERTS_PER_GROUP = N_EXPERTS // N_GROUPS
TOP_K = 2
D_EXPERT = 1408
PLE_DIM = 256
ROPE_THETA = 10000.0
LN_EPS = 1e-5
N_RET = (DEPTH + 1) // 2
N_NSA = DEPTH // 2
DN_ALPHA = (2 * DEPTH) ** 0.25
DN_BETA = (8 * DEPTH) ** -0.25
NEG = -1e30
FORCE_SCORE = float(NSA_HPG + 1)

kernel_name = 'hybrid_retention_nsa_grouped_moe_step'


def layer_norm(x, g, b):
    xf = x.astype(jnp.float32)
    mu = xf.mean(-1, keepdims=True)
    var = jnp.mean(jnp.square(xf - mu), -1, keepdims=True)
    return ((xf - mu) * lax.rsqrt(var + LN_EPS) * g + b).astype(x.dtype)


def rope(x, pos):
    half = x.shape[-1] // 2
    inv = ROPE_THETA ** (-jnp.arange(half, dtype=jnp.float32) / half)
    ang = pos.astype(jnp.float32)[:, None] * inv[None, :]
    cos = jnp.cos(ang)[None, :, None, :]
    sin = jnp.sin(ang)[None, :, None, :]
    xf = x.astype(jnp.float32)
    x1, x2 = xf[..., :half], xf[..., half:]
    return jnp.concatenate([x1 * cos - x2 * sin, x2 * cos + x1 * sin], axis=-1).astype(x.dtype)


def masked_softmax(s, mask):
    s = jnp.where(mask, s.astype(jnp.float32), NEG)
    return jax.nn.softmax(s, axis=-1) * mask


def retention(x, pos0, s0, w_in, w_out, gn_g):
    B, T, _ = x.shape
    H, DK, DV = RET_HEADS, RET_DK, RET_DV
    f32 = jnp.float32
    q, k, v, g = jnp.split(x @ w_in, [H * DK, 2 * H * DK, 2 * H * DK + H * DV], axis=-1)
    pos = pos0 + jnp.arange(T)
    q = rope(q.reshape(B, T, H, DK), pos).astype(f32)
    k = rope(k.reshape(B, T, H, DK), pos).astype(f32) * DK ** -0.5
    v = v.reshape(B, T, H, DV).astype(f32)
    C = math.gcd(T, RET_CHUNK)
    n = T // C
    log_gamma = jnp.log1p(-jnp.exp2(-5.0 - jnp.arange(H, dtype=f32)))
    idx = jnp.arange(C, dtype=f32)
    rel = idx[:, None] - idx[None, :]
    intra = jnp.where(rel >= 0, jnp.exp(jnp.maximum(rel, 0.0)[None] * log_gamma[:, None, None]), 0.0)
    q_decay = jnp.exp((idx[:, None] + 1.0) * log_gamma[None, :])[None, :, :, None]
    k_decay = jnp.exp((C - 1.0 - idx[:, None]) * log_gamma[None, :])[None, :, :, None]
    c_decay = jnp.exp(C * log_gamma)[None, :, None, None]

    def chunks(a):
        return a.reshape(B, n, C, H, a.shape[-1]).swapaxes(0, 1)

    def step(S, inp):
        qc, kc, vc = inp
        att = jnp.einsum('bihk,bjhk->bhij', qc, kc) * intra[None]
        o = jnp.einsum('bhij,bjhv->bihv', att, vc) + jnp.einsum('bihk,bhkv->bihv', qc, S) * q_decay
        S = S * c_decay + jnp.einsum('bjhk,bjhv->bhkv', kc * k_decay, vc)
        return S, o

    S, o = lax.scan(step, s0.astype(f32), (chunks(q), chunks(k), chunks(v)))
    o = o.swapaxes(0, 1).reshape(B, T, H, DV)
    mu = o.mean(-1, keepdims=True)
    var = jnp.mean(jnp.square(o - mu), -1, keepdims=True)
    o = ((o - mu) * lax.rsqrt(var + LN_EPS)).reshape(B, T, H * DV) * gn_g
    y = (jax.nn.silu(g) * o.astype(x.dtype)) @ w_out
    return y, S.astype(s0.dtype)


def nsa(x, pos0, hist_kv, hist_win, w_in, w_out, cmp_pos, cmp_w1, cmp_w2):
    B, T, _ = x.shape
    H, G, R, HD, W = NSA_HEADS, NSA_KV, NSA_HPG, NSA_HD, WINDOW
    q, kv, gl = jnp.split(x @ w_in, [H * HD, H * HD + 6 * G * HD], axis=-1)
    pos = pos0 + jnp.arange(T)
    q = rope(q.reshape(B, T, H, HD), pos)
    kv = kv.reshape(B, T, 6, G, HD)
    keys = rope(kv[:, :, 0::2].reshape(B, T, 3 * G, HD), pos).reshape(B, T, 3, G, HD)
    kv = jnp.stack([keys[:, :, 0], kv[:, :, 1], keys[:, :, 1], kv[:, :, 3], keys[:, :, 2], kv[:, :, 5]], axis=2)
    gates = jax.nn.sigmoid(gl.astype(jnp.float32)).reshape(B, T, G, R, 3)
    new_rows = kv[:, :, :N_KV_SLOTS]
    new_win = kv[:, :, N_KV_SLOTS:]

    full = jnp.concatenate([hist_kv, new_rows], axis=1)
    L = full.shape[1]
    Lp = -(-L // SEL_BLOCK) * SEL_BLOCK
    full = jnp.pad(full, ((0, 0), (0, Lp - L), (0, 0), (0, 0), (0, 0)))
    NU = Lp // CMP_STRIDE
    RB = CMP_BLOCK // CMP_STRIDE
    NC = NU - RB + 1
    NS = Lp // SEL_BLOCK

    def compress(rows, pe, w1, w2):
        units = rows.reshape(B, NU, CMP_STRIDE, G, HD)
        blocks = jnp.concatenate([units[:, r:r + NC] for r in range(RB)], axis=2) + pe[:, None, :]
        hid = jax.nn.gelu(jnp.einsum('bclgd,ldf->bcgf', blocks, w1))
        return jnp.einsum('bcgf,fd->bcgd', hid, w2)

    k_cmp = compress(full[:, :, 0], cmp_pos[0], cmp_w1[0], cmp_w2[0])
    v_cmp = compress(full[:, :, 1], cmp_pos[1], cmp_w1[1], cmp_w2[1])
    k_slc = full[:, :, 2].reshape(B, NS, SEL_BLOCK, G, HD).transpose(0, 3, 1, 2, 4)
    v_slc = full[:, :, 3].reshape(B, NS, SEL_BLOCK, G, HD).transpose(0, 3, 1, 2, 4)
    H0 = hist_win.shape[1]
    win_all = jnp.concatenate([jnp.zeros((B, W - H0, 2, G, HD), kv.dtype), hist_win, new_win], axis=1)

    scale = HD ** -0.5
    n_top = min(N_SEL, NS)
    blk = math.gcd(T, Q_BLOCK)
    c_end = jnp.arange(NC) * CMP_STRIDE + CMP_BLOCK - 1
    s_start = jnp.arange(NS) * SEL_BLOCK
    b_ix = jnp.arange(B)[:, None, None, None]
    g_ix = jnp.arange(G)[None, :, None, None]

    def block_fn(bi):
        qs = bi * blk
        qb = lax.dynamic_slice_in_dim(q, qs, blk, axis=1).reshape(B, blk, G, R, HD)
        t = pos0 + qs + jnp.arange(blk)
        s_c = jnp.einsum('bqgrd,bcgd->bgrqc', qb, k_cmp) * scale
        p_c = masked_softmax(s_c, c_end[None, :] <= t[:, None])
        o_c = jnp.einsum('bgrqc,bcgd->bqgrd', p_c.astype(v_cmp.dtype), v_cmp)
        imp = p_c.sum(2)
        units = sum(jnp.pad(imp, ((0, 0), (0, 0), (0, 0), (r, RB - 1 - r))) for r in range(RB)) / RB
        sel = units.reshape(B, G, blk, NS, SEL_BLOCK // CMP_STRIDE).sum(-1)
        cur = t // SEL_BLOCK
        bidx = jnp.arange(NS)[None, :]
        forced = (bidx == 0) | (bidx == cur[:, None]) | (bidx == cur[:, None] - 1)
        valid = s_start[None, :] <= t[:, None]
        sel = jnp.where(valid, jnp.where(forced, FORCE_SCORE, sel), NEG)
        _, top = lax.top_k(sel, n_top)
        k_sel = k_slc[b_ix, g_ix, top]
        v_sel = v_slc[b_ix, g_ix, top]
        s_s = jnp.einsum('bqgrd,bgqnsd->bgrqns', qb, k_sel) * scale
        kpos = top[..., None] * SEL_BLOCK + jnp.arange(SEL_BLOCK)
        m_s = kpos <= t[None, None, :, None, None]
        p_s = masked_softmax(s_s.reshape(B, G, R, blk, n_top * SEL_BLOCK),
                             m_s.reshape(B, G, 1, blk, n_top * SEL_BLOCK))
        o_s = jnp.einsum('bgrqns,bgqnsd->bqgrd',
                         p_s.reshape(B, G, R, blk, n_top, SEL_BLOCK).astype(v_sel.dtype), v_sel)
        kw = lax.dynamic_slice_in_dim(win_all, qs, W + blk, axis=1)
        wpos = pos0 - W + qs + jnp.arange(W + blk)
        m_w = (wpos[None, :] <= t[:, None]) & (wpos[None, :] > t[:, None] - W) & (wpos[None, :] >= 0)
        s_w = jnp.einsum('bqgrd,bkgd->bgrqk', qb, kw[:, :, 0]) * scale
        p_w = masked_softmax(s_w, m_w)
        o_w = jnp.einsum('bgrqk,bkgd->bqgrd', p_w.astype(kw.dtype), kw[:, :, 1])
        gb = lax.dynamic_slice_in_dim(gates, qs, blk, axis=1)
        o = gb[..., 0:1] * o_c + gb[..., 1:2] * o_s + gb[..., 2:3] * o_w
        return o.reshape(B, blk, H * HD).astype(x.dtype)

    o = lax.map(block_fn, jnp.arange(T // blk))
    o = o.swapaxes(0, 1).reshape(B, T, H * HD)
    n_keep = min(W, H0 + T)
    return o @ w_out, new_rows, win_all[:, W + T - n_keep:]


def moe(x, router_w, router_b, w_gate, w_up, w_down):
    B, T, D = x.shape
    xt = x.reshape(B * T, D)
    aff = jax.nn.softmax((xt @ router_w).astype(jnp.float32) + router_b.astype(jnp.float32), axis=-1)
    grp_score = lax.top_k(aff.reshape(-1, N_GROUPS, EXPERTS_PER_GROUP), TOP_K)[0].sum(-1)
    grp = jnp.argmax(grp_score, axis=-1)
    in_grp = (jnp.arange(N_EXPERTS) // EXPERTS_PER_GROUP)[None, :] == grp[:, None]
    top_w, top_i = lax.top_k(jnp.where(in_grp, aff, -1.0), TOP_K)
    top_w = top_w / top_w.sum(-1, keepdims=True)
    combine = jnp.einsum('nk,nke->ne', top_w, jax.nn.one_hot(top_i, N_EXPERTS, dtype=jnp.float32))
    h = jax.nn.silu(jnp.einsum('nd,edf->nef', xt, w_gate)) * jnp.einsum('nd,edf->nef', xt, w_up)
    y = jnp.einsum('nef,efd->nd', h * combine[:, :, None].astype(h.dtype), w_down)
    return y.reshape(B, T, D)


def run_trunk(x, p, pos0, ret_state, kv_pool, page_table, win_state,
              ret_w_in, ret_w_out, ret_gn_g, nsa_w_in, nsa_w_out, nsa_cmp_pos, nsa_cmp_w1, nsa_cmp_w2,
              ln_g, ln_b, router_w, router_b, moe_w_gate, moe_w_up, moe_w_down, ple_w_gate, ple_w_proj):
    B, T, _ = x.shape
    ret_new, kv_new, win_new = [], [], []
    for i in range(DEPTH):
        j = i // 2
        if i % 2 == 0:
            if ret_state is None:
                s0 = jnp.zeros((B, RET_HEADS, RET_DK, RET_DV), x.dtype)
            else:
                s0 = ret_state[j]
            h, s1 = retention(x, pos0, s0, ret_w_in[j], ret_w_out[j], ret_gn_g[j])
            ret_new.append(s1)
        else:
            if kv_pool is None:
                hist = jnp.zeros((B, 0, N_KV_SLOTS, NSA_KV, NSA_HD), x.dtype)
                hwin = jnp.zeros((B, 0, 2, NSA_KV, NSA_HD), x.dtype)
            else:
                hist = kv_pool[j][page_table].reshape(B, pos0, N_KV_SLOTS, NSA_KV, NSA_HD)
                hwin = win_state[j]
            h, rows, win = nsa(x, pos0, hist, hwin, nsa_w_in[j], nsa_w_out[j],
                               nsa_cmp_pos[j], nsa_cmp_w1[j], nsa_cmp_w2[j])
            kv_new.append(rows)
            win_new.append(win)
        x = layer_norm(DN_ALPHA * x + h, ln_g[i, 0], ln_b[i, 0])
        x = layer_norm(DN_ALPHA * x + moe(x, router_w, router_b, moe_w_gate[i], moe_w_up[i], moe_w_down[i]),
                       ln_g[i, 1], ln_b[i, 1])
        x = x + jax.nn.sigmoid(x @ ple_w_gate[i]) * (p[i] @ ple_w_proj[i])
    return x, jnp.stack(ret_new), jnp.stack(kv_new), jnp.stack(win_new)


def setup_inputs(seed: int = 0) -> dict:
    key = jax.random.key(seed)
    keys = iter(jax.random.split(key, 40))

    def nrm(shape, scale=1.0):
        return jax.random.normal(next(keys), shape, jnp.float32) * scale

    n_pages = PAST_LEN // PAGE_SIZE
    n_pool = (DEC_BATCH * n_pages * 5) // 4
    wbuf = min(WINDOW, PAST_LEN)
    ret_in = 2 * RET_HEADS * RET_DK + 2 * RET_HEADS * RET_DV
    nsa_in = NSA_HEADS * NSA_HD + 6 * NSA_KV * NSA_HD + 3 * NSA_HEADS
    perm = jax.random.permutation(next(keys), n_pool)
    page_table = perm[: DEC_BATCH * n_pages].reshape(DEC_BATCH, n_pages).astype(jnp.int32)
    return {
        'x_prompt': nrm((BATCH, SEQ, D_MODEL)),
        'x_sample': nrm((DEC_BATCH, DEC_SEQ, D_MODEL)),
        'state_ret': nrm((N_RET, DEC_BATCH, RET_HEADS, RET_DK, RET_DV), RET_DK ** -0.5),
        'cache_nsa_kv': nrm((N_NSA, n_pool, PAGE_SIZE, N_KV_SLOTS, NSA_KV, NSA_HD)),
        'state_nsa_win': nrm((N_NSA, DEC_BATCH, wbuf, 2, NSA_KV, NSA_HD)),
        'page_table': page_table,
        'p_prompt': nrm((DEPTH, BATCH, SEQ, PLE_DIM)),
        'p_sample': nrm((DEPTH, DEC_BATCH, DEC_SEQ, PLE_DIM)),
        'ret_w_in': nrm((N_RET, D_MODEL, ret_in), D_MODEL ** -0.5),
        'ret_w_out': nrm((N_RET, RET_HEADS * RET_DV, D_MODEL), DN_BETA * (RET_HEADS * RET_DV) ** -0.5),
        'ret_gn_g': 1.0 + nrm((N_RET, RET_HEADS * RET_DV), 0.02),
        'nsa_w_in': nrm((N_NSA, D_MODEL, nsa_in), D_MODEL ** -0.5),
        'nsa_w_out': nrm((N_NSA, NSA_HEADS * NSA_HD, D_MODEL), DN_BETA * (NSA_HEADS * NSA_HD) ** -0.5),
        'nsa_cmp_pos': nrm((N_NSA, 2, CMP_BLOCK, NSA_HD), 0.02),
        'nsa_cmp_w1': nrm((N_NSA, 2, CMP_BLOCK, NSA_HD, CMP_HIDDEN), (CMP_BLOCK * NSA_HD) ** -0.5),
        'nsa_cmp_w2': nrm((N_NSA, 2, CMP_HIDDEN, NSA_HD), CMP_HIDDEN ** -0.5),
        'ln_g': 1.0 + nrm((DEPTH, 2, D_MODEL), 0.02),
        'ln_b': nrm((DEPTH, 2, D_MODEL), 0.02),
        'router_w': nrm((D_MODEL, N_EXPERTS), D_MODEL ** -0.5),
        'router_b': nrm((N_EXPERTS,), 0.01),
        'moe_w_gate': nrm((DEPTH, N_EXPERTS, D_MODEL, D_EXPERT), D_MODEL ** -0.5),
        'moe_w_up': nrm((DEPTH, N_EXPERTS, D_MODEL, D_EXPERT), D_MODEL ** -0.5),
        'moe_w_down': nrm((DEPTH, N_EXPERTS, D_EXPERT, D_MODEL), DN_BETA * D_EXPERT ** -0.5),
        'ple_w_gate': nrm((DEPTH, D_MODEL, D_MODEL), D_MODEL ** -0.5),
        'ple_w_proj': nrm((DEPTH, PLE_DIM, D_MODEL), PLE_DIM ** -0.5),
    }


def reference(x_prompt, x_sample, state_ret, cache_nsa_kv, state_nsa_win, page_table, p_prompt, p_sample,
              ret_w_in, ret_w_out, ret_gn_g, nsa_w_in, nsa_w_out, nsa_cmp_pos, nsa_cmp_w1, nsa_cmp_w2,
              ln_g, ln_b, router_w, router_b, moe_w_gate, moe_w_up, moe_w_down, ple_w_gate, ple_w_proj):
    past = page_table.shape[1] * cache_nsa_kv.shape[2]
    y_prompt, ret_p, kv_p, win_p = run_trunk(
        x_prompt, p_prompt, 0, None, None, None, None,
        ret_w_in, ret_w_out, ret_gn_g, nsa_w_in, nsa_w_out, nsa_cmp_pos, nsa_cmp_w1, nsa_cmp_w2,
        ln_g, ln_b, router_w, router_b, moe_w_gate, moe_w_up, moe_w_down, ple_w_gate, ple_w_proj)
    y_sample, ret_s, kv_s, win_s = run_trunk(
        x_sample, p_sample, past, state_ret, cache_nsa_kv, page_table, state_nsa_win,
        ret_w_in, ret_w_out, ret_gn_g, nsa_w_in, nsa_w_out, nsa_cmp_pos, nsa_cmp_w1, nsa_cmp_w2,
        ln_g, ln_b, router_w, router_b, moe_w_gate, moe_w_up, moe_w_down, ple_w_gate, ple_w_proj)
    return (y_prompt, y_sample, ret_p, ret_s, kv_p, kv_s, win_p, win_s)
```

```python
import functools
import math

import jax
import jax.numpy as jnp
from jax import lax
from jax.experimental import pallas as pl
from jax.experimental.pallas import tpu as pltpu

f32 = jnp.float32
bf16 = jnp.bfloat16

D_MODEL = 2048
DEPTH = 4
RET_HEADS = 8
RET_DK = D_MODEL // RET_HEADS
RET_DV = 2 * RET_DK
RET_CHUNK = 128
NSA_HEADS = 16
NSA_HD = D_MODEL // NSA_HEADS
NSA_KV = 4
NSA_HPG = NSA_HEADS // NSA_KV
CMP_BLOCK = 32
CMP_STRIDE = 16
CMP_HIDDEN = 4 * NSA_HD
SEL_BLOCK = 64
N_SEL = 16
WINDOW = 512
Q_BLOCK = 128
N_KV_SLOTS = 4
N_EXPERTS = 16
N_GROUPS = 4
EXPERTS_PER_GROUP = N_EXPERTS // N_GROUPS
D_EXPERT = 1408
PLE_DIM = 256
ROPE_THETA = 10000.0
LN_EPS = 1e-5
DN_ALPHA = (2 * DEPTH) ** 0.25
NEG = -1e30
FORCE_SCORE = float(NSA_HPG + 1)
NSA_SCALE = NSA_HD ** -0.5

V7X_VMEM_BYTES = 64 * 1024 * 1024
VMEM_LIMIT = V7X_VMEM_BYTES - 8 * 1024 * 1024
LANES = 128
MOE_TILE = 256


def _params(*sem):
    return pltpu.CompilerParams(dimension_semantics=sem, vmem_limit_bytes=VMEM_LIMIT)


def _layer_norm(v, g, b):
    mu = jnp.mean(v, axis=-1, keepdims=True)
    d = v - mu
    var = jnp.mean(d * d, axis=-1, keepdims=True)
    return d * lax.rsqrt(var + LN_EPS) * g + b


def _dot(a, b):
    return jnp.dot(a, b, preferred_element_type=f32)


def _dot_nt(a, b):
    return lax.dot_general(a, b, (((1,), (1,)), ((), ())), preferred_element_type=f32)


def _dot_tn(a, b):
    return lax.dot_general(a, b, (((0,), (0,)), ((), ())), preferred_element_type=f32)


def _rotate_half(v, hd):
    pieces = []
    for c in range(0, v.shape[1], hd):
        if hd == 2 * LANES:
            pieces += [v[:, c + LANES:c + hd], v[:, c:c + LANES]]
        else:
            pieces.append(pltpu.roll(v[:, c:c + hd], hd // 2, 1))
    return jnp.concatenate(pieces, axis=1)


def _mm_rope_kernel(flags_ref, x_ref, w_ref, cos_ref, sin_ref, *o_refs, hd):
    acc = _dot(x_ref[...].astype(bf16), w_ref[...])
    roped = flags_ref[pl.program_id(1)] == 1

    @pl.when(roped)
    def _():
        reps = acc.shape[1] // hd
        cos = jnp.concatenate([cos_ref[...]] * reps, axis=1)
        sin = jnp.concatenate([sin_ref[...]] * reps, axis=1)
        v = acc * cos + _rotate_half(acc, hd) * sin
        for o in o_refs:
            o[...] = v.astype(o.dtype)

    @pl.when(jnp.logical_not(roped))
    def _():
        for o in o_refs:
            o[...] = acc.astype(o.dtype)


def mm_rope(x, w, cos, sin, flags, out_dtypes, *, hd, tm, tn):
    n, k = x.shape
    m = w.shape[1]
    t = cos.shape[0]
    tm = min(tm, n)
    assert n % tm == 0 and m % tn == 0 and t % tm == 0 and tn % hd == 0
    tper = t // tm
    grid_spec = pltpu.PrefetchScalarGridSpec(
        num_scalar_prefetch=1,
        grid=(n // tm, m // tn),
        in_specs=[
            pl.BlockSpec((tm, k), lambda i, j, f: (i, 0)),
            pl.BlockSpec((k, tn), lambda i, j, f: (0, j)),
            pl.BlockSpec((tm, hd), lambda i, j, f: (i % tper, 0)),
            pl.BlockSpec((tm, hd), lambda i, j, f: (i % tper, 0)),
        ],
        out_specs=[pl.BlockSpec((tm, tn), lambda i, j, f: (i, j)) for _ in out_dtypes],
    )
    return pl.pallas_call(
        functools.partial(_mm_rope_kernel, hd=hd),
        grid_spec=grid_spec,
        out_shape=[jax.ShapeDtypeStruct((n, m), d) for d in out_dtypes],
        compiler_params=_params("parallel", "arbitrary"),
        name="mm_rope",
    )(flags, x, w, cos, sin)


def _mm_kernel(x_ref, w_ref, o_ref):
    o_ref[...] = _dot(x_ref[...].astype(bf16), w_ref[...]).astype(o_ref.dtype)


def mm(x, w, out_dtype, *, tm, tn):
    n, k = x.shape
    m = w.shape[1]
    tm = min(tm, n)
    tn = min(tn, m)
    assert n % tm == 0 and m % tn == 0
    return pl.pallas_call(
        _mm_kernel,
        grid=(n // tm, m // tn),
        in_specs=[pl.BlockSpec((tm, k), lambda i, j: (i, 0)), pl.BlockSpec((k, tn), lambda i, j: (0, j))],
        out_specs=pl.BlockSpec((tm, tn), lambda i, j: (i, j)),
        out_shape=jax.ShapeDtypeStruct((n, m), out_dtype),
        compiler_params=_params("parallel", "arbitrary"),
        name="mm",
    )(x, w)


def _mm_ln_kernel(a_ref, w_ref, x_ref, g_ref, b_ref, o_ref, acc_ref):
    kk = pl.program_id(1)

    @pl.when(kk == 0)
    def _():
        acc_ref[...] = jnp.zeros_like(acc_ref)

    acc_ref[...] += _dot(a_ref[...].astype(bf16), w_ref[...])

    @pl.when(kk == pl.num_programs(1) - 1)
    def _():
        o_ref[...] = _layer_norm(DN_ALPHA * x_ref[...] + acc_ref[...], g_ref[...], b_ref[...])


def mm_ln(a, w, x, g, b, *, tm, tk):
    n, k = a.shape
    d = w.shape[1]
    tm = min(tm, n)
    tk = min(tk, k)
    assert n % tm == 0 and k % tk == 0
    return pl.pallas_call(
        _mm_ln_kernel,
        grid=(n // tm, k // tk),
        in_specs=[
            pl.BlockSpec((tm, tk), lambda i, kk: (i, kk)),
            pl.BlockSpec((tk, d), lambda i, kk: (kk, 0)),
            pl.BlockSpec((tm, d), lambda i, kk: (i, 0)),
            pl.BlockSpec((1, d), lambda i, kk: (0, 0)),
            pl.BlockSpec((1, d), lambda i, kk: (0, 0)),
        ],
        out_specs=pl.BlockSpec((tm, d), lambda i, kk: (i, 0)),
        out_shape=jax.ShapeDtypeStruct((n, d), f32),
        scratch_shapes=[pltpu.VMEM((tm, d), f32)],
        compiler_params=_params("parallel", "arbitrary"),
        name="mm_ln",
    )(a, w, x, g, b)


def _retention_kernel(q_ref, k_ref, v_ref, g_ref, gn_ref, intra_ref, qdec_ref, kdec_ref, o_ref, s_ref, state):
    c = pl.program_id(2)

    @pl.when(c == 0)
    def _():
        state[...] = jnp.zeros_like(state)

    q = q_ref[...]
    k = k_ref[...] * (RET_DK ** -0.5)
    v = v_ref[...]
    qdec = qdec_ref[0]
    kdec = kdec_ref[0]
    cdec = qdec[-1:, :]
    s_prev = state[...]
    att = _dot_nt(q, k) * intra_ref[0]
    o = _dot(att.astype(bf16), v) + _dot(q, s_prev.astype(bf16)) * qdec
    kd = (k.astype(f32) * kdec).astype(bf16)
    s_new = s_prev * cdec + _dot_tn(kd, v)
    state[...] = s_new

    mu = jnp.mean(o, axis=-1, keepdims=True)
    d = o - mu
    var = jnp.mean(d * d, axis=-1, keepdims=True)
    o = d * lax.rsqrt(var + LN_EPS) * gn_ref[...]
    o_ref[...] = (jax.nn.silu(g_ref[...].astype(f32)) * o).astype(o_ref.dtype)

    @pl.when(c == pl.num_programs(2) - 1)
    def _():
        s_ref[0, 0] = s_new


def _retention_decay_tables(chunk):
    h = jnp.arange(RET_HEADS, dtype=f32)
    log_gamma = jnp.log1p(-jnp.exp2(-5.0 - h))
    idx = jnp.arange(chunk, dtype=f32)
    rel = idx[:, None] - idx[None, :]
    intra = jnp.where(rel >= 0, jnp.exp(jnp.maximum(rel, 0.0)[None] * log_gamma[:, None, None]), 0.0)
    qdec = jnp.exp((idx[None, :] + 1.0) * log_gamma[:, None])[..., None]
    kdec = jnp.exp((chunk - 1.0 - idx[None, :]) * log_gamma[:, None])[..., None]
    return intra, qdec, kdec


def retention_prompt(qk, vg, gn_g, batch):
    n = qk.shape[0]
    t = n // batch
    chunk = math.gcd(t, RET_CHUNK)
    nch = t // chunk
    h = RET_HEADS
    intra, qdec, kdec = _retention_decay_tables(chunk)
    row = lambda b, hh, c: b * nch + c
    return pl.pallas_call(
        _retention_kernel,
        grid=(batch, h, nch),
        in_specs=[
            pl.BlockSpec((chunk, RET_DK), lambda b, hh, c: (row(b, hh, c), hh)),
            pl.BlockSpec((chunk, RET_DK), lambda b, hh, c: (row(b, hh, c), h + hh)),
            pl.BlockSpec((chunk, RET_DV), lambda b, hh, c: (row(b, hh, c), hh)),
            pl.BlockSpec((chunk, RET_DV), lambda b, hh, c: (row(b, hh, c), h + hh)),
            pl.BlockSpec((1, RET_DV), lambda b, hh, c: (0, hh)),
            pl.BlockSpec((1, chunk, chunk), lambda b, hh, c: (hh, 0, 0)),
            pl.BlockSpec((1, chunk, 1), lambda b, hh, c: (hh, 0, 0)),
            pl.BlockSpec((1, chunk, 1), lambda b, hh, c: (hh, 0, 0)),
        ],
        out_specs=[
            pl.BlockSpec((chunk, RET_DV), lambda b, hh, c: (row(b, hh, c), hh)),
            pl.BlockSpec((1, 1, RET_DK, RET_DV), lambda b, hh, c: (b, hh, 0, 0)),
        ],
        out_shape=[
            jax.ShapeDtypeStruct((n, h * RET_DV), bf16),
            jax.ShapeDtypeStruct((batch, h, RET_DK, RET_DV), f32),
        ],
        scratch_shapes=[pltpu.VMEM((RET_DK, RET_DV), f32)],
        compiler_params=_params("parallel", "parallel", "arbitrary"),
        name="retention_prompt",
    )(qk, qk, vg, vg, gn_g, intra, qdec, kdec)


def _rope_tables(pos, hd):
    half = hd // 2
    inv = ROPE_THETA ** (-jnp.arange(half, dtype=f32) / half)
    ang = pos.astype(f32)[:, None] * inv[None, :]
    cos, sin = jnp.cos(ang), jnp.sin(ang)
    return jnp.concatenate([cos, cos], axis=1), jnp.concatenate([-sin, sin], axis=1)


def _compress_units(load_unit_row, pe, w1_ref, nu):
    xa, xb = [], []
    for l in range(CMP_STRIDE):
        xl = load_unit_row(l)
        xa.append((xl + pe[l:l + 1]).astype(bf16))
        xb.append((xl + pe[CMP_STRIDE + l:CMP_STRIDE + l + 1]).astype(bf16))
    half = CMP_STRIDE * NSA_HD
    a = _dot(jnp.concatenate(xa, axis=1), w1_ref[0, :half])
    b = _dot(jnp.concatenate(xb, axis=1), w1_ref[0, half:])
    return a, b


def _compress_kernel(kv_ref, pe_ref, w1_ref, w2_ref, o_ref):
    nu = kv_ref.shape[0] // CMP_STRIDE
    a, b = _compress_units(lambda l: kv_ref[pl.ds(l, nu, stride=CMP_STRIDE), :], pe_ref[0], w1_ref, nu)
    hid = a + pltpu.roll(b, nu - 1, 0)
    o_ref[0, 0, 0] = _dot(jax.nn.gelu(hid).astype(bf16), w2_ref[0]).astype(o_ref.dtype)


def compress_prompt(kvf, pe, w1, w2, batch):
    n = kvf.shape[0]
    t = n // batch
    nu = t // CMP_STRIDE
    g = NSA_KV
    return pl.pallas_call(
        _compress_kernel,
        grid=(batch, 2, g),
        in_specs=[
            pl.BlockSpec((t, NSA_HD), lambda b, s, gg: (b, s * g + gg)),
            pl.BlockSpec((1, CMP_BLOCK, NSA_HD), lambda b, s, gg: (s, 0, 0)),
            pl.BlockSpec((1, CMP_BLOCK * NSA_HD, CMP_HIDDEN), lambda b, s, gg: (s, 0, 0)),
            pl.BlockSpec((1, CMP_HIDDEN, NSA_HD), lambda b, s, gg: (s, 0, 0)),
        ],
        out_specs=pl.BlockSpec((1, 1, 1, nu, NSA_HD), lambda b, s, gg: (b, s, gg, 0, 0)),
        out_shape=jax.ShapeDtypeStruct((batch, 2, g, nu, NSA_HD), bf16),
        compiler_params=_params("parallel", "parallel", "parallel"),
        name="compress_prompt",
    )(kvf, pe, w1, w2)


def _select_blocks(sel, n_top):
    ns = sel.shape[-1]
    jidx = lax.broadcasted_iota(jnp.int32, sel.shape, sel.ndim - 1)
    chosen = jnp.zeros(sel.shape, f32)
    for _ in range(n_top):
        mx = jnp.max(sel, axis=-1, keepdims=True)
        idx = jnp.min(jnp.where(sel == mx, jidx, ns), axis=-1, keepdims=True)
        hit = jidx == idx
        chosen = jnp.where(hit, 1.0, chosen)
        sel = jnp.where(hit, -jnp.inf, sel)
    return chosen


def _masked_softmax(s, mask):
    s = jnp.where(mask, s, NEG)
    e = jnp.exp(s - jnp.max(s, axis=-1, keepdims=True))
    return jnp.where(mask, e / jnp.sum(e, axis=-1, keepdims=True), 0.0)


def _nsa_attn_kernel(q_ref, kc_ref, vc_ref, ks_ref, vs_ref, kw_ref, vw_ref, gl_ref, msel_ref, exp_ref,
                     o_ref, mexp, *, n_top):
    qi = pl.program_id(2)
    blk = q_ref.shape[0]
    r_heads, hd = NSA_HPG, NSA_HD
    q = q_ref[...]
    qb = jnp.concatenate([q[:, r * hd:(r + 1) * hd] for r in range(r_heads)], axis=0)
    t = qi * blk + lax.broadcasted_iota(jnp.int32, (blk, 1), 0)

    kc = kc_ref[0, 0, 0]
    nc = kc.shape[0]
    s = (_dot_nt(qb, kc) * NSA_SCALE).reshape(r_heads, blk, nc)
    c_end = lax.broadcasted_iota(jnp.int32, (1, nc), 1) * CMP_STRIDE + (CMP_BLOCK - 1)
    p = _masked_softmax(s, (c_end <= t)[None])
    o_c = _dot(p.reshape(r_heads * blk, nc).astype(bf16), vc_ref[0, 0, 0]).reshape(r_heads, blk, hd)

    imp = p[0]
    for r in range(1, r_heads):
        imp = imp + p[r]
    sel = jnp.dot(imp, msel_ref[...], precision=lax.Precision.HIGHEST, preferred_element_type=f32)
    jidx = lax.broadcasted_iota(jnp.int32, sel.shape, 1)
    cur = lax.shift_right_logical(t, int(math.log2(SEL_BLOCK)))
    forced = (jidx == 0) | (jidx == cur) | (jidx == cur - 1)
    sel = jnp.where(jidx * SEL_BLOCK <= t, jnp.where(forced, FORCE_SCORE, sel), NEG)
    mexp[...] = _dot(_select_blocks(sel, n_top).astype(bf16), exp_ref[...])

    def flash(k_ref, v_ref, lo, hi, mask_fn):
        def body(kc_i, carry):
            m, l, acc = carry
            off = pl.multiple_of(kc_i * blk, blk)
            s = (_dot_nt(qb, k_ref[pl.ds(off, blk), :]) * NSA_SCALE).reshape(r_heads, blk, blk)
            kpos = off + lax.broadcasted_iota(jnp.int32, (1, blk), 1)
            msk = mask_fn(off, kpos)[None]
            s = jnp.where(msk, s, NEG)
            m_new = jnp.maximum(m, jnp.max(s, axis=-1, keepdims=True))
            alpha = jnp.exp(m - m_new)
            e = jnp.where(msk, jnp.exp(s - m_new), 0.0)
            l = l * alpha + jnp.sum(e, axis=-1, keepdims=True)
            pv = _dot(e.reshape(r_heads * blk, blk).astype(bf16), v_ref[pl.ds(off, blk), :])
            return m_new, l, acc * alpha + pv.reshape(r_heads, blk, hd)

        init = (jnp.full((r_heads, blk, 1), NEG, f32), jnp.zeros((r_heads, blk, 1), f32),
                jnp.zeros((r_heads, blk, hd), f32))
        _, l, acc = lax.fori_loop(lo, hi, body, init)
        return acc / l

    o_s = flash(ks_ref, vs_ref, 0, qi + 1,
                lambda off, kpos: (mexp[:, pl.ds(off, blk)] > 0.5) & (kpos <= t))
    o_w = flash(kw_ref, vw_ref, jnp.maximum(qi - WINDOW // blk, 0), qi + 1,
                lambda off, kpos: (kpos <= t) & (kpos > t - WINDOW))

    gates = jax.nn.sigmoid(gl_ref[...])
    outs = []
    for r in range(r_heads):
        outs.append(gates[:, 3 * r:3 * r + 1] * o_c[r] + gates[:, 3 * r + 1:3 * r + 2] * o_s[r]
                    + gates[:, 3 * r + 2:3 * r + 3] * o_w[r])
    o_ref[...] = jnp.concatenate(outs, axis=1).astype(o_ref.dtype)


def _selection_constants(nu, ns, nkeys):
    c = jnp.arange(nu)[:, None]
    j = jnp.arange(ns)[None, :]
    per_sel = SEL_BLOCK // CMP_STRIDE
    rb = CMP_BLOCK // CMP_STRIDE
    msel = sum(((c + r) // per_sel == j).astype(f32) for r in range(rb)) / rb
    msel = jnp.where(c < nu - rb + 1, msel, 0.0)
    expand = (jnp.arange(nkeys)[None, :] // SEL_BLOCK == jnp.arange(ns)[:, None]).astype(bf16)
    return msel, expand


def nsa_attn_prompt(q, cmp, kvb, gl, batch):
    n = q.shape[0]
    t = n // batch
    blk = math.gcd(t, Q_BLOCK)
    nq = t // blk
    g = NSA_KV
    nu = t // CMP_STRIDE
    ns = t // SEL_BLOCK
    msel, expand = _selection_constants(nu, ns, t)
    gw = NSA_HPG * NSA_HD
    kv_spec = lambda slot: pl.BlockSpec((t, NSA_HD), lambda b, gg, qi: (b, slot * g + gg))
    cmp_spec = lambda s: pl.BlockSpec((1, 1, 1, nu, NSA_HD), lambda b, gg, qi: (b, s, gg, 0, 0))
    return pl.pallas_call(
        functools.partial(_nsa_attn_kernel, n_top=min(N_SEL, ns)),
        grid=(batch, g, nq),
        in_specs=[
            pl.BlockSpec((blk, gw), lambda b, gg, qi: (b * nq + qi, gg)),
            cmp_spec(0), cmp_spec(1),
            kv_spec(2), kv_spec(3), kv_spec(4), kv_spec(5),
            pl.BlockSpec((blk, LANES), lambda b, gg, qi: (b * nq + qi, gg)),
            pl.BlockSpec((nu, ns), lambda b, gg, qi: (0, 0)),
            pl.BlockSpec((ns, t), lambda b, gg, qi: (0, 0)),
        ],
        out_specs=pl.BlockSpec((blk, gw), lambda b, gg, qi: (b * nq + qi, gg)),
        out_shape=jax.ShapeDtypeStruct((n, NSA_HEADS * NSA_HD), bf16),
        scratch_shapes=[pltpu.VMEM((blk, t), f32)],
        compiler_params=_params("parallel", "parallel", "arbitrary"),
        name="nsa_attn_prompt",
    )(q, cmp, cmp, kvb, kvb, kvb, kvb, gl, msel, expand)


def _top2(vals, lane):
    width = vals.shape[-1]
    m1 = jnp.max(vals, axis=-1, keepdims=True)
    i1 = jnp.min(jnp.where(vals == m1, lane, width), axis=-1, keepdims=True)
    rest = jnp.where(lane == i1, -2.0, vals)
    m2 = jnp.max(rest, axis=-1, keepdims=True)
    i2 = jnp.min(jnp.where(rest == m2, lane, width), axis=-1, keepdims=True)
    return m1, i1, m2, i2


def _router_kernel(x_ref, w_ref, b_ref, info_ref, cnt_ref, carry):
    @pl.when(pl.program_id(0) == 0)
    def _():
        carry[...] = jnp.zeros_like(carry)

    logits = jnp.dot(x_ref[...], w_ref[...], precision=lax.Precision.HIGHEST, preferred_element_type=f32)
    logits = logits + b_ref[...]
    e = jnp.exp(logits - jnp.max(logits, axis=-1, keepdims=True))
    aff = e / jnp.sum(e, axis=-1, keepdims=True)
    tm = aff.shape[0]
    lane = lax.broadcasted_iota(jnp.int32, aff.shape, 1)
    lane_grp = lax.shift_right_logical(lane, int(math.log2(EXPERTS_PER_GROUP)))

    best, grp = None, None
    for gidx in range(N_GROUPS):
        m1, _, m2, _ = _top2(jnp.where(lane_grp == gidx, aff, -1.0), lane)
        score = m1 + m2
        if gidx == 0:
            best, grp = score, jnp.zeros_like(lane[:, :1])
        else:
            better = score > best
            grp = jnp.where(better, gidx, grp)
            best = jnp.where(better, score, best)
    m1, i1, m2, i2 = _top2(jnp.where(lane_grp == grp, aff, -1.0), lane)
    den = m1 + m2

    hot1 = (lane == i1).astype(f32)
    hot2 = (lane == i2).astype(f32)
    both = hot1 + hot2
    row = lax.broadcasted_iota(jnp.int32, (tm, tm), 0)
    col = lax.broadcasted_iota(jnp.int32, (tm, tm), 1)
    before = _dot((col < row).astype(bf16), both.astype(bf16)) + carry[...]
    rank1 = jnp.sum(hot1 * before, axis=-1, keepdims=True)
    rank2 = jnp.sum(hot2 * before, axis=-1, keepdims=True)
    carry[...] += jnp.sum(both, axis=0, keepdims=True)
    cnt_ref[...] = carry[...]

    cols = (i1.astype(f32), i2.astype(f32), m1 / den, m2 / den, rank1, rank2)
    info = jnp.zeros(aff.shape, f32)
    for c, v in enumerate(cols):
        info = jnp.where(lane == c, v, info)
    info_ref[...] = info


def moe_router(x, router_w, router_b, *, tm):
    n, d = x.shape
    tm = min(tm, n)
    assert n % tm == 0
    return pl.pallas_call(
        _router_kernel,
        grid=(n // tm,),
        in_specs=[
            pl.BlockSpec((tm, d), lambda i: (i, 0)),
            pl.BlockSpec((d, N_EXPERTS), lambda i: (0, 0)),
            pl.BlockSpec((1, N_EXPERTS), lambda i: (0, 0)),
        ],
        out_specs=[pl.BlockSpec((tm, N_EXPERTS), lambda i: (i, 0)), pl.BlockSpec((1, N_EXPERTS), lambda i: (0, 0))],
        out_shape=[jax.ShapeDtypeStruct((n, N_EXPERTS), f32), jax.ShapeDtypeStruct((1, N_EXPERTS), f32)],
        scratch_shapes=[pltpu.VMEM((1, N_EXPERTS), f32)],
        compiler_params=_params("arbitrary"),
        name="moe_router",
    )(x, router_w, router_b)


def _moe_kernel(te_ref, nt_ref, x_ref, cw_ref, wg_ref, wu_ref, wd_ref, o_ref):
    live = pl.program_id(0) < nt_ref[0]

    @pl.when(live)
    def _():
        x = x_ref[...]
        h = jax.nn.silu(_dot(x, wg_ref[0])) * _dot(x, wu_ref[0])
        o_ref[...] = _dot((h * cw_ref[...]).astype(bf16), wd_ref[0])

    @pl.when(jnp.logical_not(live))
    def _():
        o_ref[...] = jnp.zeros_like(o_ref)


def moe_experts(xs, cw, tile_expert, n_live, w_gate, w_up, w_down, *, tm):
    p, d = xs.shape
    fdim = w_gate.shape[2]
    grid_spec = pltpu.PrefetchScalarGridSpec(
        num_scalar_prefetch=2,
        grid=(p // tm,),
        in_specs=[
            pl.BlockSpec((tm, d), lambda i, te, nt: (i, 0)),
            pl.BlockSpec((tm, 1), lambda i, te, nt: (i, 0)),
            pl.BlockSpec((1, d, fdim), lambda i, te, nt: (te[i], 0, 0)),
            pl.BlockSpec((1, d, fdim), lambda i, te, nt: (te[i], 0, 0)),
            pl.BlockSpec((1, fdim, d), lambda i, te, nt: (te[i], 0, 0)),
        ],
        out_specs=pl.BlockSpec((tm, d), lambda i, te, nt: (i, 0)),
    )
    return pl.pallas_call(
        _moe_kernel,
        grid_spec=grid_spec,
        out_shape=jax.ShapeDtypeStruct((p, d), f32),
        compiler_params=_params("arbitrary"),
        name="moe_experts",
    )(tile_expert, n_live, xs, cw, w_gate, w_up, w_down)


def moe_layer(x, router_w, router_b, w_gate, w_up, w_down, *, tm):
    n = x.shape[0]
    info, counts = moe_router(x, router_w, router_b, tm=512)
    e1, e2 = info[:, 0].astype(jnp.int32), info[:, 1].astype(jnp.int32)
    r1, r2 = info[:, 4].astype(jnp.int32), info[:, 5].astype(jnp.int32)
    cnt = counts[0].astype(jnp.int32)
    padded = (cnt + tm - 1) // tm * tm
    ends = jnp.cumsum(padded)
    starts = ends - padded
    d1, d2 = starts[e1] + r1, starts[e2] + r2
    n_tiles = -(-2 * n // tm) + N_EXPERTS
    p = n_tiles * tm
    tok = jnp.arange(n, dtype=jnp.int32)
    src = jnp.zeros((p,), jnp.int32).at[d1].set(tok).at[d2].set(tok)
    cw = jnp.zeros((p,), f32).at[d1].set(info[:, 2]).at[d2].set(info[:, 3])
    n_live = ends[-1:] // tm
    tile_ix = jnp.arange(n_tiles, dtype=jnp.int32)
    tile_expert = jnp.searchsorted(ends, jnp.minimum(tile_ix, n_live[0] - 1) * tm, side="right").astype(jnp.int32)
    xs = x.astype(bf16)[src]
    ys = moe_experts(xs, cw[:, None], tile_expert, n_live.astype(jnp.int32), w_gate, w_up, w_down, tm=tm)
    return ys[d1] + ys[d2]


def _ln_ple_kernel(x_ref, y_ref, g_ref, b_ref, p_ref, wg_ref, wp_ref, o_ref):
    x2 = _layer_norm(DN_ALPHA * x_ref[...] + y_ref[...], g_ref[...], b_ref[...])
    gate = jax.nn.sigmoid(_dot(x2.astype(bf16), wg_ref[...]))
    o_ref[...] = x2 + gate * _dot(p_ref[...].astype(bf16), wp_ref[...])


def ln_ple(x, y, g, b, p, w_gate, w_proj, *, tm):
    n, d = x.shape
    tm = min(tm, n)
    assert n % tm == 0
    pd = p.shape[1]
    row = lambda i: (i, 0)
    fixed = lambda i: (0, 0)
    return pl.pallas_call(
        _ln_ple_kernel,
        grid=(n // tm,),
        in_specs=[
            pl.BlockSpec((tm, d), row), pl.BlockSpec((tm, d), row),
            pl.BlockSpec((1, d), fixed), pl.BlockSpec((1, d), fixed),
            pl.BlockSpec((tm, pd), row), pl.BlockSpec((d, d), fixed), pl.BlockSpec((pd, d), fixed),
        ],
        out_specs=pl.BlockSpec((tm, d), row),
        out_shape=jax.ShapeDtypeStruct((n, d), f32),
        compiler_params=_params("parallel"),
        name="ln_ple",
    )(x, y, g, b, p, w_gate, w_proj)


def _retention_decode_kernel(q_ref, k_ref, v_ref, g_ref, gn_ref, dec_ref, s_ref, o_ref, so_ref):
    q = q_ref[0]
    k = k_ref[0] * (RET_DK ** -0.5)
    v = v_ref[0]
    gamma = dec_ref[0]
    s_prev = s_ref[0, 0]
    att = jnp.sum(q.astype(f32) * k.astype(f32), axis=-1, keepdims=True)
    rows = 8
    q8 = jnp.broadcast_to(q, (rows, RET_DK))
    first = (lax.broadcasted_iota(jnp.int32, (rows, 1), 0) == 0).astype(f32)
    k8 = (jnp.broadcast_to(k.astype(f32), (rows, RET_DK)) * first).astype(bf16)
    v8 = jnp.broadcast_to(v, (rows, RET_DV))
    o = att.astype(bf16).astype(f32) * v.astype(f32) + _dot(q8, s_prev.astype(bf16))[:1] * gamma
    so_ref[0, 0] = s_prev * gamma + _dot_tn(k8, v8)
    mu = jnp.mean(o, axis=-1, keepdims=True)
    d = o - mu
    var = jnp.mean(d * d, axis=-1, keepdims=True)
    o = d * lax.rsqrt(var + LN_EPS) * gn_ref[...]
    o_ref[0] = (jax.nn.silu(g_ref[0].astype(f32)) * o).astype(o_ref.dtype)


def retention_decode(qk, vg, gn_g, states, base):
    b = qk.shape[0]
    h = RET_HEADS
    _, qdec, _ = _retention_decay_tables(1)
    qk3, vg3 = qk[:, None, :], vg[:, None, :]
    o, s = pl.pallas_call(
        _retention_decode_kernel,
        grid=(b, h),
        in_specs=[
            pl.BlockSpec((1, 1, RET_DK), lambda i, hh: (i, 0, hh)),
            pl.BlockSpec((1, 1, RET_DK), lambda i, hh: (i, 0, h + hh)),
            pl.BlockSpec((1, 1, RET_DV), lambda i, hh: (i, 0, hh)),
            pl.BlockSpec((1, 1, RET_DV), lambda i, hh: (i, 0, h + hh)),
            pl.BlockSpec((1, RET_DV), lambda i, hh: (0, hh)),
            pl.BlockSpec((1, 1, 1), lambda i, hh: (hh, 0, 0)),
            pl.BlockSpec((1, 1, RET_DK, RET_DV), lambda i, hh: (base + i, hh, 0, 0)),
        ],
        out_specs=[
            pl.BlockSpec((1, 1, RET_DV), lambda i, hh: (i, 0, hh)),
            pl.BlockSpec((1, 1, RET_DK, RET_DV), lambda i, hh: (i, hh, 0, 0)),
        ],
        out_shape=[jax.ShapeDtypeStruct((b, 1, h * RET_DV), bf16), jax.ShapeDtypeStruct((b, h, RET_DK, RET_DV), f32)],
        compiler_params=_params("parallel", "parallel"),
        name="retention_decode",
    )(qk3, qk3, vg3, vg3, gn_g, qdec, states)
    return o[:, 0, :], s


HIST_PAGES_PER_STEP = 32


def _compress_hist_kernel(pt_ref, *refs, n_pages):
    page_refs = refs[:n_pages]
    pe_ref, w1_ref, w2_ref, o_ref, carry = refs[n_pages:]
    upp = page_refs[0].shape[1] // CMP_STRIDE
    seg = n_pages * upp

    @pl.when(pl.program_id(3) == 0)
    def _():
        carry[...] = jnp.zeros_like(carry)

    def load(l):
        return jnp.concatenate([pr[0, pl.ds(l, upp, stride=CMP_STRIDE), :] for pr in page_refs], axis=0)

    a, b = _compress_units(load, pe_ref[0], w1_ref, seg)
    row = lax.broadcasted_iota(jnp.int32, (seg, 1), 0)
    a_prev = jnp.where(row == 0, carry[...], pltpu.roll(a, 1, 0))
    carry[...] = a[seg - 1:seg, :]
    o_ref[0, 0, 0] = _dot(jax.nn.gelu(a_prev + b).astype(bf16), w2_ref[0]).astype(o_ref.dtype)


def compress_history(pool, page_table, pe, w1, w2):
    b, ppb = page_table.shape
    page = pool.shape[1]
    g = NSA_KV
    n_pages = min(HIST_PAGES_PER_STEP, ppb)
    assert ppb % n_pages == 0 and page % CMP_STRIDE == 0
    upp = page // CMP_STRIDE
    seg = n_pages * upp
    nu = ppb * upp
    page_spec = lambda k: pl.BlockSpec(
        (1, page, NSA_HD), lambda i, s, gg, ch, pt: (pt[i, ch * n_pages + k], 0, s * g + gg))
    grid_spec = pltpu.PrefetchScalarGridSpec(
        num_scalar_prefetch=1,
        grid=(b, 2, g, ppb // n_pages),
        in_specs=[page_spec(k) for k in range(n_pages)] + [
            pl.BlockSpec((1, CMP_BLOCK, NSA_HD), lambda i, s, gg, ch, pt: (s, 0, 0)),
            pl.BlockSpec((1, CMP_BLOCK * NSA_HD, CMP_HIDDEN), lambda i, s, gg, ch, pt: (s, 0, 0)),
            pl.BlockSpec((1, CMP_HIDDEN, NSA_HD), lambda i, s, gg, ch, pt: (s, 0, 0)),
        ],
        out_specs=pl.BlockSpec((1, 1, 1, seg, NSA_HD), lambda i, s, gg, ch, pt: (i, s, gg, ch, 0)),
        scratch_shapes=[pltpu.VMEM((1, CMP_HIDDEN), f32)],
    )
    return pl.pallas_call(
        functools.partial(_compress_hist_kernel, n_pages=n_pages),
        grid_spec=grid_spec,
        out_shape=jax.ShapeDtypeStruct((b, 2, g, nu, NSA_HD), bf16),
        compiler_params=_params("parallel", "parallel", "parallel", "arbitrary"),
        name="compress_history",
    )(page_table, *([pool] * n_pages), pe, w1, w2)


def _nsa_select_decode_kernel(q_ref, kc_ref, vc_ref, msel_ref, oc_ref, top_ref, *, t, ns, n_top):
    r_heads, hd = NSA_HPG, NSA_HD
    q = q_ref[0]
    qb = jnp.concatenate([q[:, r * hd:(r + 1) * hd] for r in range(r_heads)], axis=0)
    kc = kc_ref[0, 0, 0]
    nu = kc.shape[0]
    s = _dot_nt(qb, kc) * NSA_SCALE
    u = lax.broadcasted_iota(jnp.int32, (1, nu), 1)
    p = _masked_softmax(s, (u >= 1) & (u * CMP_STRIDE + (CMP_STRIDE - 1) <= t))
    oc_ref[0, 0] = _dot(p.astype(bf16), vc_ref[0, 0, 0])
    imp = jnp.sum(p, axis=0, keepdims=True)
    sel = jnp.dot(imp, msel_ref[...], precision=lax.Precision.HIGHEST, preferred_element_type=f32)
    jidx = lax.broadcasted_iota(jnp.int32, sel.shape, 1)
    cur = t // SEL_BLOCK
    forced = (jidx == 0) | (jidx == cur) | (jidx == cur - 1)
    sel = jnp.where(jidx * SEL_BLOCK <= t, jnp.where(forced, FORCE_SCORE, sel), NEG)
    sel = jnp.where(jidx < ns, sel, -jnp.inf)
    width = sel.shape[-1]
    lane = lax.broadcasted_iota(jnp.int32, (1, top_ref.shape[-1]), 1)
    top = jnp.zeros(lane.shape, jnp.int32)
    for it in range(n_top):
        mx = jnp.max(sel, axis=-1, keepdims=True)
        idx = jnp.min(jnp.where(sel == mx, jidx, width), axis=-1, keepdims=True)
        top = jnp.where(lane == it, idx, top)
        sel = jnp.where(jidx == idx, -jnp.inf, sel)
    top_ref[0, 0] = top


def nsa_select_decode(q, cmp, t):
    b = q.shape[0]
    g = NSA_KV
    nu = cmp.shape[3]
    ns = (t + 1 + SEL_BLOCK - 1) // SEL_BLOCK
    ns_pad = -(-ns // LANES) * LANES
    per_sel = SEL_BLOCK // CMP_STRIDE
    u = jnp.arange(nu)[:, None]
    j = jnp.arange(ns_pad)[None, :]
    rb = CMP_BLOCK // CMP_STRIDE
    msel = sum(((u - 1 + r) // per_sel == j).astype(f32) for r in range(rb)) / rb
    msel = jnp.where(u >= 1, msel, 0.0)
    gw = NSA_HPG * NSA_HD
    n_top = min(N_SEL, ns)
    cmp_spec = lambda s: pl.BlockSpec((1, 1, 1, nu, NSA_HD), lambda i, gg: (i, s, gg, 0, 0))
    return pl.pallas_call(
        functools.partial(_nsa_select_decode_kernel, t=t, ns=ns, n_top=n_top),
        grid=(b, g),
        in_specs=[
            pl.BlockSpec((1, 1, gw), lambda i, gg: (i, 0, gg)),
            cmp_spec(0), cmp_spec(1),
            pl.BlockSpec((nu, ns_pad), lambda i, gg: (0, 0)),
        ],
        out_specs=[
            pl.BlockSpec((1, 1, NSA_HPG, NSA_HD), lambda i, gg: (i, gg, 0, 0)),
            pl.BlockSpec((1, 1, 1, LANES), lambda i, gg: (i, gg, 0, 0)),
        ],
        out_shape=[jax.ShapeDtypeStruct((b, g, NSA_HPG, NSA_HD), f32), jax.ShapeDtypeStruct((b, g, 1, LANES), jnp.int32)],
        compiler_params=_params("parallel", "parallel"),
        name="nsa_select_decode",
    )(q[:, None, :], cmp, cmp, msel)


def _nsa_attn_decode_kernel(pt_ref, top_ref, *refs, n_top, n_hist_blocks):
    k_refs, v_refs = refs[:n_top], refs[n_top:2 * n_top]
    (q_ref, oc_ref, knew_ref, vnew_ref, kw_ref, vw_ref, kwnew_ref, vwnew_ref, gl_ref, o_ref) = refs[2 * n_top:]
    i, gg = pl.program_id(0), pl.program_id(1)
    r_heads, hd = NSA_HPG, NSA_HD
    q = q_ref[0]
    qb = jnp.concatenate([q[:, r * hd:(r + 1) * hd] for r in range(r_heads)], axis=0)

    def attend(keys, vals, mask, k_new, v_new):
        s = jnp.where(mask, _dot_nt(qb, keys) * NSA_SCALE, NEG)
        s_new = jnp.sum(qb.astype(f32) * k_new.astype(f32), axis=-1, keepdims=True) * NSA_SCALE
        m = jnp.maximum(jnp.max(s, axis=-1, keepdims=True), s_new)
        e = jnp.where(mask, jnp.exp(s - m), 0.0)
        e_new = jnp.exp(s_new - m)
        num = _dot(e.astype(bf16), vals) + e_new.astype(bf16).astype(f32) * v_new.astype(f32)
        return num / (jnp.sum(e, axis=-1, keepdims=True) + e_new)

    sb = k_refs[0].shape[1]
    keys = jnp.concatenate([r[0].astype(bf16) for r in k_refs], axis=0)
    vals = jnp.concatenate([r[0].astype(bf16) for r in v_refs], axis=0)
    blk_of_lane = lax.broadcasted_iota(jnp.int32, (1, n_top * sb), 1) // sb
    sel_of_lane = jnp.zeros((1, n_top * sb), jnp.int32)
    for n in range(n_top):
        sel_of_lane = jnp.where(blk_of_lane == n, top_ref[i, gg, n], sel_of_lane)
    o_s = attend(keys, vals, sel_of_lane < n_hist_blocks, knew_ref[0], vnew_ref[0])

    wlen = kw_ref.shape[1]
    wmask = lax.broadcasted_iota(jnp.int32, (1, wlen), 1) >= 1
    o_w = attend(kw_ref[0].astype(bf16), vw_ref[0].astype(bf16), wmask, kwnew_ref[0], vwnew_ref[0])

    gates = jax.nn.sigmoid(gl_ref[0])
    o_c = oc_ref[0, 0]
    outs = []
    for r in range(r_heads):
        outs.append(gates[:, 3 * r:3 * r + 1] * o_c[r:r + 1] + gates[:, 3 * r + 1:3 * r + 2] * o_s[r:r + 1]
                    + gates[:, 3 * r + 2:3 * r + 3] * o_w[r:r + 1])
    o_ref[0] = jnp.concatenate(outs, axis=1).astype(o_ref.dtype)


def nsa_attn_decode(q, o_c, top, kvb, gl, pool, page_table, win, win_base, t):
    b = q.shape[0]
    g = NSA_KV
    n_top = top.shape[-1]
    page = pool.shape[1]
    assert t % SEL_BLOCK == 0 and page % SEL_BLOCK == 0 and win.shape[1] == WINDOW
    n_hist_blocks = t // SEL_BLOCK
    bpp = page // SEL_BLOCK
    pool_b = pool.reshape(pool.shape[0] * bpp, SEL_BLOCK, pool.shape[2])
    gw = NSA_HPG * NSA_HD

    def blk_spec(n, slot):
        def imap(i, gg, pt, tp):
            j = jnp.minimum(tp[i, gg, n], n_hist_blocks - 1)
            return (pt[i, j // bpp] * bpp + j % bpp, 0, slot * g + gg)
        return pl.BlockSpec((1, SEL_BLOCK, NSA_HD), imap)

    new_spec = lambda slot: pl.BlockSpec((1, 1, NSA_HD), lambda i, gg, pt, tp: (i, 0, slot * g + gg))
    win_spec = lambda slot: pl.BlockSpec((1, WINDOW, NSA_HD),
                                         lambda i, gg, pt, tp: (win_base + i, 0, slot * g + gg))
    kvb3 = kvb[:, None, :]
    grid_spec = pltpu.PrefetchScalarGridSpec(
        num_scalar_prefetch=2,
        grid=(b, g),
        in_specs=[blk_spec(n, 2) for n in range(n_top)] + [blk_spec(n, 3) for n in range(n_top)] + [
            pl.BlockSpec((1, 1, gw), lambda i, gg, pt, tp: (i, 0, gg)),
            pl.BlockSpec((1, 1, NSA_HPG, NSA_HD), lambda i, gg, pt, tp: (i, gg, 0, 0)),
            new_spec(2), new_spec(3), win_spec(0), win_spec(1), new_spec(4), new_spec(5),
            pl.BlockSpec((1, 1, LANES), lambda i, gg, pt, tp: (i, 0, gg)),
        ],
        out_specs=pl.BlockSpec((1, 1, gw), lambda i, gg, pt, tp: (i, 0, gg)),
    )
    o = pl.pallas_call(
        functools.partial(_nsa_attn_decode_kernel, n_top=n_top, n_hist_blocks=n_hist_blocks),
        grid_spec=grid_spec,
        out_shape=jax.ShapeDtypeStruct((b, 1, NSA_HEADS * NSA_HD), bf16),
        compiler_params=_params("parallel", "parallel"),
        name="nsa_attn_decode",
    )(page_table, top, *([pool_b] * (2 * n_top)), q[:, None, :], o_c, kvb3, kvb3, win, win, kvb3, kvb3,
      gl[:, None, :])
    return o[:, 0, :]


def _retention_mixer(x, pos, batch, w, states, base):
    cos, sin = _rope_tables(pos, RET_DK)
    tn = 2 * RET_DK
    flags = jnp.ones((w["qk"].shape[1] // tn,), jnp.int32)
    qk, = mm_rope(x, w["qk"], cos, sin, flags, [bf16], hd=RET_DK, tm=512, tn=tn)
    vg = mm(x, w["vg"], bf16, tm=512, tn=512)
    if states is None:
        return retention_prompt(qk, vg, w["gn"], batch)
    return retention_decode(qk, vg, w["gn"], states, base)


def _nsa_projections(x, pos, w):
    cos, sin = _rope_tables(pos, NSA_HD)
    tn = NSA_KV * NSA_HD
    q, = mm_rope(x, w["q"], cos, sin, jnp.ones((NSA_HEADS * NSA_HD // tn,), jnp.int32), [bf16],
                 hd=NSA_HD, tm=512, tn=tn)
    kvf, kvb = mm_rope(x, w["kv"], cos, sin, jnp.array([1, 0] * 3, jnp.int32), [f32, bf16],
                       hd=NSA_HD, tm=512, tn=tn)
    gl = mm(x, w["gl"], f32, tm=512, tn=tn)
    return q, kvf, kvb, gl


def _layer_tail(x, h, p, w, i, tm_moe):
    x1 = mm_ln(h, w["out"], x, w["ln_g"][0:1], w["ln_b"][0:1], tm=256, tk=2048)
    y = moe_layer(x1, w["router_w"], w["router_b"], w["moe_gate"], w["moe_up"], w["moe_down"], tm=tm_moe)
    return ln_ple(x1, y, w["ln_g"][1:2], w["ln_b"][1:2], p, w["ple_gate"], w["ple_proj"], tm=256)


def kernel(x_prompt, x_sample, state_ret, cache_nsa_kv, state_nsa_win, page_table, p_prompt, p_sample, ret_w_in, ret_w_out, ret_gn_g, nsa_w_in, nsa_w_out, nsa_cmp_pos, nsa_cmp_w1, nsa_cmp_w2, ln_g, ln_b, router_w, router_b, moe_w_gate, moe_w_up, moe_w_down, ple_w_gate, ple_w_proj):
    bp, tp, d = x_prompt.shape
    bs, ts, _ = x_sample.shape
    assert ts == 1
    n_pool, page = cache_nsa_kv.shape[1], cache_nsa_kv.shape[2]
    past = page_table.shape[1] * page
    g, hd = NSA_KV, NSA_HD
    kv_cols = N_KV_SLOTS * g * hd

    xp = x_prompt.reshape(bp * tp, d)
    xs = x_sample.reshape(bs * ts, d)
    pos_p = jnp.arange(tp, dtype=jnp.int32)
    pos_s = jnp.full((bs,), past, jnp.int32)
    states = state_ret.reshape((-1,) + state_ret.shape[2:])
    pool = cache_nsa_kv.reshape(-1, page, kv_cols)
    wins = state_nsa_win.reshape(-1, state_nsa_win.shape[2], 2 * g * hd)

    ret_p, ret_s, kv_p, kv_s, win_p, win_s = [], [], [], [], [], []
    for i in range(DEPTH):
        j = i // 2
        w = {
            "ln_g": ln_g[i], "ln_b": ln_b[i],
            "router_w": router_w, "router_b": router_b[None, :],
            "moe_gate": moe_w_gate[i].astype(bf16), "moe_up": moe_w_up[i].astype(bf16),
            "moe_down": moe_w_down[i].astype(bf16),
            "ple_gate": ple_w_gate[i].astype(bf16), "ple_proj": ple_w_proj[i].astype(bf16),
        }
        if i % 2 == 0:
            hk = RET_HEADS * RET_DK
            w.update(qk=ret_w_in[j][:, :2 * hk].astype(bf16), vg=ret_w_in[j][:, 2 * hk:].astype(bf16),
                     out=ret_w_out[j].astype(bf16), gn=ret_gn_g[j][None, :])
            hp, sp = _retention_mixer(xp, pos_p, bp, w, None, 0)
            hs, ss = _retention_mixer(xs, pos_s, bs, w, states, j * bs)
            ret_p.append(sp)
            ret_s.append(ss)
        else:
            nq = NSA_HEADS * hd
            w_in = nsa_w_in[j]
            gl = w_in[:, nq + 6 * g * hd:].reshape(d, g, NSA_HPG * 3)
            gl = jnp.pad(gl, ((0, 0), (0, 0), (0, LANES - NSA_HPG * 3))).reshape(d, g * LANES)
            w.update(q=w_in[:, :nq].astype(bf16), kv=w_in[:, nq:nq + 6 * g * hd].astype(bf16), gl=gl.astype(bf16),
                     out=nsa_w_out[j].astype(bf16))
            pe = nsa_cmp_pos[j]
            w1 = nsa_cmp_w1[j].reshape(2, CMP_BLOCK * hd, CMP_HIDDEN).astype(bf16)
            w2 = nsa_cmp_w2[j].astype(bf16)
            q, kvf, kvb, glp = _nsa_projections(xp, pos_p, w)
            cmp = compress_prompt(kvf, pe, w1, w2, bp)
            hp = nsa_attn_prompt(q, cmp, kvb, glp, bp)
            kv_p.append(kvf[:, :kv_cols].reshape(bp, tp, N_KV_SLOTS, g, hd))
            keep = min(WINDOW, tp)
            win_p.append(kvf.reshape(bp, tp, -1)[:, tp - keep:, kv_cols:].reshape(bp, keep, 2, g, hd))
            q, kvf, kvb, gls = _nsa_projections(xs, pos_s, w)
            pt = page_table + j * n_pool
            cmp = compress_history(pool, pt, pe, w1, w2)
            o_c, top = nsa_select_decode(q, cmp, past)
            top = top[:, :, 0, :min(N_SEL, past // SEL_BLOCK + 1)]
            hs = nsa_attn_decode(q, o_c, top, kvb, gls, pool, pt, wins, j * bs, past)
            kv_s.append(kvf[:, :kv_cols].reshape(bs, ts, N_KV_SLOTS, g, hd))
            new_win = kvf[:, kv_cols:].reshape(bs, ts, 2, g, hd)
            win_s.append(jnp.concatenate([state_nsa_win[j][:, ts:], new_win], axis=1))
        xp = _layer_tail(xp, hp, p_prompt[i].reshape(bp * tp, -1), w, i, MOE_TILE)
        xs = _layer_tail(xs, hs, p_sample[i].reshape(bs * ts, -1), w, i, 16)

    return (xp.reshape(bp, tp, d), xs.reshape(bs, ts, d), jnp.stack(ret_p), jnp.stack(ret_s),
            jnp.stack(kv_p), jnp.stack(kv_s), jnp.stack(win_p), jnp.stack(win_s))
```

```python
import functools
import math

import jax
import jax.numpy as jnp
from jax import lax
from jax.experimental import pallas as pl
from jax.experimental.pallas import tpu as pltpu

f32 = jnp.float32
bf16 = jnp.bfloat16

D_MODEL = 2048
DEPTH = 4
RET_HEADS = 8
RET_DK = D_MODEL // RET_HEADS
RET_DV = 2 * RET_DK
RET_CHUNK = 128
NSA_HEADS = 16
NSA_HD = D_MODEL // NSA_HEADS
NSA_KV = 4
NSA_HPG = NSA_HEADS // NSA_KV
CMP_BLOCK = 32
CMP_STRIDE = 16
CMP_HIDDEN = 4 * NSA_HD
SEL_BLOCK = 64
N_SEL = 16
WINDOW = 512
Q_BLOCK = 128
N_KV_SLOTS = 4
N_EXPERTS = 16
N_GROUPS = 4
EXPERTS_PER_GROUP = N_EXPERTS // N_GROUPS
D_EXPERT = 1408
PLE_DIM = 256
ROPE_THETA = 10000.0
LN_EPS = 1e-5
DN_ALPHA = (2 * DEPTH) ** 0.25
NEG = -1e30
FORCE_SCORE = float(NSA_HPG + 1)
NSA_SCALE = NSA_HD ** -0.5

V7X_VMEM_BYTES = 64 * 1024 * 1024
VMEM_LIMIT = V7X_VMEM_BYTES - 8 * 1024 * 1024
LANES = 128
MOE_TILE = 256


def _params(*sem):
    return pltpu.CompilerParams(dimension_semantics=sem, vmem_limit_bytes=VMEM_LIMIT)


def _layer_norm(v, g, b):
    mu = jnp.mean(v, axis=-1, keepdims=True)
    d = v - mu
    var = jnp.mean(d * d, axis=-1, keepdims=True)
    return d * lax.rsqrt(var + LN_EPS) * g + b


def _dot(a, b):
    return jnp.dot(a, b, preferred_element_type=f32)


def _dot_nt(a, b):
    return lax.dot_general(a, b, (((1,), (1,)), ((), ())), preferred_element_type=f32)


def _dot_tn(a, b):
    return lax.dot_general(a, b, (((0,), (0,)), ((), ())), preferred_element_type=f32)


def _rotate_half(v, hd):
    pieces = []
    for c in range(0, v.shape[1], hd):
        if hd == 2 * LANES:
            pieces += [v[:, c + LANES:c + hd], v[:, c:c + LANES]]
        else:
            pieces.append(pltpu.roll(v[:, c:c + hd], hd // 2, 1))
    return jnp.concatenate(pieces, axis=1)


def _mm_rope_kernel(flags_ref, x_ref, w_ref, cos_ref, sin_ref, *o_refs, hd):
    acc = _dot(x_ref[...].astype(bf16), w_ref[...])
    roped = flags_ref[pl.program_id(1)] == 1

    @pl.when(roped)
    def _():
        reps = acc.shape[1] // hd
        cos = jnp.concatenate([cos_ref[...]] * reps, axis=1)
        sin = jnp.concatenate([sin_ref[...]] * reps, axis=1)
        v = acc * cos + _rotate_half(acc, hd) * sin
        for o in o_refs:
            o[...] = v.astype(o.dtype)

    @pl.when(jnp.logical_not(roped))
    def _():
        for o in o_refs:
            o[...] = acc.astype(o.dtype)


def mm_rope(x, w, cos, sin, flags, out_dtypes, *, hd, tm, tn):
    n, k = x.shape
    m = w.shape[1]
    t = cos.shape[0]
    tm = min(tm, n)
    assert n % tm == 0 and m % tn == 0 and t % tm == 0 and tn % hd == 0
    tper = t // tm
    grid_spec = pltpu.PrefetchScalarGridSpec(
        num_scalar_prefetch=1,
        grid=(n // tm, m // tn),
        in_specs=[
            pl.BlockSpec((tm, k), lambda i, j, f: (i, 0)),
            pl.BlockSpec((k, tn), lambda i, j, f: (0, j)),
            pl.BlockSpec((tm, hd), lambda i, j, f: (i % tper, 0)),
            pl.BlockSpec((tm, hd), lambda i, j, f: (i % tper, 0)),
        ],
        out_specs=[pl.BlockSpec((tm, tn), lambda i, j, f: (i, j)) for _ in out_dtypes],
    )
    return pl.pallas_call(
        functools.partial(_mm_rope_kernel, hd=hd),
        grid_spec=grid_spec,
        out_shape=[jax.ShapeDtypeStruct((n, m), d) for d in out_dtypes],
        compiler_params=_params("parallel", "arbitrary"),
        name="mm_rope",
    )(flags, x, w, cos, sin)


def _mm_kernel(x_ref, w_ref, o_ref):
    o_ref[...] = _dot(x_ref[...].astype(bf16), w_ref[...]).astype(o_ref.dtype)


def mm(x, w, out_dtype, *, tm, tn):
    n, k = x.shape
    m = w.shape[1]
    tm = min(tm, n)
    tn = min(tn, m)
    assert n % tm == 0 and m % tn == 0
    return pl.pallas_call(
        _mm_kernel,
        grid=(n // tm, m // tn),
        in_specs=[pl.BlockSpec((tm, k), lambda i, j: (i, 0)), pl.BlockSpec((k, tn), lambda i, j: (0, j))],
        out_specs=pl.BlockSpec((tm, tn), lambda i, j: (i, j)),
        out_shape=jax.ShapeDtypeStruct((n, m), out_dtype),
        compiler_params=_params("parallel", "arbitrary"),
        name="mm",
    )(x, w)


def _mm_ln_kernel(a_ref, w_ref, x_ref, g_ref, b_ref, o_ref, acc_ref):
    kk = pl.program_id(1)

    @pl.when(kk == 0)
    def _():
        acc_ref[...] = jnp.zeros_like(acc_ref)

    acc_ref[...] += _dot(a_ref[...].astype(bf16), w_ref[...])

    @pl.when(kk == pl.num_programs(1) - 1)
    def _():
        o_ref[...] = _layer_norm(DN_ALPHA * x_ref[...] + acc_ref[...], g_ref[...], b_ref[...])


def _mm_ln_single_kernel(a_ref, w_ref, x_ref, g_ref, b_ref, o_ref):
    acc = _dot(a_ref[...].astype(bf16), w_ref[...])
    o_ref[...] = _layer_norm(DN_ALPHA * x_ref[...] + acc, g_ref[...], b_ref[...])


def mm_ln(a, w, x, g, b, *, tm, tk):
    n, k = a.shape
    d = w.shape[1]
    tm = min(tm, n)
    tk = min(tk, k)
    assert n % tm == 0 and k % tk == 0
    if tk == k:
        return pl.pallas_call(
            _mm_ln_single_kernel,
            grid=(n // tm,),
            in_specs=[
                pl.BlockSpec((tm, k), lambda i: (i, 0)),
                pl.BlockSpec((k, d), lambda i: (0, 0)),
                pl.BlockSpec((tm, d), lambda i: (i, 0)),
                pl.BlockSpec((1, d), lambda i: (0, 0)),
                pl.BlockSpec((1, d), lambda i: (0, 0)),
            ],
            out_specs=pl.BlockSpec((tm, d), lambda i: (i, 0)),
            out_shape=jax.ShapeDtypeStruct((n, d), f32),
            compiler_params=_params("parallel"),
            name="mm_ln",
        )(a, w, x, g, b)
    return pl.pallas_call(
        _mm_ln_kernel,
        grid=(n // tm, k // tk),
        in_specs=[
            pl.BlockSpec((tm, tk), lambda i, kk: (i, kk)),
            pl.BlockSpec((tk, d), lambda i, kk: (kk, 0)),
            pl.BlockSpec((tm, d), lambda i, kk: (i, 0)),
            pl.BlockSpec((1, d), lambda i, kk: (0, 0)),
            pl.BlockSpec((1, d), lambda i, kk: (0, 0)),
        ],
        out_specs=pl.BlockSpec((tm, d), lambda i, kk: (i, 0)),
        out_shape=jax.ShapeDtypeStruct((n, d), f32),
        scratch_shapes=[pltpu.VMEM((tm, d), f32)],
        compiler_params=_params("parallel", "arbitrary"),
        name="mm_ln",
    )(a, w, x, g, b)


def _retention_kernel(q_ref, k_ref, v_ref, g_ref, gn_ref, intra_ref, qdec_ref, kdec_ref, o_ref, s_ref, state):
    c = pl.program_id(2)

    @pl.when(c == 0)
    def _():
        state[...] = jnp.zeros_like(state)

    q = q_ref[...]
    k = k_ref[...] * (RET_DK ** -0.5)
    v = v_ref[...]
    qdec = qdec_ref[0]
    kdec = kdec_ref[0]
    cdec = qdec[-1:, :]
    s_prev = state[...]
    att = _dot_nt(q, k) * intra_ref[0]
    o = _dot(att.astype(bf16), v) + _dot(q, s_prev.astype(bf16)) * qdec
    kd = (k.astype(f32) * kdec).astype(bf16)
    s_new = s_prev * cdec + _dot_tn(kd, v)
    state[...] = s_new

    mu = jnp.mean(o, axis=-1, keepdims=True)
    d = o - mu
    var = jnp.mean(d * d, axis=-1, keepdims=True)
    o = d * lax.rsqrt(var + LN_EPS) * gn_ref[...]
    o_ref[...] = (jax.nn.silu(g_ref[...].astype(f32)) * o).astype(o_ref.dtype)

    @pl.when(c == pl.num_programs(2) - 1)
    def _():
        s_ref[0, 0] = s_new


def _retention_decay_tables(chunk):
    h = jnp.arange(RET_HEADS, dtype=f32)
    log_gamma = jnp.log1p(-jnp.exp2(-5.0 - h))
    idx = jnp.arange(chunk, dtype=f32)
    rel = idx[:, None] - idx[None, :]
    intra = jnp.where(rel >= 0, jnp.exp(jnp.maximum(rel, 0.0)[None] * log_gamma[:, None, None]), 0.0)
    qdec = jnp.exp((idx[None, :] + 1.0) * log_gamma[:, None])[..., None]
    kdec = jnp.exp((chunk - 1.0 - idx[None, :]) * log_gamma[:, None])[..., None]
    return intra, qdec, kdec


def retention_prompt(qk, vg, gn_g, batch):
    n = qk.shape[0]
    t = n // batch
    chunk = math.gcd(t, RET_CHUNK)
    nch = t // chunk
    h = RET_HEADS
    intra, qdec, kdec = _retention_decay_tables(chunk)
    row = lambda b, hh, c: b * nch + c
    return pl.pallas_call(
        _retention_kernel,
        grid=(batch, h, nch),
        in_specs=[
            pl.BlockSpec((chunk, RET_DK), lambda b, hh, c: (row(b, hh, c), hh)),
            pl.BlockSpec((chunk, RET_DK), lambda b, hh, c: (row(b, hh, c), h + hh)),
            pl.BlockSpec((chunk, RET_DV), lambda b, hh, c: (row(b, hh, c), hh)),
            pl.BlockSpec((chunk, RET_DV), lambda b, hh, c: (row(b, hh, c), h + hh)),
            pl.BlockSpec((1, RET_DV), lambda b, hh, c: (0, hh)),
            pl.BlockSpec((1, chunk, chunk), lambda b, hh, c: (hh, 0, 0)),
            pl.BlockSpec((1, chunk, 1), lambda b, hh, c: (hh, 0, 0)),
            pl.BlockSpec((1, chunk, 1), lambda b, hh, c: (hh, 0, 0)),
        ],
        out_specs=[
            pl.BlockSpec((chunk, RET_DV), lambda b, hh, c: (row(b, hh, c), hh)),
            pl.BlockSpec((1, 1, RET_DK, RET_DV), lambda b, hh, c: (b, hh, 0, 0)),
        ],
        out_shape=[
            jax.ShapeDtypeStruct((n, h * RET_DV), bf16),
            jax.ShapeDtypeStruct((batch, h, RET_DK, RET_DV), f32),
        ],
        scratch_shapes=[pltpu.VMEM((RET_DK, RET_DV), f32)],
        compiler_params=_params("parallel", "parallel", "arbitrary"),
        name="retention_prompt",
    )(qk, qk, vg, vg, gn_g, intra, qdec, kdec)


def _rope_tables(pos, hd):
    half = hd // 2
    inv = ROPE_THETA ** (-jnp.arange(half, dtype=f32) / half)
    ang = pos.astype(f32)[:, None] * inv[None, :]
    cos, sin = jnp.cos(ang), jnp.sin(ang)
    return jnp.concatenate([cos, cos], axis=1), jnp.concatenate([-sin, sin], axis=1)


def _compress_units(load_unit_row, pe, w1_ref):
    xa, xb = [], []
    for l in range(CMP_STRIDE):
        xl = load_unit_row(l)
        xa.append((xl + pe[l:l + 1]).astype(bf16))
        xb.append((xl + pe[CMP_STRIDE + l:CMP_STRIDE + l + 1]).astype(bf16))
    half = CMP_STRIDE * NSA_HD
    a = _dot(jnp.concatenate(xa, axis=1), w1_ref[0, :half])
    b = _dot(jnp.concatenate(xb, axis=1), w1_ref[0, half:])
    return a, b


def _compress_kernel(kv_ref, pe_ref, w1_ref, w2_ref, o_ref):
    nu = kv_ref.shape[0] // CMP_STRIDE
    a, b = _compress_units(lambda l: kv_ref[pl.ds(l, nu, stride=CMP_STRIDE), :], pe_ref[0], w1_ref)
    hid = a + pltpu.roll(b, nu - 1, 0)
    o_ref[0, 0, 0] = _dot(jax.nn.gelu(hid).astype(bf16), w2_ref[0]).astype(o_ref.dtype)


def compress_prompt(kvf, pe, w1, w2, batch):
    n = kvf.shape[0]
    t = n // batch
    nu = t // CMP_STRIDE
    g = NSA_KV
    return pl.pallas_call(
        _compress_kernel,
        grid=(batch, 2, g),
        in_specs=[
            pl.BlockSpec((t, NSA_HD), lambda b, s, gg: (b, s * g + gg)),
            pl.BlockSpec((1, CMP_BLOCK, NSA_HD), lambda b, s, gg: (s, 0, 0)),
            pl.BlockSpec((1, CMP_BLOCK * NSA_HD, CMP_HIDDEN), lambda b, s, gg: (s, 0, 0)),
            pl.BlockSpec((1, CMP_HIDDEN, NSA_HD), lambda b, s, gg: (s, 0, 0)),
        ],
        out_specs=pl.BlockSpec((1, 1, 1, nu, NSA_HD), lambda b, s, gg: (b, s, gg, 0, 0)),
        out_shape=jax.ShapeDtypeStruct((batch, 2, g, nu, NSA_HD), bf16),
        compiler_params=_params("parallel", "parallel", "parallel"),
        name="compress_prompt",
    )(kvf, pe, w1, w2)


def _select_blocks(sel, n_top, axis):
    ns = sel.shape[axis]
    jidx = lax.broadcasted_iota(jnp.int32, sel.shape, axis)
    chosen = jnp.zeros(sel.shape, f32)
    for _ in range(n_top):
        mx = jnp.max(sel, axis=axis, keepdims=True)
        idx = jnp.min(jnp.where(sel == mx, jidx, ns), axis=axis, keepdims=True)
        hit = jidx == idx
        chosen = jnp.where(hit, 1.0, chosen)
        sel = jnp.where(hit, -jnp.inf, sel)
    return chosen


def _masked_softmax(s, mask):
    s = jnp.where(mask, s, NEG)
    e = jnp.exp(s - jnp.max(s, axis=-1, keepdims=True))
    return jnp.where(mask, e / jnp.sum(e, axis=-1, keepdims=True), 0.0)


def _nsa_attn_kernel(q_ref, kc_ref, vc_ref, ks_ref, vs_ref, kw_ref, vw_ref, gl_ref, msel_ref, exp_ref,
                     o_ref, mexp, vs_t, vw_t, vc_t, *, n_top):
    qi = pl.program_id(2)
    blk = q_ref.shape[0]
    r_heads, hd = NSA_HPG, NSA_HD
    t_len = ks_ref.shape[0]
    nc = kc_ref.shape[3]
    ck = 2 * blk
    zscale = NSA_SCALE * math.log2(math.e)

    def transposed(ref_rows):
        return ref_rows.astype(f32).T.astype(bf16)

    @pl.when(qi == 0)
    def _():
        def tr(c, carry):
            off = pl.multiple_of(c * blk, blk)
            vs_t[:, pl.ds(off, blk)] = transposed(vs_ref[pl.ds(off, blk), :])
            vw_t[:, pl.ds(off, blk)] = transposed(vw_ref[pl.ds(off, blk), :])
            return carry

        lax.fori_loop(0, t_len // blk, tr, 0)
        for c in range(nc // blk):
            vc_t[:, c * blk:(c + 1) * blk] = transposed(vc_ref[0, 0, 0, c * blk:(c + 1) * blk, :])

    def head(x, r):
        return x[:, r * blk:(r + 1) * blk]

    def per_head(fn):
        return jnp.concatenate([fn(r) for r in range(r_heads)], axis=1)

    q = q_ref[...].astype(f32)
    q_t = per_head(lambda r: q[:, r * hd:(r + 1) * hd].T).astype(bf16)
    t_q = qi * blk + lax.broadcasted_iota(jnp.int32, (1, blk), 1)

    z = _dot(kc_ref[0, 0, 0], q_t) * zscale
    c_end = lax.broadcasted_iota(jnp.int32, (nc, 1), 0) * CMP_STRIDE + (CMP_BLOCK - 1)
    mc = c_end <= t_q
    z = per_head(lambda r: jnp.where(mc, head(z, r), NEG))
    e = jnp.exp2(z - jnp.max(z, axis=0, keepdims=True))
    p = e * (1.0 / jnp.sum(e, axis=0, keepdims=True))
    p = per_head(lambda r: jnp.where(mc, head(p, r), 0.0))
    oc_t = _dot(vc_t[...], p.astype(bf16))

    imp = head(p, 0)
    for r in range(1, r_heads):
        imp = imp + head(p, r)
    sel = jnp.dot(msel_ref[...], imp, precision=lax.Precision.HIGHEST, preferred_element_type=f32)
    jidx = lax.broadcasted_iota(jnp.int32, sel.shape, 0)
    cur = lax.shift_right_logical(t_q, int(math.log2(SEL_BLOCK)))
    forced = (jidx == 0) | (jidx == cur) | (jidx == cur - 1)
    sel = jnp.where(jidx * SEL_BLOCK <= t_q, jnp.where(forced, FORCE_SCORE, sel), NEG)
    mexp[...] = _dot(exp_ref[...], _select_blocks(sel, n_top, 0).astype(bf16))

    def flash(k_ref, vt_ref, n_chunks, chunk_of, mask_fn):
        def body(i, carry):
            m, l, acc = carry
            off = pl.multiple_of(chunk_of(i) * ck, ck)
            z = _dot(k_ref[pl.ds(off, ck), :], q_t) * zscale
            msk = mask_fn(off, off + lax.broadcasted_iota(jnp.int32, (ck, 1), 0))
            z = per_head(lambda r: jnp.where(msk, head(z, r), NEG))
            m_new = jnp.maximum(m, jnp.max(z, axis=0, keepdims=True))
            alpha = jnp.exp2(m - m_new)
            e = jnp.exp2(z - m_new)
            l = l * alpha + jnp.sum(e, axis=0, keepdims=True)
            return m_new, l, acc * alpha + _dot(vt_ref[:, pl.ds(off, ck)], e.astype(bf16))

        width = r_heads * blk
        init = (jnp.full((1, width), NEG, f32), jnp.zeros((1, width), f32), jnp.zeros((hd, width), f32))
        _, l, acc = lax.fori_loop(0, n_chunks, body, init)
        return acc * (1.0 / l)

    top = qi // (ck // blk)
    os_t = flash(ks_ref, vs_t, top + 1, lambda i: i,
                 lambda off, kpos: (mexp[pl.ds(off, ck), :] > 0.5) & (kpos <= t_q))
    n_win = jnp.minimum(top, -(-WINDOW // ck)) + 1
    ow_t = flash(kw_ref, vw_t, n_win, lambda i: top - i,
                 lambda off, kpos: (kpos <= t_q) & (kpos > t_q - WINDOW))

    g_t = jax.nn.sigmoid(gl_ref[...]).T
    outs = []
    for r in range(r_heads):
        o_r = (g_t[3 * r:3 * r + 1] * head(oc_t, r) + g_t[3 * r + 1:3 * r + 2] * head(os_t, r)
               + g_t[3 * r + 2:3 * r + 3] * head(ow_t, r))
        outs.append(o_r.T)
    o_ref[...] = jnp.concatenate(outs, axis=1).astype(o_ref.dtype)


def _selection_constants(nu, ns, nkeys):
    c = jnp.arange(nu)[None, :]
    j = jnp.arange(ns)[:, None]
    per_sel = SEL_BLOCK // CMP_STRIDE
    rb = CMP_BLOCK // CMP_STRIDE
    msel = sum(((c + r) // per_sel == j).astype(f32) for r in range(rb)) / rb
    msel = jnp.where(c < nu - rb + 1, msel, 0.0)
    expand = (jnp.arange(nkeys)[:, None] // SEL_BLOCK == jnp.arange(ns)[None, :]).astype(bf16)
    return msel, expand


def nsa_attn_prompt(q, cmp, kvb, gl, batch):
    n = q.shape[0]
    t = n // batch
    blk = math.gcd(t, Q_BLOCK)
    nq = t // blk
    g = NSA_KV
    nu = t // CMP_STRIDE
    ns = t // SEL_BLOCK
    msel, expand = _selection_constants(nu, ns, t)
    gw = NSA_HPG * NSA_HD
    kv_spec = lambda slot: pl.BlockSpec((t, NSA_HD), lambda b, gg, qi: (b, slot * g + gg))
    cmp_spec = lambda s: pl.BlockSpec((1, 1, 1, nu, NSA_HD), lambda b, gg, qi: (b, s, gg, 0, 0))
    return pl.pallas_call(
        functools.partial(_nsa_attn_kernel, n_top=min(N_SEL, ns)),
        grid=(batch, g, nq),
        in_specs=[
            pl.BlockSpec((blk, gw), lambda b, gg, qi: (b * nq + qi, gg)),
            cmp_spec(0), cmp_spec(1),
            kv_spec(2), kv_spec(3), kv_spec(4), kv_spec(5),
            pl.BlockSpec((blk, LANES), lambda b, gg, qi: (b * nq + qi, gg)),
            pl.BlockSpec((ns, nu), lambda b, gg, qi: (0, 0)),
            pl.BlockSpec((t, ns), lambda b, gg, qi: (0, 0)),
        ],
        out_specs=pl.BlockSpec((blk, gw), lambda b, gg, qi: (b * nq + qi, gg)),
        out_shape=jax.ShapeDtypeStruct((n, NSA_HEADS * NSA_HD), bf16),
        scratch_shapes=[pltpu.VMEM((t, blk), f32), pltpu.VMEM((NSA_HD, t), bf16), pltpu.VMEM((NSA_HD, t), bf16),
                        pltpu.VMEM((NSA_HD, nu), bf16)],
        compiler_params=_params("parallel", "parallel", "arbitrary"),
        name="nsa_attn_prompt",
    )(q, cmp, cmp, kvb, kvb, kvb, kvb, gl, msel, expand)


def _top2(vals, lane):
    width = vals.shape[-1]
    m1 = jnp.max(vals, axis=-1, keepdims=True)
    i1 = jnp.min(jnp.where(vals == m1, lane, width), axis=-1, keepdims=True)
    rest = jnp.where(lane == i1, -2.0, vals)
    m2 = jnp.max(rest, axis=-1, keepdims=True)
    i2 = jnp.min(jnp.where(rest == m2, lane, width), axis=-1, keepdims=True)
    return m1, i1, m2, i2


def _router_kernel(x_ref, w_ref, b_ref, info_ref, cnt_ref, carry):
    @pl.when(pl.program_id(0) == 0)
    def _():
        carry[...] = jnp.zeros_like(carry)

    logits = jnp.dot(x_ref[...], w_ref[...], precision=lax.Precision.HIGHEST, preferred_element_type=f32)
    logits = logits + b_ref[...]
    e = jnp.exp(logits - jnp.max(logits, axis=-1, keepdims=True))
    aff = e / jnp.sum(e, axis=-1, keepdims=True)
    tm = aff.shape[0]
    lane = lax.broadcasted_iota(jnp.int32, aff.shape, 1)
    lane_grp = lax.shift_right_logical(lane, int(math.log2(EXPERTS_PER_GROUP)))

    best, grp = None, None
    for gidx in range(N_GROUPS):
        m1, _, m2, _ = _top2(jnp.where(lane_grp == gidx, aff, -1.0), lane)
        score = m1 + m2
        if gidx == 0:
            best, grp = score, jnp.zeros_like(lane[:, :1])
        else:
            better = score > best
            grp = jnp.where(better, gidx, grp)
            best = jnp.where(better, score, best)
    m1, i1, m2, i2 = _top2(jnp.where(lane_grp == grp, aff, -1.0), lane)
    den = m1 + m2

    hot1 = (lane == i1).astype(f32)
    hot2 = (lane == i2).astype(f32)
    both = hot1 + hot2
    row = lax.broadcasted_iota(jnp.int32, (tm, tm), 0)
    col = lax.broadcasted_iota(jnp.int32, (tm, tm), 1)
    before = _dot((col < row).astype(bf16), both.astype(bf16)) + carry[...]
    rank1 = jnp.sum(hot1 * before, axis=-1, keepdims=True)
    rank2 = jnp.sum(hot2 * before, axis=-1, keepdims=True)
    carry[...] += jnp.sum(both, axis=0, keepdims=True)
    cnt_ref[...] = carry[...]

    cols = (i1.astype(f32), i2.astype(f32), m1 / den, m2 / den, rank1, rank2)
    info = jnp.zeros(aff.shape, f32)
    for c, v in enumerate(cols):
        info = jnp.where(lane == c, v, info)
    info_ref[...] = info


def moe_router(x, router_w, router_b, *, tm):
    n, d = x.shape
    tm = min(tm, n)
    assert n % tm == 0
    return pl.pallas_call(
        _router_kernel,
        grid=(n // tm,),
        in_specs=[
            pl.BlockSpec((tm, d), lambda i: (i, 0)),
            pl.BlockSpec((d, N_EXPERTS), lambda i: (0, 0)),
            pl.BlockSpec((1, N_EXPERTS), lambda i: (0, 0)),
        ],
        out_specs=[pl.BlockSpec((tm, N_EXPERTS), lambda i: (i, 0)), pl.BlockSpec((1, N_EXPERTS), lambda i: (0, 0))],
        out_shape=[jax.ShapeDtypeStruct((n, N_EXPERTS), f32), jax.ShapeDtypeStruct((1, N_EXPERTS), f32)],
        scratch_shapes=[pltpu.VMEM((1, N_EXPERTS), f32)],
        compiler_params=_params("arbitrary"),
        name="moe_router",
    )(x, router_w, router_b)


def _row_gather(src_hbm, dst, sem, index_of, n_rows):
    def body(r, carry):
        pltpu.make_async_copy(src_hbm.at[pl.ds(index_of(r), 1), :], dst.at[pl.ds(r, 1), :], sem).start()
        return carry

    lax.fori_loop(0, n_rows, body, 0)


def _row_gather_wait(src_hbm, dst, sem):
    pltpu.make_async_copy(src_hbm.at[pl.ds(0, dst.shape[0]), :], dst, sem).wait()


def _moe_kernel(te_ref, nt_ref, src_ref, x_hbm, cw_ref, wg_ref, wu_ref, wd_ref, o_ref, xbuf, sem):
    t = pl.program_id(0)
    n_live = nt_ref[0]
    tm = xbuf.shape[1]

    def start(tile, slot):
        _row_gather(x_hbm, xbuf.at[slot], sem.at[slot], lambda r: src_ref[tile * tm + r], tm)

    @pl.when(t == 0)
    def _():
        start(0, 0)

    @pl.when(t + 1 < n_live)
    def _():
        start(t + 1, (t + 1) % 2)

    @pl.when(t < n_live)
    def _():
        slot = t % 2
        _row_gather_wait(x_hbm, xbuf.at[slot], sem.at[slot])
        x = xbuf[slot].astype(bf16)
        h = jax.nn.silu(_dot(x, wg_ref[0])) * _dot(x, wu_ref[0])
        o_ref[...] = _dot((h * cw_ref[...]).astype(bf16), wd_ref[0])

    @pl.when(t >= n_live)
    def _():
        o_ref[...] = jnp.zeros_like(o_ref)


def moe_experts(x, src, cw, tile_expert, n_live, w_gate, w_up, w_down, *, tm):
    p = src.shape[0]
    d = x.shape[1]
    fdim = w_gate.shape[2]
    grid_spec = pltpu.PrefetchScalarGridSpec(
        num_scalar_prefetch=3,
        grid=(p // tm,),
        in_specs=[
            pl.BlockSpec(memory_space=pl.ANY),
            pl.BlockSpec((tm, 1), lambda i, te, nt, sr: (i, 0)),
            pl.BlockSpec((1, d, fdim), lambda i, te, nt, sr: (te[i], 0, 0)),
            pl.BlockSpec((1, d, fdim), lambda i, te, nt, sr: (te[i], 0, 0)),
            pl.BlockSpec((1, fdim, d), lambda i, te, nt, sr: (te[i], 0, 0)),
        ],
        out_specs=pl.BlockSpec((tm, d), lambda i, te, nt, sr: (i, 0)),
        scratch_shapes=[pltpu.VMEM((2, tm, d), f32), pltpu.SemaphoreType.DMA((2,))],
    )
    return pl.pallas_call(
        _moe_kernel,
        grid_spec=grid_spec,
        out_shape=jax.ShapeDtypeStruct((p, d), f32),
        compiler_params=_params("arbitrary"),
        name="moe_experts",
    )(tile_expert, n_live, src, x, cw, w_gate, w_up, w_down)


def moe_layer(x, router_w, router_b, w_gate, w_up, w_down, *, tm):
    n = x.shape[0]
    info, counts = moe_router(x, router_w, router_b, tm=512)
    e1, e2 = info[:, 0].astype(jnp.int32), info[:, 1].astype(jnp.int32)
    r1, r2 = info[:, 4].astype(jnp.int32), info[:, 5].astype(jnp.int32)
    cnt = counts[0].astype(jnp.int32)
    padded = (cnt + tm - 1) // tm * tm
    ends = jnp.cumsum(padded)
    starts = ends - padded
    d1, d2 = starts[e1] + r1, starts[e2] + r2
    n_tiles = -(-2 * n // tm) + N_EXPERTS
    p = n_tiles * tm
    tok = jnp.arange(n, dtype=jnp.int32)
    src = jnp.zeros((p,), jnp.int32).at[d1].set(tok).at[d2].set(tok)
    cw = jnp.zeros((p,), f32).at[d1].set(info[:, 2]).at[d2].set(info[:, 3])
    n_live = ends[-1:] // tm
    tile_ix = jnp.arange(n_tiles, dtype=jnp.int32)
    tile_start = jnp.minimum(tile_ix, n_live[0] - 1) * tm
    tile_expert = jnp.sum((ends[None, :] <= tile_start[:, None]).astype(jnp.int32), axis=1)
    ys = moe_experts(x, src, cw[:, None], tile_expert, n_live.astype(jnp.int32), w_gate, w_up, w_down, tm=tm)
    return ys, d1, d2


def _ln_ple_kernel(d1_ref, d2_ref, x_ref, ys_hbm, g_ref, b_ref, p_ref, wg_ref, wp_ref, o_ref, ybuf, sem):
    i = pl.program_id(0)
    tm = x_ref.shape[0]

    def start(tile, slot):
        _row_gather(ys_hbm, ybuf.at[slot, 0], sem.at[slot], lambda r: d1_ref[tile * tm + r], tm)
        _row_gather(ys_hbm, ybuf.at[slot, 1], sem.at[slot], lambda r: d2_ref[tile * tm + r], tm)

    @pl.when(i == 0)
    def _():
        start(0, 0)

    @pl.when(i + 1 < pl.num_programs(0))
    def _():
        start(i + 1, (i + 1) % 2)

    slot = i % 2
    _row_gather_wait(ys_hbm, ybuf.at[slot, 0], sem.at[slot])
    _row_gather_wait(ys_hbm, ybuf.at[slot, 1], sem.at[slot])
    y = ybuf[slot, 0] + ybuf[slot, 1]
    x2 = _layer_norm(DN_ALPHA * x_ref[...] + y, g_ref[...], b_ref[...])
    gate = jax.nn.sigmoid(_dot(x2.astype(bf16), wg_ref[...]))
    o_ref[...] = x2 + gate * _dot(p_ref[...].astype(bf16), wp_ref[...])


def ln_ple(x, ys, d1, d2, g, b, p, w_gate, w_proj, *, tm):
    n, d = x.shape
    tm = min(tm, n)
    assert n % tm == 0
    pd = p.shape[1]
    row = lambda i, a, c: (i, 0)
    fixed = lambda i, a, c: (0, 0)
    grid_spec = pltpu.PrefetchScalarGridSpec(
        num_scalar_prefetch=2,
        grid=(n // tm,),
        in_specs=[
            pl.BlockSpec((tm, d), row), pl.BlockSpec(memory_space=pl.ANY),
            pl.BlockSpec((1, d), fixed), pl.BlockSpec((1, d), fixed),
            pl.BlockSpec((tm, pd), row), pl.BlockSpec((d, d), fixed), pl.BlockSpec((pd, d), fixed),
        ],
        out_specs=pl.BlockSpec((tm, d), row),
        scratch_shapes=[pltpu.VMEM((2, 2, tm, d), f32), pltpu.SemaphoreType.DMA((2,))],
    )
    return pl.pallas_call(
        _ln_ple_kernel,
        grid_spec=grid_spec,
        out_shape=jax.ShapeDtypeStruct((n, d), f32),
        compiler_params=_params("arbitrary"),
        name="ln_ple",
    )(d1, d2, x, ys, g, b, p, w_gate, w_proj)


def _retention_decode_kernel(q_ref, k_ref, v_ref, g_ref, gn_ref, dec_ref, s_ref, o_ref, so_ref):
    q = q_ref[0]
    k = k_ref[0] * (RET_DK ** -0.5)
    v = v_ref[0]
    gamma = dec_ref[0]
    s_prev = s_ref[0, 0]
    att = jnp.sum(q.astype(f32) * k.astype(f32), axis=-1, keepdims=True)
    rows = 8
    q8 = jnp.broadcast_to(q, (rows, RET_DK))
    first = (lax.broadcasted_iota(jnp.int32, (rows, 1), 0) == 0).astype(f32)
    k8 = (jnp.broadcast_to(k.astype(f32), (rows, RET_DK)) * first).astype(bf16)
    v8 = jnp.broadcast_to(v, (rows, RET_DV))
    o = att.astype(bf16).astype(f32) * v.astype(f32) + _dot(q8, s_prev.astype(bf16))[:1] * gamma
    so_ref[0, 0] = s_prev * gamma + _dot_tn(k8, v8)
    mu = jnp.mean(o, axis=-1, keepdims=True)
    d = o - mu
    var = jnp.mean(d * d, axis=-1, keepdims=True)
    o = d * lax.rsqrt(var + LN_EPS) * gn_ref[...]
    o_ref[0] = (jax.nn.silu(g_ref[0].astype(f32)) * o).astype(o_ref.dtype)


def retention_decode(qk, vg, gn_g, states, base):
    b = qk.shape[0]
    h = RET_HEADS
    _, qdec, _ = _retention_decay_tables(1)
    qk3, vg3 = qk[:, None, :], vg[:, None, :]
    o, s = pl.pallas_call(
        _retention_decode_kernel,
        grid=(b, h),
        in_specs=[
            pl.BlockSpec((1, 1, RET_DK), lambda i, hh: (i, 0, hh)),
            pl.BlockSpec((1, 1, RET_DK), lambda i, hh: (i, 0, h + hh)),
            pl.BlockSpec((1, 1, RET_DV), lambda i, hh: (i, 0, hh)),
            pl.BlockSpec((1, 1, RET_DV), lambda i, hh: (i, 0, h + hh)),
            pl.BlockSpec((1, RET_DV), lambda i, hh: (0, hh)),
            pl.BlockSpec((1, 1, 1), lambda i, hh: (hh, 0, 0)),
            pl.BlockSpec((1, 1, RET_DK, RET_DV), lambda i, hh: (base + i, hh, 0, 0)),
        ],
        out_specs=[
            pl.BlockSpec((1, 1, RET_DV), lambda i, hh: (i, 0, hh)),
            pl.BlockSpec((1, 1, RET_DK, RET_DV), lambda i, hh: (i, hh, 0, 0)),
        ],
        out_shape=[jax.ShapeDtypeStruct((b, 1, h * RET_DV), bf16), jax.ShapeDtypeStruct((b, h, RET_DK, RET_DV), f32)],
        compiler_params=_params("parallel", "parallel"),
        name="retention_decode",
    )(qk3, qk3, vg3, vg3, gn_g, qdec, states)
    return o[:, 0, :], s


HIST_PAGES_PER_STEP = 16


def _compress_hist_kernel(pt_ref, *refs, n_pages):
    page_refs = refs[:n_pages]
    pe_ref, w1_ref, w2_ref, o_ref, a_buf, o_buf = refs[n_pages:]
    g = NSA_KV
    upp = page_refs[0].shape[1] // CMP_STRIDE
    rows = n_pages * upp * g
    pad = a_buf.shape[0] - rows

    def load(l):
        return jnp.concatenate([pr[0, pl.ds(l, upp, stride=CMP_STRIDE), :, :].reshape(upp * g, NSA_HD)
                                for pr in page_refs], axis=0)

    a, b = _compress_units(load, pe_ref[0], w1_ref)

    @pl.when(pl.program_id(2) == 0)
    def _():
        a_buf[0:pad, :] = jnp.zeros((pad, a_buf.shape[1]), f32)

    @pl.when(pl.program_id(2) > 0)
    def _():
        a_buf[0:pad, :] = a_buf[rows:rows + pad, :]

    a_buf[pad:pad + rows, :] = a
    a_prev = a_buf[pad - g:pad - g + rows, :]
    o_buf[...] = _dot(jax.nn.gelu(a_prev + b).astype(bf16), w2_ref[0])
    for gg in range(g):
        o_ref[0, 0, gg] = o_buf[pl.ds(gg, rows // g, stride=g), :].astype(o_ref.dtype)


def compress_history(pool, page_table, pe, w1, w2):
    b, ppb = page_table.shape
    page = pool.shape[1]
    g = NSA_KV
    n_pages = min(HIST_PAGES_PER_STEP, ppb)
    assert ppb % n_pages == 0 and page % CMP_STRIDE == 0
    upp = page // CMP_STRIDE
    seg = n_pages * upp
    nu = ppb * upp
    sublanes = 8
    page_spec = lambda k: pl.BlockSpec(
        (1, page, None, g, NSA_HD), lambda i, s, ch, pt: (pt[i, ch * n_pages + k], 0, s, 0, 0))
    grid_spec = pltpu.PrefetchScalarGridSpec(
        num_scalar_prefetch=1,
        grid=(b, 2, ppb // n_pages),
        in_specs=[page_spec(k) for k in range(n_pages)] + [
            pl.BlockSpec((1, CMP_BLOCK, NSA_HD), lambda i, s, ch, pt: (s, 0, 0)),
            pl.BlockSpec((1, CMP_BLOCK * NSA_HD, CMP_HIDDEN), lambda i, s, ch, pt: (s, 0, 0)),
            pl.BlockSpec((1, CMP_HIDDEN, NSA_HD), lambda i, s, ch, pt: (s, 0, 0)),
        ],
        out_specs=pl.BlockSpec((1, 1, g, seg, NSA_HD), lambda i, s, ch, pt: (i, s, 0, ch, 0)),
        scratch_shapes=[pltpu.VMEM((seg * g + sublanes, CMP_HIDDEN), f32), pltpu.VMEM((seg * g, NSA_HD), f32)],
    )
    return pl.pallas_call(
        functools.partial(_compress_hist_kernel, n_pages=n_pages),
        grid_spec=grid_spec,
        out_shape=jax.ShapeDtypeStruct((b, 2, g, nu, NSA_HD), bf16),
        compiler_params=_params("parallel", "parallel", "arbitrary"),
        name="compress_history",
    )(page_table, *([pool] * n_pages), pe, w1, w2)


def _nsa_select_decode_kernel(q_ref, kc_ref, vc_ref, msel_ref, oc_ref, top_ref, *, t, ns, n_top):
    r_heads, hd = NSA_HPG, NSA_HD
    q = q_ref[0]
    qb = jnp.concatenate([q[:, r * hd:(r + 1) * hd] for r in range(r_heads)], axis=0)
    kc = kc_ref[0, 0, 0]
    nu = kc.shape[0]
    s = _dot_nt(qb, kc) * NSA_SCALE
    u = lax.broadcasted_iota(jnp.int32, (1, nu), 1)
    p = _masked_softmax(s, (u >= 1) & (u * CMP_STRIDE + (CMP_STRIDE - 1) <= t))
    oc_ref[0, 0] = _dot(p.astype(bf16), vc_ref[0, 0, 0])
    imp = jnp.sum(p, axis=0, keepdims=True)
    sel = jnp.dot(imp, msel_ref[...], precision=lax.Precision.HIGHEST, preferred_element_type=f32)
    jidx = lax.broadcasted_iota(jnp.int32, sel.shape, 1)
    cur = t // SEL_BLOCK
    forced = (jidx == 0) | (jidx == cur) | (jidx == cur - 1)
    sel = jnp.where(jidx * SEL_BLOCK <= t, jnp.where(forced, FORCE_SCORE, sel), NEG)
    sel = jnp.where(jidx < ns, sel, -jnp.inf)
    width = sel.shape[-1]
    lane = lax.broadcasted_iota(jnp.int32, (1, top_ref.shape[-1]), 1)
    top = jnp.zeros(lane.shape, jnp.int32)
    for it in range(n_top):
        mx = jnp.max(sel, axis=-1, keepdims=True)
        idx = jnp.min(jnp.where(sel == mx, jidx, width), axis=-1, keepdims=True)
        top = jnp.where(lane == it, idx, top)
        sel = jnp.where(jidx == idx, -jnp.inf, sel)
    top_ref[0, 0] = top


def nsa_select_decode(q, cmp, t):
    b = q.shape[0]
    g = NSA_KV
    nu = cmp.shape[3]
    ns = (t + 1 + SEL_BLOCK - 1) // SEL_BLOCK
    ns_pad = -(-ns // LANES) * LANES
    per_sel = SEL_BLOCK // CMP_STRIDE
    u = jnp.arange(nu)[:, None]
    j = jnp.arange(ns_pad)[None, :]
    rb = CMP_BLOCK // CMP_STRIDE
    msel = sum(((u - 1 + r) // per_sel == j).astype(f32) for r in range(rb)) / rb
    msel = jnp.where(u >= 1, msel, 0.0)
    gw = NSA_HPG * NSA_HD
    n_top = min(N_SEL, ns)
    cmp_spec = lambda s: pl.BlockSpec((1, 1, 1, nu, NSA_HD), lambda i, gg: (i, s, gg, 0, 0))
    return pl.pallas_call(
        functools.partial(_nsa_select_decode_kernel, t=t, ns=ns, n_top=n_top),
        grid=(b, g),
        in_specs=[
            pl.BlockSpec((1, 1, gw), lambda i, gg: (i, 0, gg)),
            cmp_spec(0), cmp_spec(1),
            pl.BlockSpec((nu, ns_pad), lambda i, gg: (0, 0)),
        ],
        out_specs=[
            pl.BlockSpec((1, 1, NSA_HPG, NSA_HD), lambda i, gg: (i, gg, 0, 0)),
            pl.BlockSpec((1, 1, 1, LANES), lambda i, gg: (i, gg, 0, 0)),
        ],
        out_shape=[jax.ShapeDtypeStruct((b, g, NSA_HPG, NSA_HD), f32), jax.ShapeDtypeStruct((b, g, 1, LANES), jnp.int32)],
        compiler_params=_params("parallel", "parallel"),
        name="nsa_select_decode",
    )(q[:, None, :], cmp, cmp, msel)


def _nsa_attn_decode_kernel(pt_ref, top_ref, *refs, n_top, n_hist_blocks):
    k_refs, v_refs = refs[:n_top], refs[n_top:2 * n_top]
    (q_ref, oc_ref, knew_ref, vnew_ref, kw_ref, vw_ref, kwnew_ref, vwnew_ref, gl_ref, o_ref) = refs[2 * n_top:]
    i, gg = pl.program_id(0), pl.program_id(1)
    r_heads, hd = NSA_HPG, NSA_HD
    q = q_ref[0]
    qb = jnp.concatenate([q[:, r * hd:(r + 1) * hd] for r in range(r_heads)], axis=0)

    def attend(keys, vals, mask, k_new, v_new):
        s = jnp.where(mask, _dot_nt(qb, keys) * NSA_SCALE, NEG)
        s_new = jnp.sum(qb.astype(f32) * k_new.astype(f32), axis=-1, keepdims=True) * NSA_SCALE
        m = jnp.maximum(jnp.max(s, axis=-1, keepdims=True), s_new)
        e = jnp.where(mask, jnp.exp(s - m), 0.0)
        e_new = jnp.exp(s_new - m)
        num = _dot(e.astype(bf16), vals) + e_new.astype(bf16).astype(f32) * v_new.astype(f32)
        return num / (jnp.sum(e, axis=-1, keepdims=True) + e_new)

    g = NSA_KV

    def rows_of(ref):
        v = ref[0]
        return v.reshape(v.shape[0] * g, hd).astype(bf16)

    def own_group(n_rows):
        return lax.broadcasted_iota(jnp.int32, (1, n_rows * g), 1) % g == gg

    sb = k_refs[0].shape[1]
    keys = jnp.concatenate([rows_of(r) for r in k_refs], axis=0)
    vals = jnp.concatenate([rows_of(r) for r in v_refs], axis=0)
    blk_of_lane = lax.broadcasted_iota(jnp.int32, (1, n_top * sb * g), 1) // (sb * g)
    sel_of_lane = jnp.zeros((1, n_top * sb * g), jnp.int32)
    for n in range(n_top):
        sel_of_lane = jnp.where(blk_of_lane == n, top_ref[i, gg, n], sel_of_lane)
    o_s = attend(keys, vals, (sel_of_lane < n_hist_blocks) & own_group(n_top * sb), knew_ref[0], vnew_ref[0])

    wlen = kw_ref.shape[1]
    wmask = (lax.broadcasted_iota(jnp.int32, (1, wlen * g), 1) >= g) & own_group(wlen)
    o_w = attend(rows_of(kw_ref), rows_of(vw_ref), wmask, kwnew_ref[0], vwnew_ref[0])

    gates = jax.nn.sigmoid(gl_ref[0])
    o_c = oc_ref[0, 0]
    outs = []
    for r in range(r_heads):
        outs.append(gates[:, 3 * r:3 * r + 1] * o_c[r:r + 1] + gates[:, 3 * r + 1:3 * r + 2] * o_s[r:r + 1]
                    + gates[:, 3 * r + 2:3 * r + 3] * o_w[r:r + 1])
    o_ref[0] = jnp.concatenate(outs, axis=1).astype(o_ref.dtype)


def nsa_attn_decode(q, o_c, top, kvb, gl, pool, page_table, win, win_base, t):
    b = q.shape[0]
    g = NSA_KV
    n_top = top.shape[-1]
    page = pool.shape[1]
    assert t % SEL_BLOCK == 0 and page % SEL_BLOCK == 0 and win.shape[1] == WINDOW
    n_hist_blocks = t // SEL_BLOCK
    bpp = page // SEL_BLOCK
    gw = NSA_HPG * NSA_HD

    def blk_spec(n, slot):
        def imap(i, gg, pt, tp):
            j = jnp.minimum(tp[i, gg, n], n_hist_blocks - 1)
            return (pt[i, j // bpp], j % bpp, slot, 0, 0)
        return pl.BlockSpec((1, SEL_BLOCK, None, g, NSA_HD), imap)

    new_spec = lambda slot: pl.BlockSpec((1, 1, NSA_HD), lambda i, gg, pt, tp: (i, 0, slot * g + gg))
    win_spec = lambda slot: pl.BlockSpec((1, WINDOW, None, g, NSA_HD),
                                         lambda i, gg, pt, tp: (win_base + i, 0, slot, 0, 0))
    kvb3 = kvb[:, None, :]
    grid_spec = pltpu.PrefetchScalarGridSpec(
        num_scalar_prefetch=2,
        grid=(b, g),
        in_specs=[blk_spec(n, 2) for n in range(n_top)] + [blk_spec(n, 3) for n in range(n_top)] + [
            pl.BlockSpec((1, 1, gw), lambda i, gg, pt, tp: (i, 0, gg)),
            pl.BlockSpec((1, 1, NSA_HPG, NSA_HD), lambda i, gg, pt, tp: (i, gg, 0, 0)),
            new_spec(2), new_spec(3), win_spec(0), win_spec(1), new_spec(4), new_spec(5),
            pl.BlockSpec((1, 1, LANES), lambda i, gg, pt, tp: (i, 0, gg)),
        ],
        out_specs=pl.BlockSpec((1, 1, gw), lambda i, gg, pt, tp: (i, 0, gg)),
    )
    o = pl.pallas_call(
        functools.partial(_nsa_attn_decode_kernel, n_top=n_top, n_hist_blocks=n_hist_blocks),
        grid_spec=grid_spec,
        out_shape=jax.ShapeDtypeStruct((b, 1, NSA_HEADS * NSA_HD), bf16),
        compiler_params=_params("parallel", "parallel"),
        name="nsa_attn_decode",
    )(page_table, top, *([pool] * (2 * n_top)), q[:, None, :], o_c, kvb3, kvb3, win, win, kvb3, kvb3,
      gl[:, None, :])
    return o[:, 0, :]


def _retention_mixer(x, pos, batch, w, states, base):
    cos, sin = _rope_tables(pos, RET_DK)
    tn = 2 * RET_DK
    flags = jnp.ones((w["qk"].shape[1] // tn,), jnp.int32)
    qk, = mm_rope(x, w["qk"], cos, sin, flags, [bf16], hd=RET_DK, tm=512, tn=tn)
    vg = mm(x, w["vg"], bf16, tm=512, tn=512)
    if states is None:
        return retention_prompt(qk, vg, w["gn"], batch)
    return retention_decode(qk, vg, w["gn"], states, base)


def _nsa_projections(x, pos, w):
    cos, sin = _rope_tables(pos, NSA_HD)
    tn = NSA_KV * NSA_HD
    q, = mm_rope(x, w["q"], cos, sin, jnp.ones((NSA_HEADS * NSA_HD // tn,), jnp.int32), [bf16],
                 hd=NSA_HD, tm=512, tn=tn)
    kvf, kvb = mm_rope(x, w["kv"], cos, sin, jnp.array([1, 0] * 3, jnp.int32), [f32, bf16],
                       hd=NSA_HD, tm=512, tn=tn)
    gl = mm(x, w["gl"], f32, tm=512, tn=tn)
    return q, kvf, kvb, gl


def _layer_tail(x, h, p, w, tm_moe):
    x1 = mm_ln(h, w["out"], x, w["ln_g"][0:1], w["ln_b"][0:1], tm=256, tk=2048)
    ys, d1, d2 = moe_layer(x1, w["router_w"], w["router_b"], w["moe_gate"], w["moe_up"], w["moe_down"], tm=tm_moe)
    return ln_ple(x1, ys, d1, d2, w["ln_g"][1:2], w["ln_b"][1:2], p, w["ple_gate"], w["ple_proj"], tm=256)


def kernel(x_prompt, x_sample, state_ret, cache_nsa_kv, state_nsa_win, page_table, p_prompt, p_sample, ret_w_in, ret_w_out, ret_gn_g, nsa_w_in, nsa_w_out, nsa_cmp_pos, nsa_cmp_w1, nsa_cmp_w2, ln_g, ln_b, router_w, router_b, moe_w_gate, moe_w_up, moe_w_down, ple_w_gate, ple_w_proj):
    bp, tp, d = x_prompt.shape
    bs, ts, _ = x_sample.shape
    assert ts == 1
    n_pool, page = cache_nsa_kv.shape[1], cache_nsa_kv.shape[2]
    past = page_table.shape[1] * page
    g, hd = NSA_KV, NSA_HD
    kv_cols = N_KV_SLOTS * g * hd

    xp = x_prompt.reshape(bp * tp, d)
    xs = x_sample.reshape(bs * ts, d)
    pos_p = jnp.arange(tp, dtype=jnp.int32)
    pos_s = jnp.full((bs,), past, jnp.int32)
    states = state_ret.reshape((-1,) + state_ret.shape[2:])
    pool = cache_nsa_kv.reshape((-1,) + cache_nsa_kv.shape[2:])
    wins = state_nsa_win.reshape((-1,) + state_nsa_win.shape[2:])

    ret_p, ret_s, kv_p, kv_s, win_p, win_s = [], [], [], [], [], []
    for i in range(DEPTH):
        j = i // 2
        w = {
            "ln_g": ln_g[i], "ln_b": ln_b[i],
            "router_w": router_w, "router_b": router_b[None, :],
            "moe_gate": moe_w_gate[i].astype(bf16), "moe_up": moe_w_up[i].astype(bf16),
            "moe_down": moe_w_down[i].astype(bf16),
            "ple_gate": ple_w_gate[i].astype(bf16), "ple_proj": ple_w_proj[i].astype(bf16),
        }
        if i % 2 == 0:
            hk = RET_HEADS * RET_DK
            w.update(qk=ret_w_in[j][:, :2 * hk].astype(bf16), vg=ret_w_in[j][:, 2 * hk:].astype(bf16),
                     out=ret_w_out[j].astype(bf16), gn=ret_gn_g[j][None, :])
            hp, sp = _retention_mixer(xp, pos_p, bp, w, None, 0)
            hs, ss = _retention_mixer(xs, pos_s, bs, w, states, j * bs)
            ret_p.append(sp)
            ret_s.append(ss)
        else:
            nq = NSA_HEADS * hd
            w_in = nsa_w_in[j]
            gl = w_in[:, nq + 6 * g * hd:].reshape(d, g, NSA_HPG * 3)
            gl = jnp.pad(gl, ((0, 0), (0, 0), (0, LANES - NSA_HPG * 3))).reshape(d, g * LANES)
            w.update(q=w_in[:, :nq].astype(bf16), kv=w_in[:, nq:nq + 6 * g * hd].astype(bf16), gl=gl.astype(bf16),
                     out=nsa_w_out[j].astype(bf16))
            pe = nsa_cmp_pos[j]
            w1 = nsa_cmp_w1[j].reshape(2, CMP_BLOCK * hd, CMP_HIDDEN).astype(bf16)
            w2 = nsa_cmp_w2[j].astype(bf16)
            q, kvf, kvb, glp = _nsa_projections(xp, pos_p, w)
            cmp = compress_prompt(kvf, pe, w1, w2, bp)
            hp = nsa_attn_prompt(q, cmp, kvb, glp, bp)
            kv_p.append(kvf[:, :kv_cols].reshape(bp, tp, N_KV_SLOTS, g, hd))
            keep = min(WINDOW, tp)
            win_p.append(kvf.reshape(bp, tp, -1)[:, tp - keep:, kv_cols:].reshape(bp, keep, 2, g, hd))
            q, kvf, kvb, gls = _nsa_projections(xs, pos_s, w)
            pt = page_table + j * n_pool
            cmp = compress_history(pool, pt, pe, w1, w2)
            o_c, top = nsa_select_decode(q, cmp, past)
            top = top[:, :, 0, :min(N_SEL, past // SEL_BLOCK + 1)]
            hs = nsa_attn_decode(q, o_c, top, kvb, gls, pool, pt, wins, j * bs, past)
            kv_s.append(kvf[:, :kv_cols].reshape(bs, ts, N_KV_SLOTS, g, hd))
            new_win = kvf[:, kv_cols:].reshape(bs, ts, 2, g, hd)
            win_s.append(jnp.concatenate([state_nsa_win[j][:, ts:], new_win], axis=1))
        xp = _layer_tail(xp, hp, p_prompt[i].reshape(bp * tp, -1), w, MOE_TILE)
        xs = _layer_tail(xs, hs, p_sample[i].reshape(bs * ts, -1), w, 16)

    return (xp.reshape(bp, tp, d), xs.reshape(bs, ts, d), jnp.stack(ret_p), jnp.stack(ret_s),
            jnp.stack(kv_p), jnp.stack(kv_s), jnp.stack(win_p), jnp.stack(win_s))
```

```python
import functools
import math

import jax
import jax.numpy as jnp
from jax import lax
from jax.experimental import pallas as pl
from jax.experimental.pallas import tpu as pltpu

f32 = jnp.float32
bf16 = jnp.bfloat16

D_MODEL = 2048
DEPTH = 4
RET_HEADS = 8
RET_DK = D_MODEL // RET_HEADS
RET_DV = 2 * RET_DK
RET_CHUNK = 128
NSA_HEADS = 16
NSA_HD = D_MODEL // NSA_HEADS
NSA_KV = 4
NSA_HPG = NSA_HEADS // NSA_KV
CMP_BLOCK = 32
CMP_STRIDE = 16
CMP_HIDDEN = 4 * NSA_HD
SEL_BLOCK = 64
N_SEL = 16
WINDOW = 512
Q_BLOCK = 128
N_KV_SLOTS = 4
N_EXPERTS = 16
N_GROUPS = 4
EXPERTS_PER_GROUP = N_EXPERTS // N_GROUPS
D_EXPERT = 1408
PLE_DIM = 256
ROPE_THETA = 10000.0
LN_EPS = 1e-5
DN_ALPHA = (2 * DEPTH) ** 0.25
NEG = -1e30
FORCE_SCORE = float(NSA_HPG + 1)
NSA_SCALE = NSA_HD ** -0.5

V7X_VMEM_BYTES = 64 * 1024 * 1024
VMEM_LIMIT = V7X_VMEM_BYTES - 8 * 1024 * 1024
LANES = 128
MOE_TILE = 256
PROJ_TILE = 1024
RET_HEADS_PER_STEP = 2


def _params(*sem):
    return pltpu.CompilerParams(dimension_semantics=sem, vmem_limit_bytes=VMEM_LIMIT)


def _layer_norm(v, g, b):
    mu = jnp.mean(v, axis=-1, keepdims=True)
    d = v - mu
    var = jnp.mean(d * d, axis=-1, keepdims=True)
    return d * lax.rsqrt(var + LN_EPS) * g + b


def _dot(a, b):
    return jnp.dot(a, b, preferred_element_type=f32)


def _dot_nt(a, b):
    return lax.dot_general(a, b, (((1,), (1,)), ((), ())), preferred_element_type=f32)


def _dot_tn(a, b):
    return lax.dot_general(a, b, (((0,), (0,)), ((), ())), preferred_element_type=f32)


def _rotate_half(v, hd):
    pieces = []
    for c in range(0, v.shape[1], hd):
        if hd == 2 * LANES:
            pieces += [v[:, c + LANES:c + hd], v[:, c:c + LANES]]
        else:
            pieces.append(pltpu.roll(v[:, c:c + hd], hd // 2, 1))
    return jnp.concatenate(pieces, axis=1)


def _mm_rope_kernel(flags_ref, x_ref, w_ref, cos_ref, sin_ref, *o_refs, hd):
    acc = _dot(x_ref[...].astype(bf16), w_ref[...])
    roped = flags_ref[pl.program_id(1)] == 1

    @pl.when(roped)
    def _():
        reps = acc.shape[1] // hd
        cos = jnp.concatenate([cos_ref[...]] * reps, axis=1)
        sin = jnp.concatenate([sin_ref[...]] * reps, axis=1)
        v = acc * cos + _rotate_half(acc, hd) * sin
        for o in o_refs:
            o[...] = v.astype(o.dtype)

    @pl.when(jnp.logical_not(roped))
    def _():
        for o in o_refs:
            o[...] = acc.astype(o.dtype)


def mm_rope(x, w, cos, sin, flags, out_dtypes, *, hd, tm, tn):
    n, k = x.shape
    m = w.shape[1]
    t = cos.shape[0]
    tm = min(tm, n)
    assert n % tm == 0 and m % tn == 0 and t % tm == 0 and tn % hd == 0
    tper = t // tm
    grid_spec = pltpu.PrefetchScalarGridSpec(
        num_scalar_prefetch=1,
        grid=(n // tm, m // tn),
        in_specs=[
            pl.BlockSpec((tm, k), lambda i, j, f: (i, 0)),
            pl.BlockSpec((k, tn), lambda i, j, f: (0, j)),
            pl.BlockSpec((tm, hd), lambda i, j, f: (i % tper, 0)),
            pl.BlockSpec((tm, hd), lambda i, j, f: (i % tper, 0)),
        ],
        out_specs=[pl.BlockSpec((tm, tn), lambda i, j, f: (i, j)) for _ in out_dtypes],
    )
    return pl.pallas_call(
        functools.partial(_mm_rope_kernel, hd=hd),
        grid_spec=grid_spec,
        out_shape=[jax.ShapeDtypeStruct((n, m), d) for d in out_dtypes],
        compiler_params=_params("parallel", "arbitrary"),
        name="mm_rope",
    )(flags, x, w, cos, sin)


def _mm_kernel(x_ref, w_ref, o_ref):
    o_ref[...] = _dot(x_ref[...].astype(bf16), w_ref[...]).astype(o_ref.dtype)


def mm(x, w, out_dtype, *, tm, tn):
    n, k = x.shape
    m = w.shape[1]
    tm = min(tm, n)
    tn = min(tn, m)
    assert n % tm == 0 and m % tn == 0
    return pl.pallas_call(
        _mm_kernel,
        grid=(n // tm, m // tn),
        in_specs=[pl.BlockSpec((tm, k), lambda i, j: (i, 0)), pl.BlockSpec((k, tn), lambda i, j: (0, j))],
        out_specs=pl.BlockSpec((tm, tn), lambda i, j: (i, j)),
        out_shape=jax.ShapeDtypeStruct((n, m), out_dtype),
        compiler_params=_params("parallel", "arbitrary"),
        name="mm",
    )(x, w)


def _mm_ln_kernel(a_ref, w_ref, x_ref, g_ref, b_ref, o_ref, acc_ref):
    kk = pl.program_id(1)

    @pl.when(kk == 0)
    def _():
        acc_ref[...] = jnp.zeros_like(acc_ref)

    acc_ref[...] += _dot(a_ref[...].astype(bf16), w_ref[...])

    @pl.when(kk == pl.num_programs(1) - 1)
    def _():
        o_ref[...] = _layer_norm(DN_ALPHA * x_ref[...] + acc_ref[...], g_ref[...], b_ref[...])


def _mm_ln_single_kernel(a_ref, w_ref, x_ref, g_ref, b_ref, o_ref):
    acc = _dot(a_ref[...].astype(bf16), w_ref[...])
    o_ref[...] = _layer_norm(DN_ALPHA * x_ref[...] + acc, g_ref[...], b_ref[...])


def mm_ln(a, w, x, g, b, *, tm, tk):
    n, k = a.shape
    d = w.shape[1]
    tm = min(tm, n)
    tk = min(tk, k)
    assert n % tm == 0 and k % tk == 0
    if tk == k:
        return pl.pallas_call(
            _mm_ln_single_kernel,
            grid=(n // tm,),
            in_specs=[
                pl.BlockSpec((tm, k), lambda i: (i, 0)),
                pl.BlockSpec((k, d), lambda i: (0, 0)),
                pl.BlockSpec((tm, d), lambda i: (i, 0)),
                pl.BlockSpec((1, d), lambda i: (0, 0)),
                pl.BlockSpec((1, d), lambda i: (0, 0)),
            ],
            out_specs=pl.BlockSpec((tm, d), lambda i: (i, 0)),
            out_shape=jax.ShapeDtypeStruct((n, d), f32),
            compiler_params=_params("parallel"),
            name="mm_ln",
        )(a, w, x, g, b)
    return pl.pallas_call(
        _mm_ln_kernel,
        grid=(n // tm, k // tk),
        in_specs=[
            pl.BlockSpec((tm, tk), lambda i, kk: (i, kk)),
            pl.BlockSpec((tk, d), lambda i, kk: (kk, 0)),
            pl.BlockSpec((tm, d), lambda i, kk: (i, 0)),
            pl.BlockSpec((1, d), lambda i, kk: (0, 0)),
            pl.BlockSpec((1, d), lambda i, kk: (0, 0)),
        ],
        out_specs=pl.BlockSpec((tm, d), lambda i, kk: (i, 0)),
        out_shape=jax.ShapeDtypeStruct((n, d), f32),
        scratch_shapes=[pltpu.VMEM((tm, d), f32)],
        compiler_params=_params("parallel", "arbitrary"),
        name="mm_ln",
    )(a, w, x, g, b)


def _retention_kernel(q_ref, k_ref, v_ref, g_ref, gn_ref, intra_ref, qdec_ref, kdec_ref, o_ref, s_ref, state):
    c = pl.program_id(2)

    @pl.when(c == 0)
    def _():
        state[...] = jnp.zeros_like(state)

    for hh in range(state.shape[0]):
        kcols = slice(hh * RET_DK, (hh + 1) * RET_DK)
        vcols = slice(hh * RET_DV, (hh + 1) * RET_DV)
        q = q_ref[:, kcols]
        k = k_ref[:, kcols] * (RET_DK ** -0.5)
        v = v_ref[:, vcols]
        qdec = qdec_ref[hh]
        kdec = kdec_ref[hh]
        cdec = qdec[-1:, :]
        s_prev = state[hh]
        att = _dot_nt(q, k) * intra_ref[hh]
        o = _dot(att.astype(bf16), v) + _dot(q, s_prev.astype(bf16)) * qdec
        kd = (k.astype(f32) * kdec).astype(bf16)
        s_new = s_prev * cdec + _dot_tn(kd, v)
        state[hh] = s_new

        mu = jnp.mean(o, axis=-1, keepdims=True)
        d = o - mu
        var = jnp.mean(d * d, axis=-1, keepdims=True)
        o = d * lax.rsqrt(var + LN_EPS) * gn_ref[:, vcols]
        o_ref[:, vcols] = (jax.nn.silu(g_ref[:, vcols].astype(f32)) * o).astype(o_ref.dtype)

    @pl.when(c == pl.num_programs(2) - 1)
    def _():
        s_ref[0] = state[...]


def _retention_decay_tables(chunk):
    h = jnp.arange(RET_HEADS, dtype=f32)
    log_gamma = jnp.log1p(-jnp.exp2(-5.0 - h))
    idx = jnp.arange(chunk, dtype=f32)
    rel = idx[:, None] - idx[None, :]
    intra = jnp.where(rel >= 0, jnp.exp(jnp.maximum(rel, 0.0)[None] * log_gamma[:, None, None]), 0.0)
    qdec = jnp.exp((idx[None, :] + 1.0) * log_gamma[:, None])[..., None]
    kdec = jnp.exp((chunk - 1.0 - idx[None, :]) * log_gamma[:, None])[..., None]
    return intra, qdec, kdec


def retention_prompt(qk, vg, gn_g, batch):
    n = qk.shape[0]
    t = n // batch
    chunk = math.gcd(t, RET_CHUNK)
    nch = t // chunk
    h = RET_HEADS
    hps = RET_HEADS_PER_STEP
    hb = h // hps
    intra, qdec, kdec = _retention_decay_tables(chunk)
    row = lambda b, hh, c: b * nch + c
    return pl.pallas_call(
        _retention_kernel,
        grid=(batch, hb, nch),
        in_specs=[
            pl.BlockSpec((chunk, hps * RET_DK), lambda b, hh, c: (row(b, hh, c), hh)),
            pl.BlockSpec((chunk, hps * RET_DK), lambda b, hh, c: (row(b, hh, c), hb + hh)),
            pl.BlockSpec((chunk, hps * RET_DV), lambda b, hh, c: (row(b, hh, c), hh)),
            pl.BlockSpec((chunk, hps * RET_DV), lambda b, hh, c: (row(b, hh, c), hb + hh)),
            pl.BlockSpec((1, hps * RET_DV), lambda b, hh, c: (0, hh)),
            pl.BlockSpec((hps, chunk, chunk), lambda b, hh, c: (hh, 0, 0)),
            pl.BlockSpec((hps, chunk, 1), lambda b, hh, c: (hh, 0, 0)),
            pl.BlockSpec((hps, chunk, 1), lambda b, hh, c: (hh, 0, 0)),
        ],
        out_specs=[
            pl.BlockSpec((chunk, hps * RET_DV), lambda b, hh, c: (row(b, hh, c), hh)),
            pl.BlockSpec((1, hps, RET_DK, RET_DV), lambda b, hh, c: (b, hh, 0, 0)),
        ],
        out_shape=[
            jax.ShapeDtypeStruct((n, h * RET_DV), bf16),
            jax.ShapeDtypeStruct((batch, h, RET_DK, RET_DV), f32),
        ],
        scratch_shapes=[pltpu.VMEM((hps, RET_DK, RET_DV), f32)],
        compiler_params=_params("parallel", "parallel", "arbitrary"),
        name="retention_prompt",
    )(qk, qk, vg, vg, gn_g, intra, qdec, kdec)


def _rope_tables(pos, hd):
    half = hd // 2
    inv = ROPE_THETA ** (-jnp.arange(half, dtype=f32) / half)
    ang = pos.astype(f32)[:, None] * inv[None, :]
    cos, sin = jnp.cos(ang), jnp.sin(ang)
    return jnp.concatenate([cos, cos], axis=1), jnp.concatenate([-sin, sin], axis=1)


def _compress_units(load_unit_row, pe, w1_ref):
    xa, xb = [], []
    for l in range(CMP_STRIDE):
        xl = load_unit_row(l)
        xa.append((xl + pe[l:l + 1]).astype(bf16))
        xb.append((xl + pe[CMP_STRIDE + l:CMP_STRIDE + l + 1]).astype(bf16))
    half = CMP_STRIDE * NSA_HD
    a = _dot(jnp.concatenate(xa, axis=1), w1_ref[0, :half])
    b = _dot(jnp.concatenate(xb, axis=1), w1_ref[0, half:])
    return a, b


def _compress_kernel(kv_ref, pe_ref, w1_ref, w2_ref, o_ref):
    nu = kv_ref.shape[0] // CMP_STRIDE
    a, b = _compress_units(lambda l: kv_ref[pl.ds(l, nu, stride=CMP_STRIDE), :], pe_ref[0], w1_ref)
    hid = a + pltpu.roll(b, nu - 1, 0)
    o_ref[0, 0, 0] = _dot(jax.nn.gelu(hid).astype(bf16), w2_ref[0]).astype(o_ref.dtype)


def compress_prompt(kvf, pe, w1, w2, batch):
    n = kvf.shape[0]
    t = n // batch
    nu = t // CMP_STRIDE
    g = NSA_KV
    return pl.pallas_call(
        _compress_kernel,
        grid=(batch, 2, g),
        in_specs=[
            pl.BlockSpec((t, NSA_HD), lambda b, s, gg: (b, s * g + gg)),
            pl.BlockSpec((1, CMP_BLOCK, NSA_HD), lambda b, s, gg: (s, 0, 0)),
            pl.BlockSpec((1, CMP_BLOCK * NSA_HD, CMP_HIDDEN), lambda b, s, gg: (s, 0, 0)),
            pl.BlockSpec((1, CMP_HIDDEN, NSA_HD), lambda b, s, gg: (s, 0, 0)),
        ],
        out_specs=pl.BlockSpec((1, 1, 1, nu, NSA_HD), lambda b, s, gg: (b, s, gg, 0, 0)),
        out_shape=jax.ShapeDtypeStruct((batch, 2, g, nu, NSA_HD), bf16),
        compiler_params=_params("parallel", "parallel", "parallel"),
        name="compress_prompt",
    )(kvf, pe, w1, w2)


def _select_blocks(sel, n_top, axis):
    ns = sel.shape[axis]
    jidx = lax.broadcasted_iota(jnp.int32, sel.shape, axis)
    chosen = jnp.zeros(sel.shape, f32)
    for _ in range(n_top):
        mx = jnp.max(sel, axis=axis, keepdims=True)
        idx = jnp.min(jnp.where(sel == mx, jidx, ns), axis=axis, keepdims=True)
        hit = jidx == idx
        chosen = jnp.where(hit, 1.0, chosen)
        sel = jnp.where(hit, -jnp.inf, sel)
    return chosen


def _masked_softmax(s, mask):
    s = jnp.where(mask, s, NEG)
    e = jnp.exp(s - jnp.max(s, axis=-1, keepdims=True))
    return jnp.where(mask, e / jnp.sum(e, axis=-1, keepdims=True), 0.0)


def _nsa_attn_kernel(q_ref, kc_ref, vc_ref, ks_ref, vs_ref, kw_ref, vw_ref, gl_ref, msel_ref, exp_ref,
                     o_ref, mexp, vs_t, vw_t, vc_t, *, n_top):
    qi = pl.program_id(2)
    blk = q_ref.shape[0]
    r_heads, hd = NSA_HPG, NSA_HD
    t_len = ks_ref.shape[0]
    nc = kc_ref.shape[3]
    ck = 4 * blk
    zscale = NSA_SCALE * math.log2(math.e)

    def transposed(ref_rows):
        return ref_rows.astype(f32).T.astype(bf16)

    @pl.when(qi == 0)
    def _():
        def tr(c, carry):
            off = pl.multiple_of(c * blk, blk)
            vs_t[:, pl.ds(off, blk)] = transposed(vs_ref[pl.ds(off, blk), :])
            vw_t[:, pl.ds(off, blk)] = transposed(vw_ref[pl.ds(off, blk), :])
            return carry

        lax.fori_loop(0, t_len // blk, tr, 0)
        for c in range(nc // blk):
            vc_t[:, c * blk:(c + 1) * blk] = transposed(vc_ref[0, 0, 0, c * blk:(c + 1) * blk, :])

    def head(x, r):
        return x[:, r * blk:(r + 1) * blk]

    def per_head(fn):
        return jnp.concatenate([fn(r) for r in range(r_heads)], axis=1)

    q = q_ref[...].astype(f32)
    q_t = per_head(lambda r: q[:, r * hd:(r + 1) * hd].T).astype(bf16)
    t_q = qi * blk + lax.broadcasted_iota(jnp.int32, (1, blk), 1)

    z = _dot(kc_ref[0, 0, 0], q_t) * zscale
    c_end = lax.broadcasted_iota(jnp.int32, (nc, 1), 0) * CMP_STRIDE + (CMP_BLOCK - 1)
    mc = c_end <= t_q
    z = per_head(lambda r: jnp.where(mc, head(z, r), NEG))
    e = jnp.exp2(z - jnp.max(z, axis=0, keepdims=True))
    p = e * (1.0 / jnp.sum(e, axis=0, keepdims=True))
    p = per_head(lambda r: jnp.where(mc, head(p, r), 0.0))
    oc_t = _dot(vc_t[...], p.astype(bf16))

    imp = head(p, 0)
    for r in range(1, r_heads):
        imp = imp + head(p, r)
    sel = jnp.dot(msel_ref[...], imp, precision=lax.Precision.HIGHEST, preferred_element_type=f32)
    jidx = lax.broadcasted_iota(jnp.int32, sel.shape, 0)
    cur = lax.shift_right_logical(t_q, int(math.log2(SEL_BLOCK)))
    forced = (jidx == 0) | (jidx == cur) | (jidx == cur - 1)
    sel = jnp.where(jidx * SEL_BLOCK <= t_q, jnp.where(forced, FORCE_SCORE, sel), NEG)
    mexp[...] = _dot(exp_ref[...], _select_blocks(sel, n_top, 0).astype(bf16))

    def flash(k_ref, vt_ref, n_chunks, chunk_of, mask_fn):
        def body(i, carry):
            m, l, acc = carry
            off = pl.multiple_of(chunk_of(i) * ck, ck)
            z = _dot(k_ref[pl.ds(off, ck), :], q_t) * zscale
            msk = mask_fn(off, off + lax.broadcasted_iota(jnp.int32, (ck, 1), 0))
            z = per_head(lambda r: jnp.where(msk, head(z, r), NEG))
            m_new = jnp.maximum(m, jnp.max(z, axis=0, keepdims=True))
            alpha = jnp.exp2(m - m_new)
            e = jnp.exp2(z - m_new)
            l = l * alpha + jnp.sum(e, axis=0, keepdims=True)
            return m_new, l, acc * alpha + _dot(vt_ref[:, pl.ds(off, ck)], e.astype(bf16))

        width = r_heads * blk
        init = (jnp.full((1, width), NEG, f32), jnp.zeros((1, width), f32), jnp.zeros((hd, width), f32))
        _, l, acc = lax.fori_loop(0, n_chunks, body, init)
        return acc * (1.0 / l)

    os_t = flash(ks_ref, vs_t, qi // (ck // blk) + 1, lambda i: i,
                 lambda off, kpos: (mexp[pl.ds(off, ck), :] > 0.5) & (kpos <= t_q))

    span = WINDOW + blk
    w_off = pl.multiple_of(jnp.clip(qi * blk - WINDOW, 0, t_len - span), blk)
    kpos = w_off + lax.broadcasted_iota(jnp.int32, (span, 1), 0)
    in_win = (kpos <= t_q) & (kpos > t_q - WINDOW)
    z = _dot(kw_ref[pl.ds(w_off, span), :], q_t) * zscale
    z = per_head(lambda r: jnp.where(in_win, head(z, r), NEG))
    e = jnp.exp2(z - jnp.max(z, axis=0, keepdims=True))
    ow_t = _dot(vw_t[:, pl.ds(w_off, span)], e.astype(bf16)) * (1.0 / jnp.sum(e, axis=0, keepdims=True))

    g_t = jax.nn.sigmoid(gl_ref[...]).T
    outs = []
    for r in range(r_heads):
        o_r = (g_t[3 * r:3 * r + 1] * head(oc_t, r) + g_t[3 * r + 1:3 * r + 2] * head(os_t, r)
               + g_t[3 * r + 2:3 * r + 3] * head(ow_t, r))
        outs.append(o_r.T)
    o_ref[...] = jnp.concatenate(outs, axis=1).astype(o_ref.dtype)


def _selection_constants(nu, ns, nkeys):
    c = jnp.arange(nu)[None, :]
    j = jnp.arange(ns)[:, None]
    per_sel = SEL_BLOCK // CMP_STRIDE
    rb = CMP_BLOCK // CMP_STRIDE
    msel = sum(((c + r) // per_sel == j).astype(f32) for r in range(rb)) / rb
    msel = jnp.where(c < nu - rb + 1, msel, 0.0)
    expand = (jnp.arange(nkeys)[:, None] // SEL_BLOCK == jnp.arange(ns)[None, :]).astype(bf16)
    return msel, expand


def nsa_attn_prompt(q, cmp, kvb, gl, batch):
    n = q.shape[0]
    t = n // batch
    blk = math.gcd(t, Q_BLOCK)
    nq = t // blk
    g = NSA_KV
    nu = t // CMP_STRIDE
    ns = t // SEL_BLOCK
    msel, expand = _selection_constants(nu, ns, t)
    gw = NSA_HPG * NSA_HD
    kv_spec = lambda slot: pl.BlockSpec((t, NSA_HD), lambda b, gg, qi: (b, slot * g + gg))
    cmp_spec = lambda s: pl.BlockSpec((1, 1, 1, nu, NSA_HD), lambda b, gg, qi: (b, s, gg, 0, 0))
    return pl.pallas_call(
        functools.partial(_nsa_attn_kernel, n_top=min(N_SEL, ns)),
        grid=(batch, g, nq),
        in_specs=[
            pl.BlockSpec((blk, gw), lambda b, gg, qi: (b * nq + qi, gg)),
            cmp_spec(0), cmp_spec(1),
            kv_spec(2), kv_spec(3), kv_spec(4), kv_spec(5),
            pl.BlockSpec((blk, LANES), lambda b, gg, qi: (b * nq + qi, gg)),
            pl.BlockSpec((ns, nu), lambda b, gg, qi: (0, 0)),
            pl.BlockSpec((t, ns), lambda b, gg, qi: (0, 0)),
        ],
        out_specs=pl.BlockSpec((blk, gw), lambda b, gg, qi: (b * nq + qi, gg)),
        out_shape=jax.ShapeDtypeStruct((n, NSA_HEADS * NSA_HD), bf16),
        scratch_shapes=[pltpu.VMEM((t, blk), f32), pltpu.VMEM((NSA_HD, t), bf16), pltpu.VMEM((NSA_HD, t), bf16),
                        pltpu.VMEM((NSA_HD, nu), bf16)],
        compiler_params=_params("parallel", "parallel", "arbitrary"),
        name="nsa_attn_prompt",
    )(q, cmp, cmp, kvb, kvb, kvb, kvb, gl, msel, expand)


def _top2(vals, lane):
    width = vals.shape[-1]
    m1 = jnp.max(vals, axis=-1, keepdims=True)
    i1 = jnp.min(jnp.where(vals == m1, lane, width), axis=-1, keepdims=True)
    rest = jnp.where(lane == i1, -2.0, vals)
    m2 = jnp.max(rest, axis=-1, keepdims=True)
    i2 = jnp.min(jnp.where(rest == m2, lane, width), axis=-1, keepdims=True)
    return m1, i1, m2, i2


def _router_kernel(x_ref, w_ref, b_ref, info_ref, cnt_ref, carry):
    @pl.when(pl.program_id(0) == 0)
    def _():
        carry[...] = jnp.zeros_like(carry)

    logits = jnp.dot(x_ref[...], w_ref[...], precision=lax.Precision.HIGHEST, preferred_element_type=f32)
    logits = logits + b_ref[...]
    e = jnp.exp(logits - jnp.max(logits, axis=-1, keepdims=True))
    aff = e / jnp.sum(e, axis=-1, keepdims=True)
    tm = aff.shape[0]
    lane = lax.broadcasted_iota(jnp.int32, aff.shape, 1)
    lane_grp = lax.shift_right_logical(lane, int(math.log2(EXPERTS_PER_GROUP)))

    best, grp = None, None
    for gidx in range(N_GROUPS):
        m1, _, m2, _ = _top2(jnp.where(lane_grp == gidx, aff, -1.0), lane)
        score = m1 + m2
        if gidx == 0:
            best, grp = score, jnp.zeros_like(lane[:, :1])
        else:
            better = score > best
            grp = jnp.where(better, gidx, grp)
            best = jnp.where(better, score, best)
    m1, i1, m2, i2 = _top2(jnp.where(lane_grp == grp, aff, -1.0), lane)
    den = m1 + m2

    hot1 = (lane == i1).astype(f32)
    hot2 = (lane == i2).astype(f32)
    both = hot1 + hot2
    row = lax.broadcasted_iota(jnp.int32, (tm, tm), 0)
    col = lax.broadcasted_iota(jnp.int32, (tm, tm), 1)
    before = _dot((col < row).astype(bf16), both.astype(bf16)) + carry[...]
    rank1 = jnp.sum(hot1 * before, axis=-1, keepdims=True)
    rank2 = jnp.sum(hot2 * before, axis=-1, keepdims=True)
    carry[...] += jnp.sum(both, axis=0, keepdims=True)
    cnt_ref[...] = carry[...]

    cols = (i1.astype(f32), i2.astype(f32), m1 / den, m2 / den, rank1, rank2)
    info = jnp.zeros(aff.shape, f32)
    for c, v in enumerate(cols):
        info = jnp.where(lane == c, v, info)
    info_ref[...] = info


def moe_router(x, router_w, router_b, *, tm):
    n, d = x.shape
    tm = min(tm, n)
    assert n % tm == 0
    return pl.pallas_call(
        _router_kernel,
        grid=(n // tm,),
        in_specs=[
            pl.BlockSpec((tm, d), lambda i: (i, 0)),
            pl.BlockSpec((d, N_EXPERTS), lambda i: (0, 0)),
            pl.BlockSpec((1, N_EXPERTS), lambda i: (0, 0)),
        ],
        out_specs=[pl.BlockSpec((tm, N_EXPERTS), lambda i: (i, 0)), pl.BlockSpec((1, N_EXPERTS), lambda i: (0, 0))],
        out_shape=[jax.ShapeDtypeStruct((n, N_EXPERTS), f32), jax.ShapeDtypeStruct((1, N_EXPERTS), f32)],
        scratch_shapes=[pltpu.VMEM((1, N_EXPERTS), f32)],
        compiler_params=_params("arbitrary"),
        name="moe_router",
    )(x, router_w, router_b)


def _row_gather(src_hbm, dst, sem, index_of, n_rows):
    def body(r, carry):
        pltpu.make_async_copy(src_hbm.at[pl.ds(index_of(r), 1), :], dst.at[pl.ds(r, 1), :], sem).start()
        return carry

    lax.fori_loop(0, n_rows, body, 0, unroll=8)


def _row_gather_wait(src_hbm, dst, sem):
    pltpu.make_async_copy(src_hbm.at[pl.ds(0, dst.shape[0]), :], dst, sem).wait()


def _moe_kernel(te_ref, nt_ref, src_ref, x_hbm, cw_ref, wg_ref, wu_ref, wd_ref, o_ref, xbuf, sem):
    t = pl.program_id(0)
    n_live = nt_ref[0]
    tm = xbuf.shape[1]

    def start(tile, slot):
        _row_gather(x_hbm, xbuf.at[slot], sem.at[slot], lambda r: src_ref[tile * tm + r], tm)

    @pl.when(t == 0)
    def _():
        start(0, 0)

    @pl.when(t + 1 < n_live)
    def _():
        start(t + 1, (t + 1) % 2)

    @pl.when(t < n_live)
    def _():
        slot = t % 2
        _row_gather_wait(x_hbm, xbuf.at[slot], sem.at[slot])
        x = xbuf[slot].astype(bf16)
        h = jax.nn.silu(_dot(x, wg_ref[0])) * _dot(x, wu_ref[0])
        o_ref[...] = _dot((h * cw_ref[...]).astype(bf16), wd_ref[0])

    @pl.when(t >= n_live)
    def _():
        o_ref[...] = jnp.zeros_like(o_ref)


def moe_experts(x, src, cw, tile_expert, n_live, w_gate, w_up, w_down, *, tm):
    p = src.shape[0]
    d = x.shape[1]
    fdim = w_gate.shape[2]
    grid_spec = pltpu.PrefetchScalarGridSpec(
        num_scalar_prefetch=3,
        grid=(p // tm,),
        in_specs=[
            pl.BlockSpec(memory_space=pl.ANY),
            pl.BlockSpec((tm, 1), lambda i, te, nt, sr: (i, 0)),
            pl.BlockSpec((1, d, fdim), lambda i, te, nt, sr: (te[i], 0, 0)),
            pl.BlockSpec((1, d, fdim), lambda i, te, nt, sr: (te[i], 0, 0)),
            pl.BlockSpec((1, fdim, d), lambda i, te, nt, sr: (te[i], 0, 0)),
        ],
        out_specs=pl.BlockSpec((tm, d), lambda i, te, nt, sr: (i, 0)),
        scratch_shapes=[pltpu.VMEM((2, tm, d), f32), pltpu.SemaphoreType.DMA((2,))],
    )
    return pl.pallas_call(
        _moe_kernel,
        grid_spec=grid_spec,
        out_shape=jax.ShapeDtypeStruct((p, d), f32),
        compiler_params=_params("arbitrary"),
        name="moe_experts",
    )(tile_expert, n_live, src, x, cw, w_gate, w_up, w_down)


def moe_layer(x, router_w, router_b, w_gate, w_up, w_down, *, tm):
    n = x.shape[0]
    info, counts = moe_router(x, router_w, router_b, tm=512)
    e1, e2 = info[:, 0].astype(jnp.int32), info[:, 1].astype(jnp.int32)
    r1, r2 = info[:, 4].astype(jnp.int32), info[:, 5].astype(jnp.int32)
    cnt = counts[0].astype(jnp.int32)
    padded = (cnt + tm - 1) // tm * tm
    ends = jnp.cumsum(padded)
    starts = ends - padded
    d1, d2 = starts[e1] + r1, starts[e2] + r2
    n_tiles = -(-2 * n // tm) + N_EXPERTS
    p = n_tiles * tm
    tok = jnp.arange(n, dtype=jnp.int32)
    src = jnp.zeros((p,), jnp.int32).at[d1].set(tok).at[d2].set(tok)
    cw = jnp.zeros((p,), f32).at[d1].set(info[:, 2]).at[d2].set(info[:, 3])
    n_live = ends[-1:] // tm
    tile_ix = jnp.arange(n_tiles, dtype=jnp.int32)
    tile_start = jnp.minimum(tile_ix, n_live[0] - 1) * tm
    tile_expert = jnp.sum((ends[None, :] <= tile_start[:, None]).astype(jnp.int32), axis=1)
    ys = moe_experts(x, src, cw[:, None], tile_expert, n_live.astype(jnp.int32), w_gate, w_up, w_down, tm=tm)
    return ys, d1, d2


def _ln_ple_kernel(d1_ref, d2_ref, x_ref, ys_hbm, g_ref, b_ref, p_ref, wg_ref, wp_ref, o_ref, ybuf, sem):
    i = pl.program_id(0)
    tm = x_ref.shape[0]

    def start(tile, slot):
        _row_gather(ys_hbm, ybuf.at[slot, 0], sem.at[slot], lambda r: d1_ref[tile * tm + r], tm)
        _row_gather(ys_hbm, ybuf.at[slot, 1], sem.at[slot], lambda r: d2_ref[tile * tm + r], tm)

    @pl.when(i == 0)
    def _():
        start(0, 0)

    @pl.when(i + 1 < pl.num_programs(0))
    def _():
        start(i + 1, (i + 1) % 2)

    slot = i % 2
    _row_gather_wait(ys_hbm, ybuf.at[slot, 0], sem.at[slot])
    _row_gather_wait(ys_hbm, ybuf.at[slot, 1], sem.at[slot])
    y = ybuf[slot, 0] + ybuf[slot, 1]
    x2 = _layer_norm(DN_ALPHA * x_ref[...] + y, g_ref[...], b_ref[...])
    gate = jax.nn.sigmoid(_dot(x2.astype(bf16), wg_ref[...]))
    o_ref[...] = x2 + gate * _dot(p_ref[...].astype(bf16), wp_ref[...])


def ln_ple(x, ys, d1, d2, g, b, p, w_gate, w_proj, *, tm):
    n, d = x.shape
    tm = min(tm, n)
    assert n % tm == 0
    pd = p.shape[1]
    row = lambda i, a, c: (i, 0)
    fixed = lambda i, a, c: (0, 0)
    grid_spec = pltpu.PrefetchScalarGridSpec(
        num_scalar_prefetch=2,
        grid=(n // tm,),
        in_specs=[
            pl.BlockSpec((tm, d), row), pl.BlockSpec(memory_space=pl.ANY),
            pl.BlockSpec((1, d), fixed), pl.BlockSpec((1, d), fixed),
            pl.BlockSpec((tm, pd), row), pl.BlockSpec((d, d), fixed), pl.BlockSpec((pd, d), fixed),
        ],
        out_specs=pl.BlockSpec((tm, d), row),
        scratch_shapes=[pltpu.VMEM((2, 2, tm, d), f32), pltpu.SemaphoreType.DMA((2,))],
    )
    return pl.pallas_call(
        _ln_ple_kernel,
        grid_spec=grid_spec,
        out_shape=jax.ShapeDtypeStruct((n, d), f32),
        compiler_params=_params("arbitrary"),
        name="ln_ple",
    )(d1, d2, x, ys, g, b, p, w_gate, w_proj)


def _retention_decode_kernel(q_ref, k_ref, v_ref, g_ref, gn_ref, dec_ref, s_ref, o_ref, so_ref):
    q = q_ref[0]
    k = k_ref[0] * (RET_DK ** -0.5)
    v = v_ref[0]
    gamma = dec_ref[0]
    s_prev = s_ref[0, 0]
    att = jnp.sum(q.astype(f32) * k.astype(f32), axis=-1, keepdims=True)
    rows = 8
    q8 = jnp.broadcast_to(q, (rows, RET_DK))
    first = (lax.broadcasted_iota(jnp.int32, (rows, 1), 0) == 0).astype(f32)
    k8 = (jnp.broadcast_to(k.astype(f32), (rows, RET_DK)) * first).astype(bf16)
    v8 = jnp.broadcast_to(v, (rows, RET_DV))
    o = att.astype(bf16).astype(f32) * v.astype(f32) + _dot(q8, s_prev.astype(bf16))[:1] * gamma
    so_ref[0, 0] = s_prev * gamma + _dot_tn(k8, v8)
    mu = jnp.mean(o, axis=-1, keepdims=True)
    d = o - mu
    var = jnp.mean(d * d, axis=-1, keepdims=True)
    o = d * lax.rsqrt(var + LN_EPS) * gn_ref[...]
    o_ref[0] = (jax.nn.silu(g_ref[0].astype(f32)) * o).astype(o_ref.dtype)


def retention_decode(qk, vg, gn_g, states, base):
    b = qk.shape[0]
    h = RET_HEADS
    _, qdec, _ = _retention_decay_tables(1)
    qk3, vg3 = qk[:, None, :], vg[:, None, :]
    o, s = pl.pallas_call(
        _retention_decode_kernel,
        grid=(b, h),
        in_specs=[
            pl.BlockSpec((1, 1, RET_DK), lambda i, hh: (i, 0, hh)),
            pl.BlockSpec((1, 1, RET_DK), lambda i, hh: (i, 0, h + hh)),
            pl.BlockSpec((1, 1, RET_DV), lambda i, hh: (i, 0, hh)),
            pl.BlockSpec((1, 1, RET_DV), lambda i, hh: (i, 0, h + hh)),
            pl.BlockSpec((1, RET_DV), lambda i, hh: (0, hh)),
            pl.BlockSpec((1, 1, 1), lambda i, hh: (hh, 0, 0)),
            pl.BlockSpec((1, 1, RET_DK, RET_DV), lambda i, hh: (base + i, hh, 0, 0)),
        ],
        out_specs=[
            pl.BlockSpec((1, 1, RET_DV), lambda i, hh: (i, 0, hh)),
            pl.BlockSpec((1, 1, RET_DK, RET_DV), lambda i, hh: (i, hh, 0, 0)),
        ],
        out_shape=[jax.ShapeDtypeStruct((b, 1, h * RET_DV), bf16), jax.ShapeDtypeStruct((b, h, RET_DK, RET_DV), f32)],
        compiler_params=_params("parallel", "parallel"),
        name="retention_decode",
    )(qk3, qk3, vg3, vg3, gn_g, qdec, states)
    return o[:, 0, :], s


HIST_PAGES_PER_STEP = 16


def _compress_hist_kernel(pt_ref, *refs, n_pages):
    page_refs = refs[:n_pages]
    pe_ref, w1_ref, w2_ref, o_ref, a_buf, o_buf = refs[n_pages:]
    g = NSA_KV
    upp = page_refs[0].shape[1] // CMP_STRIDE
    rows = n_pages * upp * g
    pad = a_buf.shape[0] - rows

    def load(l):
        return jnp.concatenate([pr[0, pl.ds(l, upp, stride=CMP_STRIDE), :, :].reshape(upp * g, NSA_HD)
                                for pr in page_refs], axis=0)

    a, b = _compress_units(load, pe_ref[0], w1_ref)

    @pl.when(pl.program_id(2) == 0)
    def _():
        a_buf[0:pad, :] = jnp.zeros((pad, a_buf.shape[1]), f32)

    @pl.when(pl.program_id(2) > 0)
    def _():
        a_buf[0:pad, :] = a_buf[rows:rows + pad, :]

    a_buf[pad:pad + rows, :] = a
    a_prev = a_buf[pad - g:pad - g + rows, :]
    o_buf[...] = _dot(jax.nn.gelu(a_prev + b).astype(bf16), w2_ref[0])
    for gg in range(g):
        o_ref[0, 0, gg] = o_buf[pl.ds(gg, rows // g, stride=g), :].astype(o_ref.dtype)


def compress_history(pool, page_table, pe, w1, w2):
    b, ppb = page_table.shape
    page = pool.shape[1]
    g = NSA_KV
    n_pages = min(HIST_PAGES_PER_STEP, ppb)
    assert ppb % n_pages == 0 and page % CMP_STRIDE == 0
    upp = page // CMP_STRIDE
    seg = n_pages * upp
    nu = ppb * upp
    sublanes = 8
    page_spec = lambda k: pl.BlockSpec(
        (1, page, None, g, NSA_HD), lambda i, s, ch, pt: (pt[i, ch * n_pages + k], 0, s, 0, 0))
    grid_spec = pltpu.PrefetchScalarGridSpec(
        num_scalar_prefetch=1,
        grid=(b, 2, ppb // n_pages),
        in_specs=[page_spec(k) for k in range(n_pages)] + [
            pl.BlockSpec((1, CMP_BLOCK, NSA_HD), lambda i, s, ch, pt: (s, 0, 0)),
            pl.BlockSpec((1, CMP_BLOCK * NSA_HD, CMP_HIDDEN), lambda i, s, ch, pt: (s, 0, 0)),
            pl.BlockSpec((1, CMP_HIDDEN, NSA_HD), lambda i, s, ch, pt: (s, 0, 0)),
        ],
        out_specs=pl.BlockSpec((1, 1, g, seg, NSA_HD), lambda i, s, ch, pt: (i, s, 0, ch, 0)),
        scratch_shapes=[pltpu.VMEM((seg * g + sublanes, CMP_HIDDEN), f32), pltpu.VMEM((seg * g, NSA_HD), f32)],
    )
    return pl.pallas_call(
        functools.partial(_compress_hist_kernel, n_pages=n_pages),
        grid_spec=grid_spec,
        out_shape=jax.ShapeDtypeStruct((b, 2, g, nu, NSA_HD), bf16),
        compiler_params=_params("parallel", "parallel", "arbitrary"),
        name="compress_history",
    )(page_table, *([pool] * n_pages), pe, w1, w2)


def _nsa_select_decode_kernel(q_ref, kc_ref, vc_ref, msel_ref, oc_ref, top_ref, *, t, ns, n_top):
    r_heads, hd = NSA_HPG, NSA_HD
    q = q_ref[0]
    qb = jnp.concatenate([q[:, r * hd:(r + 1) * hd] for r in range(r_heads)], axis=0)
    kc = kc_ref[0, 0, 0]
    nu = kc.shape[0]
    s = _dot_nt(qb, kc) * NSA_SCALE
    u = lax.broadcasted_iota(jnp.int32, (1, nu), 1)
    p = _masked_softmax(s, (u >= 1) & (u * CMP_STRIDE + (CMP_STRIDE - 1) <= t))
    oc_ref[0, 0] = _dot(p.astype(bf16), vc_ref[0, 0, 0])
    imp = jnp.sum(p, axis=0, keepdims=True)
    sel = jnp.dot(imp, msel_ref[...], precision=lax.Precision.HIGHEST, preferred_element_type=f32)
    jidx = lax.broadcasted_iota(jnp.int32, sel.shape, 1)
    cur = t // SEL_BLOCK
    forced = (jidx == 0) | (jidx == cur) | (jidx == cur - 1)
    sel = jnp.where(jidx * SEL_BLOCK <= t, jnp.where(forced, FORCE_SCORE, sel), NEG)
    sel = jnp.where(jidx < ns, sel, -jnp.inf)
    width = sel.shape[-1]
    lane = lax.broadcasted_iota(jnp.int32, (1, top_ref.shape[-1]), 1)
    top = jnp.zeros(lane.shape, jnp.int32)
    for it in range(n_top):
        mx = jnp.max(sel, axis=-1, keepdims=True)
        idx = jnp.min(jnp.where(sel == mx, jidx, width), axis=-1, keepdims=True)
        top = jnp.where(lane == it, idx, top)
        sel = jnp.where(jidx == idx, -jnp.inf, sel)
    top_ref[0, 0] = top


def nsa_select_decode(q, cmp, t):
    b = q.shape[0]
    g = NSA_KV
    nu = cmp.shape[3]
    ns = (t + 1 + SEL_BLOCK - 1) // SEL_BLOCK
    ns_pad = -(-ns // LANES) * LANES
    per_sel = SEL_BLOCK // CMP_STRIDE
    u = jnp.arange(nu)[:, None]
    j = jnp.arange(ns_pad)[None, :]
    rb = CMP_BLOCK // CMP_STRIDE
    msel = sum(((u - 1 + r) // per_sel == j).astype(f32) for r in range(rb)) / rb
    msel = jnp.where(u >= 1, msel, 0.0)
    gw = NSA_HPG * NSA_HD
    n_top = min(N_SEL, ns)
    cmp_spec = lambda s: pl.BlockSpec((1, 1, 1, nu, NSA_HD), lambda i, gg: (i, s, gg, 0, 0))
    return pl.pallas_call(
        functools.partial(_nsa_select_decode_kernel, t=t, ns=ns, n_top=n_top),
        grid=(b, g),
        in_specs=[
            pl.BlockSpec((1, 1, gw), lambda i, gg: (i, 0, gg)),
            cmp_spec(0), cmp_spec(1),
            pl.BlockSpec((nu, ns_pad), lambda i, gg: (0, 0)),
        ],
        out_specs=[
            pl.BlockSpec((1, 1, NSA_HPG, NSA_HD), lambda i, gg: (i, gg, 0, 0)),
            pl.BlockSpec((1, 1, 1, LANES), lambda i, gg: (i, gg, 0, 0)),
        ],
        out_shape=[jax.ShapeDtypeStruct((b, g, NSA_HPG, NSA_HD), f32), jax.ShapeDtypeStruct((b, g, 1, LANES), jnp.int32)],
        compiler_params=_params("parallel", "parallel"),
        name="nsa_select_decode",
    )(q[:, None, :], cmp, cmp, msel)


def _nsa_attn_decode_kernel(pt_ref, top_ref, *refs, n_top, n_hist_blocks):
    k_refs, v_refs = refs[:n_top], refs[n_top:2 * n_top]
    (q_ref, oc_ref, knew_ref, vnew_ref, kw_ref, vw_ref, kwnew_ref, vwnew_ref, gl_ref, o_ref) = refs[2 * n_top:]
    i, gg = pl.program_id(0), pl.program_id(1)
    r_heads, hd = NSA_HPG, NSA_HD
    q = q_ref[0]
    qb = jnp.concatenate([q[:, r * hd:(r + 1) * hd] for r in range(r_heads)], axis=0)

    def attend(keys, vals, mask, k_new, v_new):
        s = jnp.where(mask, _dot_nt(qb, keys) * NSA_SCALE, NEG)
        s_new = jnp.sum(qb.astype(f32) * k_new.astype(f32), axis=-1, keepdims=True) * NSA_SCALE
        m = jnp.maximum(jnp.max(s, axis=-1, keepdims=True), s_new)
        e = jnp.where(mask, jnp.exp(s - m), 0.0)
        e_new = jnp.exp(s_new - m)
        num = _dot(e.astype(bf16), vals) + e_new.astype(bf16).astype(f32) * v_new.astype(f32)
        return num / (jnp.sum(e, axis=-1, keepdims=True) + e_new)

    g = NSA_KV

    def rows_of(ref):
        v = ref[0]
        return v.reshape(v.shape[0] * g, hd).astype(bf16)

    def own_group(n_rows):
        return lax.broadcasted_iota(jnp.int32, (1, n_rows * g), 1) % g == gg

    sb = k_refs[0].shape[1]
    keys = jnp.concatenate([rows_of(r) for r in k_refs], axis=0)
    vals = jnp.concatenate([rows_of(r) for r in v_refs], axis=0)
    blk_of_lane = lax.broadcasted_iota(jnp.int32, (1, n_top * sb * g), 1) // (sb * g)
    sel_of_lane = jnp.zeros((1, n_top * sb * g), jnp.int32)
    for n in range(n_top):
        sel_of_lane = jnp.where(blk_of_lane == n, top_ref[i, gg, n], sel_of_lane)
    o_s = attend(keys, vals, (sel_of_lane < n_hist_blocks) & own_group(n_top * sb), knew_ref[0], vnew_ref[0])

    wlen = kw_ref.shape[1]
    wmask = (lax.broadcasted_iota(jnp.int32, (1, wlen * g), 1) >= g) & own_group(wlen)
    o_w = attend(rows_of(kw_ref), rows_of(vw_ref), wmask, kwnew_ref[0], vwnew_ref[0])

    gates = jax.nn.sigmoid(gl_ref[0])
    o_c = oc_ref[0, 0]
    outs = []
    for r in range(r_heads):
        outs.append(gates[:, 3 * r:3 * r + 1] * o_c[r:r + 1] + gates[:, 3 * r + 1:3 * r + 2] * o_s[r:r + 1]
                    + gates[:, 3 * r + 2:3 * r + 3] * o_w[r:r + 1])
    o_ref[0] = jnp.concatenate(outs, axis=1).astype(o_ref.dtype)


def nsa_attn_decode(q, o_c, top, kvb, gl, pool, page_table, win, win_base, t):
    b = q.shape[0]
    g = NSA_KV
    n_top = top.shape[-1]
    page = pool.shape[1]
    assert t % SEL_BLOCK == 0 and page % SEL_BLOCK == 0 and win.shape[1] == WINDOW
    n_hist_blocks = t // SEL_BLOCK
    bpp = page // SEL_BLOCK
    gw = NSA_HPG * NSA_HD

    def blk_spec(n, slot):
        def imap(i, gg, pt, tp):
            j = jnp.minimum(tp[i, gg, n], n_hist_blocks - 1)
            return (pt[i, j // bpp], j % bpp, slot, 0, 0)
        return pl.BlockSpec((1, SEL_BLOCK, None, g, NSA_HD), imap)

    new_spec = lambda slot: pl.BlockSpec((1, 1, NSA_HD), lambda i, gg, pt, tp: (i, 0, slot * g + gg))
    win_spec = lambda slot: pl.BlockSpec((1, WINDOW, None, g, NSA_HD),
                                         lambda i, gg, pt, tp: (win_base + i, 0, slot, 0, 0))
    kvb3 = kvb[:, None, :]
    grid_spec = pltpu.PrefetchScalarGridSpec(
        num_scalar_prefetch=2,
        grid=(b, g),
        in_specs=[blk_spec(n, 2) for n in range(n_top)] + [blk_spec(n, 3) for n in range(n_top)] + [
            pl.BlockSpec((1, 1, gw), lambda i, gg, pt, tp: (i, 0, gg)),
            pl.BlockSpec((1, 1, NSA_HPG, NSA_HD), lambda i, gg, pt, tp: (i, gg, 0, 0)),
            new_spec(2), new_spec(3), win_spec(0), win_spec(1), new_spec(4), new_spec(5),
            pl.BlockSpec((1, 1, LANES), lambda i, gg, pt, tp: (i, 0, gg)),
        ],
        out_specs=pl.BlockSpec((1, 1, gw), lambda i, gg, pt, tp: (i, 0, gg)),
    )
    o = pl.pallas_call(
        functools.partial(_nsa_attn_decode_kernel, n_top=n_top, n_hist_blocks=n_hist_blocks),
        grid_spec=grid_spec,
        out_shape=jax.ShapeDtypeStruct((b, 1, NSA_HEADS * NSA_HD), bf16),
        compiler_params=_params("parallel", "parallel"),
        name="nsa_attn_decode",
    )(page_table, top, *([pool] * (2 * n_top)), q[:, None, :], o_c, kvb3, kvb3, win, win, kvb3, kvb3,
      gl[:, None, :])
    return o[:, 0, :]


def _retention_mixer(x, pos, batch, w, states, base):
    cos, sin = _rope_tables(pos, RET_DK)
    tn = PROJ_TILE
    flags = jnp.ones((w["qk"].shape[1] // tn,), jnp.int32)
    qk, = mm_rope(x, w["qk"], cos, sin, flags, [bf16], hd=RET_DK, tm=PROJ_TILE, tn=tn)
    vg = mm(x, w["vg"], bf16, tm=PROJ_TILE, tn=PROJ_TILE)
    if states is None:
        return retention_prompt(qk, vg, w["gn"], batch)
    return retention_decode(qk, vg, w["gn"], states, base)


def _nsa_projections(x, pos, w):
    cos, sin = _rope_tables(pos, NSA_HD)
    tn = NSA_KV * NSA_HD
    q, = mm_rope(x, w["q"], cos, sin, jnp.ones((NSA_HEADS * NSA_HD // tn,), jnp.int32), [bf16],
                 hd=NSA_HD, tm=PROJ_TILE, tn=tn)
    kvf, kvb = mm_rope(x, w["kv"], cos, sin, jnp.array([1, 0] * 3, jnp.int32), [f32, bf16],
                       hd=NSA_HD, tm=PROJ_TILE, tn=tn)
    gl = mm(x, w["gl"], f32, tm=PROJ_TILE, tn=tn)
    return q, kvf, kvb, gl


def _layer_tail(x, h, p, w, tm_moe):
    x1 = mm_ln(h, w["out"], x, w["ln_g"][0:1], w["ln_b"][0:1], tm=256, tk=2048)
    ys, d1, d2 = moe_layer(x1, w["router_w"], w["router_b"], w["moe_gate"], w["moe_up"], w["moe_down"], tm=tm_moe)
    return ln_ple(x1, ys, d1, d2, w["ln_g"][1:2], w["ln_b"][1:2], p, w["ple_gate"], w["ple_proj"], tm=256)


def kernel(x_prompt, x_sample, state_ret, cache_nsa_kv, state_nsa_win, page_table, p_prompt, p_sample, ret_w_in, ret_w_out, ret_gn_g, nsa_w_in, nsa_w_out, nsa_cmp_pos, nsa_cmp_w1, nsa_cmp_w2, ln_g, ln_b, router_w, router_b, moe_w_gate, moe_w_up, moe_w_down, ple_w_gate, ple_w_proj):
    bp, tp, d = x_prompt.shape
    bs, ts, _ = x_sample.shape
    assert ts == 1
    n_pool, page = cache_nsa_kv.shape[1], cache_nsa_kv.shape[2]
    past = page_table.shape[1] * page
    g, hd = NSA_KV, NSA_HD
    kv_cols = N_KV_SLOTS * g * hd

    xp = x_prompt.reshape(bp * tp, d)
    xs = x_sample.reshape(bs * ts, d)
    pos_p = jnp.arange(tp, dtype=jnp.int32)
    pos_s = jnp.full((bs,), past, jnp.int32)
    states = state_ret.reshape((-1,) + state_ret.shape[2:])
    pool = cache_nsa_kv.reshape((-1,) + cache_nsa_kv.shape[2:])
    wins = state_nsa_win.reshape((-1,) + state_nsa_win.shape[2:])

    ret_p, ret_s, kv_p, kv_s, win_p, win_s = [], [], [], [], [], []
    for i in range(DEPTH):
        j = i // 2
        w = {
            "ln_g": ln_g[i], "ln_b": ln_b[i],
            "router_w": router_w, "router_b": router_b[None, :],
            "moe_gate": moe_w_gate[i].astype(bf16), "moe_up": moe_w_up[i].astype(bf16),
            "moe_down": moe_w_down[i].astype(bf16),
            "ple_gate": ple_w_gate[i].astype(bf16), "ple_proj": ple_w_proj[i].astype(bf16),
        }
        if i % 2 == 0:
            hk = RET_HEADS * RET_DK
            w.update(qk=ret_w_in[j][:, :2 * hk].astype(bf16), vg=ret_w_in[j][:, 2 * hk:].astype(bf16),
                     out=ret_w_out[j].astype(bf16), gn=ret_gn_g[j][None, :])
            hp, sp = _retention_mixer(xp, pos_p, bp, w, None, 0)
            hs, ss = _retention_mixer(xs, pos_s, bs, w, states, j * bs)
            ret_p.append(sp)
            ret_s.append(ss)
        else:
            nq = NSA_HEADS * hd
            w_in = nsa_w_in[j]
            gl = w_in[:, nq + 6 * g * hd:].reshape(d, g, NSA_HPG * 3)
            gl = jnp.pad(gl, ((0, 0), (0, 0), (0, LANES - NSA_HPG * 3))).reshape(d, g * LANES)
            w.update(q=w_in[:, :nq].astype(bf16), kv=w_in[:, nq:nq + 6 * g * hd].astype(bf16), gl=gl.astype(bf16),
                     out=nsa_w_out[j].astype(bf16))
            pe = nsa_cmp_pos[j]
            w1 = nsa_cmp_w1[j].reshape(2, CMP_BLOCK * hd, CMP_HIDDEN).astype(bf16)
            w2 = nsa_cmp_w2[j].astype(bf16)
            q, kvf, kvb, glp = _nsa_projections(xp, pos_p, w)
            cmp = compress_prompt(kvf, pe, w1, w2, bp)
            hp = nsa_attn_prompt(q, cmp, kvb, glp, bp)
            kv_p.append(kvf[:, :kv_cols].reshape(bp, tp, N_KV_SLOTS, g, hd))
            keep = min(WINDOW, tp)
            win_p.append(kvf.reshape(bp, tp, -1)[:, tp - keep:, kv_cols:].reshape(bp, keep, 2, g, hd))
            q, kvf, kvb, gls = _nsa_projections(xs, pos_s, w)
            pt = page_table + j * n_pool
            cmp = compress_history(pool, pt, pe, w1, w2)
            o_c, top = nsa_select_decode(q, cmp, past)
            top = top[:, :, 0, :min(N_SEL, past // SEL_BLOCK + 1)]
            hs = nsa_attn_decode(q, o_c, top, kvb, gls, pool, pt, wins, j * bs, past)
            kv_s.append(kvf[:, :kv_cols].reshape(bs, ts, N_KV_SLOTS, g, hd))
            new_win = kvf[:, kv_cols:].reshape(bs, ts, 2, g, hd)
            win_s.append(jnp.concatenate([state_nsa_win[j][:, ts:], new_win], axis=1))
        xp = _layer_tail(xp, hp, p_prompt[i].reshape(bp * tp, -1), w, MOE_TILE)
        xs = _layer_tail(xs, hs, p_sample[i].reshape(bs * ts, -1), w, 16)

    return (xp.reshape(bp, tp, d), xs.reshape(bs, ts, d), jnp.stack(ret_p), jnp.stack(ret_s),
            jnp.stack(kv_p), jnp.stack(kv_s), jnp.stack(win_p), jnp.stack(win_s))
```

```python
import functools
import math

import jax
import jax.numpy as jnp
from jax import lax
from jax.experimental import pallas as pl
from jax.experimental.pallas import tpu as pltpu

f32 = jnp.float32
bf16 = jnp.bfloat16

D_MODEL = 2048
DEPTH = 4
RET_HEADS = 8
RET_DK = D_MODEL // RET_HEADS
RET_DV = 2 * RET_DK
RET_CHUNK = 128
NSA_HEADS = 16
NSA_HD = D_MODEL // NSA_HEADS
NSA_KV = 4
NSA_HPG = NSA_HEADS // NSA_KV
CMP_BLOCK = 32
CMP_STRIDE = 16
CMP_HIDDEN = 4 * NSA_HD
SEL_BLOCK = 64
N_SEL = 16
WINDOW = 512
Q_BLOCK = 128
N_KV_SLOTS = 4
N_EXPERTS = 16
N_GROUPS = 4
EXPERTS_PER_GROUP = N_EXPERTS // N_GROUPS
D_EXPERT = 1408
PLE_DIM = 256
ROPE_THETA = 10000.0
LN_EPS = 1e-5
DN_ALPHA = (2 * DEPTH) ** 0.25
NEG = -1e30
FORCE_SCORE = float(NSA_HPG + 1)
NSA_SCALE = NSA_HD ** -0.5

V7X_VMEM_BYTES = 64 * 1024 * 1024
VMEM_LIMIT = V7X_VMEM_BYTES - 8 * 1024 * 1024
LANES = 128
MOE_TILE = 256
PROJ_TILE = 1024
RET_HEADS_PER_STEP = 2


def _params(*sem):
    return pltpu.CompilerParams(dimension_semantics=sem, vmem_limit_bytes=VMEM_LIMIT)


def _layer_norm(v, g, b):
    mu = jnp.mean(v, axis=-1, keepdims=True)
    d = v - mu
    var = jnp.mean(d * d, axis=-1, keepdims=True)
    return d * lax.rsqrt(var + LN_EPS) * g + b


def _dot(a, b):
    return jnp.dot(a, b, preferred_element_type=f32)


def _dot_nt(a, b):
    return lax.dot_general(a, b, (((1,), (1,)), ((), ())), preferred_element_type=f32)


def _dot_tn(a, b):
    return lax.dot_general(a, b, (((0,), (0,)), ((), ())), preferred_element_type=f32)


def _rotate_half(v, hd):
    pieces = []
    for c in range(0, v.shape[1], hd):
        if hd == 2 * LANES:
            pieces += [v[:, c + LANES:c + hd], v[:, c:c + LANES]]
        else:
            pieces.append(pltpu.roll(v[:, c:c + hd], hd // 2, 1))
    return jnp.concatenate(pieces, axis=1)


def _mm_rope_kernel(flags_ref, x_ref, w_ref, cos_ref, sin_ref, *o_refs, hd):
    acc = _dot(x_ref[...].astype(bf16), w_ref[...])
    roped = flags_ref[pl.program_id(1)] == 1

    @pl.when(roped)
    def _():
        reps = acc.shape[1] // hd
        cos = jnp.concatenate([cos_ref[...]] * reps, axis=1)
        sin = jnp.concatenate([sin_ref[...]] * reps, axis=1)
        v = acc * cos + _rotate_half(acc, hd) * sin
        for o in o_refs:
            o[...] = v.astype(o.dtype)

    @pl.when(jnp.logical_not(roped))
    def _():
        for o in o_refs:
            o[...] = acc.astype(o.dtype)


def mm_rope(x, w, cols, cos, sin, flags, out_dtypes, *, hd, tm, tn):
    n, k = x.shape
    c0, m = cols
    t = cos.shape[0]
    tm = min(tm, n)
    assert n % tm == 0 and m % tn == 0 and c0 % tn == 0 and t % tm == 0 and tn % hd == 0
    tper = t // tm
    j0 = c0 // tn
    grid_spec = pltpu.PrefetchScalarGridSpec(
        num_scalar_prefetch=1,
        grid=(n // tm, m // tn),
        in_specs=[
            pl.BlockSpec((tm, k), lambda i, j, f: (i, 0)),
            pl.BlockSpec((k, tn), lambda i, j, f: (0, j0 + j)),
            pl.BlockSpec((tm, hd), lambda i, j, f: (i % tper, 0)),
            pl.BlockSpec((tm, hd), lambda i, j, f: (i % tper, 0)),
        ],
        out_specs=[pl.BlockSpec((tm, tn), lambda i, j, f: (i, j)) for _ in out_dtypes],
    )
    return pl.pallas_call(
        functools.partial(_mm_rope_kernel, hd=hd),
        grid_spec=grid_spec,
        out_shape=[jax.ShapeDtypeStruct((n, m), d) for d in out_dtypes],
        compiler_params=_params("parallel", "arbitrary"),
        name="mm_rope",
    )(flags, x, w, cos, sin)


def _mm_kernel(x_ref, w_ref, o_ref):
    o_ref[...] = _dot(x_ref[...].astype(bf16), w_ref[...]).astype(o_ref.dtype)


def mm(x, w, cols, out_dtype, *, tm, tn):
    n, k = x.shape
    c0, m = cols
    tm = min(tm, n)
    tn = min(tn, m)
    assert n % tm == 0 and m % tn == 0 and c0 % tn == 0
    j0 = c0 // tn
    return pl.pallas_call(
        _mm_kernel,
        grid=(n // tm, m // tn),
        in_specs=[pl.BlockSpec((tm, k), lambda i, j: (i, 0)), pl.BlockSpec((k, tn), lambda i, j: (0, j0 + j))],
        out_specs=pl.BlockSpec((tm, tn), lambda i, j: (i, j)),
        out_shape=jax.ShapeDtypeStruct((n, m), out_dtype),
        compiler_params=_params("parallel", "arbitrary"),
        name="mm",
    )(x, w)


def _mm_ln_kernel(a_ref, w_ref, x_ref, g_ref, b_ref, o_ref, acc_ref):
    kk = pl.program_id(1)

    @pl.when(kk == 0)
    def _():
        acc_ref[...] = jnp.zeros_like(acc_ref)

    acc_ref[...] += _dot(a_ref[...].astype(bf16), w_ref[...])

    @pl.when(kk == pl.num_programs(1) - 1)
    def _():
        o_ref[...] = _layer_norm(DN_ALPHA * x_ref[...] + acc_ref[...], g_ref[...], b_ref[...])


def _mm_ln_single_kernel(a_ref, w_ref, x_ref, g_ref, b_ref, o_ref):
    acc = _dot(a_ref[...].astype(bf16), w_ref[...])
    o_ref[...] = _layer_norm(DN_ALPHA * x_ref[...] + acc, g_ref[...], b_ref[...])


def mm_ln(a, w, x, g, b, *, tm, tk):
    n, k = a.shape
    d = w.shape[1]
    tm = min(tm, n)
    tk = min(tk, k)
    assert n % tm == 0 and k % tk == 0
    if tk == k:
        return pl.pallas_call(
            _mm_ln_single_kernel,
            grid=(n // tm,),
            in_specs=[
                pl.BlockSpec((tm, k), lambda i: (i, 0)),
                pl.BlockSpec((k, d), lambda i: (0, 0)),
                pl.BlockSpec((tm, d), lambda i: (i, 0)),
                pl.BlockSpec((1, d), lambda i: (0, 0)),
                pl.BlockSpec((1, d), lambda i: (0, 0)),
            ],
            out_specs=pl.BlockSpec((tm, d), lambda i: (i, 0)),
            out_shape=jax.ShapeDtypeStruct((n, d), f32),
            compiler_params=_params("parallel"),
            name="mm_ln",
        )(a, w, x, g, b)
    return pl.pallas_call(
        _mm_ln_kernel,
        grid=(n // tm, k // tk),
        in_specs=[
            pl.BlockSpec((tm, tk), lambda i, kk: (i, kk)),
            pl.BlockSpec((tk, d), lambda i, kk: (kk, 0)),
            pl.BlockSpec((tm, d), lambda i, kk: (i, 0)),
            pl.BlockSpec((1, d), lambda i, kk: (0, 0)),
            pl.BlockSpec((1, d), lambda i, kk: (0, 0)),
        ],
        out_specs=pl.BlockSpec((tm, d), lambda i, kk: (i, 0)),
        out_shape=jax.ShapeDtypeStruct((n, d), f32),
        scratch_shapes=[pltpu.VMEM((tm, d), f32)],
        compiler_params=_params("parallel", "arbitrary"),
        name="mm_ln",
    )(a, w, x, g, b)


def _retention_kernel(q_ref, k_ref, v_ref, g_ref, gn_ref, intra_ref, qdec_ref, kdec_ref, o_ref, s_ref, state):
    c = pl.program_id(2)

    @pl.when(c == 0)
    def _():
        state[...] = jnp.zeros_like(state)

    for hh in range(state.shape[0]):
        kcols = slice(hh * RET_DK, (hh + 1) * RET_DK)
        vcols = slice(hh * RET_DV, (hh + 1) * RET_DV)
        q = q_ref[:, kcols]
        k = k_ref[:, kcols] * (RET_DK ** -0.5)
        v = v_ref[:, vcols]
        qdec = qdec_ref[hh]
        kdec = kdec_ref[hh]
        cdec = qdec[-1:, :]
        s_prev = state[hh]
        att = _dot_nt(q, k) * intra_ref[hh]
        o = _dot(att.astype(bf16), v) + _dot(q, s_prev.astype(bf16)) * qdec
        kd = (k.astype(f32) * kdec).astype(bf16)
        s_new = s_prev * cdec + _dot_tn(kd, v)
        state[hh] = s_new

        mu = jnp.mean(o, axis=-1, keepdims=True)
        d = o - mu
        var = jnp.mean(d * d, axis=-1, keepdims=True)
        o = d * lax.rsqrt(var + LN_EPS) * gn_ref[:, vcols]
        o_ref[:, vcols] = (jax.nn.silu(g_ref[:, vcols].astype(f32)) * o).astype(o_ref.dtype)

    @pl.when(c == pl.num_programs(2) - 1)
    def _():
        s_ref[0] = state[...]


def _retention_decay_tables(chunk):
    h = jnp.arange(RET_HEADS, dtype=f32)
    log_gamma = jnp.log1p(-jnp.exp2(-5.0 - h))
    idx = jnp.arange(chunk, dtype=f32)
    rel = idx[:, None] - idx[None, :]
    intra = jnp.where(rel >= 0, jnp.exp(jnp.maximum(rel, 0.0)[None] * log_gamma[:, None, None]), 0.0)
    qdec = jnp.exp((idx[None, :] + 1.0) * log_gamma[:, None])[..., None]
    kdec = jnp.exp((chunk - 1.0 - idx[None, :]) * log_gamma[:, None])[..., None]
    return intra, qdec, kdec


def retention_prompt(qk, vg, gn_g, batch):
    n = qk.shape[0]
    t = n // batch
    chunk = math.gcd(t, RET_CHUNK)
    nch = t // chunk
    h = RET_HEADS
    hps = RET_HEADS_PER_STEP
    hb = h // hps
    intra, qdec, kdec = _retention_decay_tables(chunk)
    row = lambda b, hh, c: b * nch + c
    return pl.pallas_call(
        _retention_kernel,
        grid=(batch, hb, nch),
        in_specs=[
            pl.BlockSpec((chunk, hps * RET_DK), lambda b, hh, c: (row(b, hh, c), hh)),
            pl.BlockSpec((chunk, hps * RET_DK), lambda b, hh, c: (row(b, hh, c), hb + hh)),
            pl.BlockSpec((chunk, hps * RET_DV), lambda b, hh, c: (row(b, hh, c), hh)),
            pl.BlockSpec((chunk, hps * RET_DV), lambda b, hh, c: (row(b, hh, c), hb + hh)),
            pl.BlockSpec((1, hps * RET_DV), lambda b, hh, c: (0, hh)),
            pl.BlockSpec((hps, chunk, chunk), lambda b, hh, c: (hh, 0, 0)),
            pl.BlockSpec((hps, chunk, 1), lambda b, hh, c: (hh, 0, 0)),
            pl.BlockSpec((hps, chunk, 1), lambda b, hh, c: (hh, 0, 0)),
        ],
        out_specs=[
            pl.BlockSpec((chunk, hps * RET_DV), lambda b, hh, c: (row(b, hh, c), hh)),
            pl.BlockSpec((1, hps, RET_DK, RET_DV), lambda b, hh, c: (b, hh, 0, 0)),
        ],
        out_shape=[
            jax.ShapeDtypeStruct((n, h * RET_DV), bf16),
            jax.ShapeDtypeStruct((batch, h, RET_DK, RET_DV), f32),
        ],
        scratch_shapes=[pltpu.VMEM((hps, RET_DK, RET_DV), f32)],
        compiler_params=_params("parallel", "parallel", "arbitrary"),
        name="retention_prompt",
    )(qk, qk, vg, vg, gn_g, intra, qdec, kdec)


def _rope_tables(pos, hd):
    half = hd // 2
    inv = ROPE_THETA ** (-jnp.arange(half, dtype=f32) / half)
    ang = pos.astype(f32)[:, None] * inv[None, :]
    cos, sin = jnp.cos(ang), jnp.sin(ang)
    return jnp.concatenate([cos, cos], axis=1), jnp.concatenate([-sin, sin], axis=1)


def _compress_units(load_unit_row, pe, w1_ref):
    xa, xb = [], []
    for l in range(CMP_STRIDE):
        xl = load_unit_row(l)
        xa.append((xl + pe[l:l + 1]).astype(bf16))
        xb.append((xl + pe[CMP_STRIDE + l:CMP_STRIDE + l + 1]).astype(bf16))
    half = CMP_STRIDE * NSA_HD
    a = _dot(jnp.concatenate(xa, axis=1), w1_ref[0, :half])
    b = _dot(jnp.concatenate(xb, axis=1), w1_ref[0, half:])
    return a, b


def _compress_kernel(kv_ref, pe_ref, w1_ref, w2_ref, o_ref):
    nu = kv_ref.shape[0] // CMP_STRIDE
    a, b = _compress_units(lambda l: kv_ref[pl.ds(l, nu, stride=CMP_STRIDE), :], pe_ref[0], w1_ref)
    hid = a + pltpu.roll(b, nu - 1, 0)
    o_ref[0, 0, 0] = _dot(jax.nn.gelu(hid).astype(bf16), w2_ref[0]).astype(o_ref.dtype)


def compress_prompt(kvf, pe, w1, w2, batch):
    n = kvf.shape[0]
    t = n // batch
    nu = t // CMP_STRIDE
    g = NSA_KV
    return pl.pallas_call(
        _compress_kernel,
        grid=(batch, 2, g),
        in_specs=[
            pl.BlockSpec((t, NSA_HD), lambda b, s, gg: (b, s * g + gg)),
            pl.BlockSpec((1, CMP_BLOCK, NSA_HD), lambda b, s, gg: (s, 0, 0)),
            pl.BlockSpec((1, CMP_BLOCK * NSA_HD, CMP_HIDDEN), lambda b, s, gg: (s, 0, 0)),
            pl.BlockSpec((1, CMP_HIDDEN, NSA_HD), lambda b, s, gg: (s, 0, 0)),
        ],
        out_specs=pl.BlockSpec((1, 1, 1, nu, NSA_HD), lambda b, s, gg: (b, s, gg, 0, 0)),
        out_shape=jax.ShapeDtypeStruct((batch, 2, g, nu, NSA_HD), bf16),
        compiler_params=_params("parallel", "parallel", "parallel"),
        name="compress_prompt",
    )(kvf, pe, w1, w2)


def _select_blocks(sel, n_top, axis):
    ns = sel.shape[axis]
    jidx = lax.broadcasted_iota(jnp.int32, sel.shape, axis)
    chosen = jnp.zeros(sel.shape, f32)
    for _ in range(n_top):
        mx = jnp.max(sel, axis=axis, keepdims=True)
        idx = jnp.min(jnp.where(sel == mx, jidx, ns), axis=axis, keepdims=True)
        hit = jidx == idx
        chosen = jnp.where(hit, 1.0, chosen)
        sel = jnp.where(hit, -jnp.inf, sel)
    return chosen


def _masked_softmax(s, mask):
    s = jnp.where(mask, s, NEG)
    e = jnp.exp(s - jnp.max(s, axis=-1, keepdims=True))
    return jnp.where(mask, e / jnp.sum(e, axis=-1, keepdims=True), 0.0)


def _nsa_attn_kernel(q_ref, kc_ref, vc_ref, ks_ref, vs_ref, kw_ref, vw_ref, gl_ref, msel_ref, exp_ref,
                     o_ref, mexp, vs_t, vw_t, vc_t, *, n_top):
    qi = pl.program_id(2)
    blk = q_ref.shape[0]
    r_heads, hd = NSA_HPG, NSA_HD
    t_len = ks_ref.shape[0]
    nc = kc_ref.shape[3]
    ck = 4 * blk

    def transposed(ref_rows):
        return ref_rows.astype(f32).T.astype(bf16)

    @pl.when(qi == 0)
    def _():
        def tr(c, carry):
            off = pl.multiple_of(c * blk, blk)
            vs_t[0:hd, pl.ds(off, blk)] = transposed(vs_ref[pl.ds(off, blk), :])
            vw_t[0:hd, pl.ds(off, blk)] = transposed(vw_ref[pl.ds(off, blk), :])
            return carry

        lax.fori_loop(0, t_len // blk, tr, 0)
        vs_t[hd:, :] = jnp.ones((vs_t.shape[0] - hd, t_len), bf16)
        vw_t[hd:, :] = jnp.ones((vw_t.shape[0] - hd, t_len), bf16)
        for c in range(nc // blk):
            vc_t[:, c * blk:(c + 1) * blk] = transposed(vc_ref[0, 0, 0, c * blk:(c + 1) * blk, :])

    def head(x, r):
        return x[:, r * blk:(r + 1) * blk]

    def per_head(fn):
        return jnp.concatenate([fn(r) for r in range(r_heads)], axis=1)

    q = q_ref[...].astype(f32) * (NSA_SCALE * math.log2(math.e))
    q_t = per_head(lambda r: q[:, r * hd:(r + 1) * hd].T).astype(bf16)
    t_q = qi * blk + lax.broadcasted_iota(jnp.int32, (1, blk), 1)

    z = _dot(kc_ref[0, 0, 0], q_t)
    c_end = lax.broadcasted_iota(jnp.int32, (nc, 1), 0) * CMP_STRIDE + (CMP_BLOCK - 1)
    mc = c_end <= t_q
    z = per_head(lambda r: jnp.where(mc, head(z, r), NEG))
    e = jnp.exp2(z - jnp.max(z, axis=0, keepdims=True))
    p = e * (1.0 / jnp.sum(e, axis=0, keepdims=True))
    p = per_head(lambda r: jnp.where(mc, head(p, r), 0.0))
    oc_t = _dot(vc_t[...], p.astype(bf16))

    imp = head(p, 0)
    for r in range(1, r_heads):
        imp = imp + head(p, r)
    sel = jnp.dot(msel_ref[...], imp, precision=lax.Precision.HIGHEST, preferred_element_type=f32)
    jidx = lax.broadcasted_iota(jnp.int32, sel.shape, 0)
    cur = lax.shift_right_logical(t_q, int(math.log2(SEL_BLOCK)))
    forced = (jidx == 0) | (jidx == cur) | (jidx == cur - 1)
    sel = jnp.where(jidx * SEL_BLOCK <= t_q, jnp.where(forced, FORCE_SCORE, sel), NEG)
    mexp[...] = _dot(exp_ref[...], _select_blocks(sel, n_top, 0).astype(bf16))

    def flash(k_ref, vt_ref, n_chunks, chunk_of, mask_fn):
        def body(i, carry):
            m, acc = carry
            off = pl.multiple_of(chunk_of(i) * ck, ck)
            z = _dot(k_ref[pl.ds(off, ck), :], q_t)
            msk = mask_fn(off, off + lax.broadcasted_iota(jnp.int32, (ck, 1), 0))
            z = per_head(lambda r: jnp.where(msk, head(z, r), NEG))
            m_new = jnp.maximum(m, jnp.max(z, axis=0, keepdims=True))
            alpha = jnp.exp2(m - m_new)
            e = jnp.exp2(z - m_new)
            return m_new, acc * alpha + _dot(vt_ref[:, pl.ds(off, ck)], e.astype(bf16))

        width = r_heads * blk
        init = (jnp.full((1, width), NEG, f32), jnp.zeros((vt_ref.shape[0], width), f32))
        _, acc = lax.fori_loop(0, n_chunks, body, init)
        return acc[0:hd] * (1.0 / acc[hd:hd + 1])

    os_t = flash(ks_ref, vs_t, qi // (ck // blk) + 1, lambda i: i,
                 lambda off, kpos: (mexp[pl.ds(off, ck), :] > 0.5) & (kpos <= t_q))

    span = WINDOW + blk
    w_off = pl.multiple_of(jnp.clip(qi * blk - WINDOW, 0, t_len - span), blk)
    kpos = w_off + lax.broadcasted_iota(jnp.int32, (span, 1), 0)
    in_win = (kpos <= t_q) & (kpos > t_q - WINDOW)
    z = _dot(kw_ref[pl.ds(w_off, span), :], q_t)
    z = per_head(lambda r: jnp.where(in_win, head(z, r), NEG))
    e = jnp.exp2(z - jnp.max(z, axis=0, keepdims=True))
    pv = _dot(vw_t[:, pl.ds(w_off, span)], e.astype(bf16))
    ow_t = pv[0:hd] * (1.0 / pv[hd:hd + 1])

    g_t = jax.nn.sigmoid(gl_ref[...]).T
    outs = []
    for r in range(r_heads):
        o_r = (g_t[3 * r:3 * r + 1] * head(oc_t, r) + g_t[3 * r + 1:3 * r + 2] * head(os_t, r)
               + g_t[3 * r + 2:3 * r + 3] * head(ow_t, r))
        outs.append(o_r.T)
    o_ref[...] = jnp.concatenate(outs, axis=1).astype(o_ref.dtype)


def _selection_constants(nu, ns, nkeys):
    c = jnp.arange(nu)[None, :]
    j = jnp.arange(ns)[:, None]
    per_sel = SEL_BLOCK // CMP_STRIDE
    rb = CMP_BLOCK // CMP_STRIDE
    msel = sum(((c + r) // per_sel == j).astype(f32) for r in range(rb)) / rb
    msel = jnp.where(c < nu - rb + 1, msel, 0.0)
    expand = (jnp.arange(nkeys)[:, None] // SEL_BLOCK == jnp.arange(ns)[None, :]).astype(bf16)
    return msel, expand


def nsa_attn_prompt(q, cmp, kvb, gl, batch):
    n = q.shape[0]
    t = n // batch
    blk = math.gcd(t, Q_BLOCK)
    nq = t // blk
    g = NSA_KV
    nu = t // CMP_STRIDE
    ns = t // SEL_BLOCK
    msel, expand = _selection_constants(nu, ns, t)
    gw = NSA_HPG * NSA_HD
    ones_rows = 16
    kv_spec = lambda slot: pl.BlockSpec((t, NSA_HD), lambda b, gg, qi: (b, slot * g + gg))
    cmp_spec = lambda s: pl.BlockSpec((1, 1, 1, nu, NSA_HD), lambda b, gg, qi: (b, s, gg, 0, 0))
    return pl.pallas_call(
        functools.partial(_nsa_attn_kernel, n_top=min(N_SEL, ns)),
        grid=(batch, g, nq),
        in_specs=[
            pl.BlockSpec((blk, gw), lambda b, gg, qi: (b * nq + qi, gg)),
            cmp_spec(0), cmp_spec(1),
            kv_spec(2), kv_spec(3), kv_spec(4), kv_spec(5),
            pl.BlockSpec((blk, LANES), lambda b, gg, qi: (b * nq + qi, gg)),
            pl.BlockSpec((ns, nu), lambda b, gg, qi: (0, 0)),
            pl.BlockSpec((t, ns), lambda b, gg, qi: (0, 0)),
        ],
        out_specs=pl.BlockSpec((blk, gw), lambda b, gg, qi: (b * nq + qi, gg)),
        out_shape=jax.ShapeDtypeStruct((n, NSA_HEADS * NSA_HD), bf16),
        scratch_shapes=[pltpu.VMEM((t, blk), f32), pltpu.VMEM((NSA_HD + ones_rows, t), bf16),
                        pltpu.VMEM((NSA_HD + ones_rows, t), bf16), pltpu.VMEM((NSA_HD, nu), bf16)],
        compiler_params=_params("parallel", "parallel", "arbitrary"),
        name="nsa_attn_prompt",
    )(q, cmp, cmp, kvb, kvb, kvb, kvb, gl, msel, expand)


def _top2(vals, lane):
    width = vals.shape[-1]
    m1 = jnp.max(vals, axis=-1, keepdims=True)
    i1 = jnp.min(jnp.where(vals == m1, lane, width), axis=-1, keepdims=True)
    rest = jnp.where(lane == i1, -2.0, vals)
    m2 = jnp.max(rest, axis=-1, keepdims=True)
    i2 = jnp.min(jnp.where(rest == m2, lane, width), axis=-1, keepdims=True)
    return m1, i1, m2, i2


def _router_kernel(x_ref, w_ref, b_ref, info_ref, cnt_ref, carry):
    @pl.when(pl.program_id(0) == 0)
    def _():
        carry[...] = jnp.zeros_like(carry)

    logits = jnp.dot(x_ref[...], w_ref[...], precision=lax.Precision.HIGHEST, preferred_element_type=f32)
    logits = logits + b_ref[...]
    e = jnp.exp(logits - jnp.max(logits, axis=-1, keepdims=True))
    aff = e / jnp.sum(e, axis=-1, keepdims=True)
    tm = aff.shape[0]
    lane = lax.broadcasted_iota(jnp.int32, aff.shape, 1)
    lane_grp = lax.shift_right_logical(lane, int(math.log2(EXPERTS_PER_GROUP)))

    best, grp = None, None
    for gidx in range(N_GROUPS):
        m1, _, m2, _ = _top2(jnp.where(lane_grp == gidx, aff, -1.0), lane)
        score = m1 + m2
        if gidx == 0:
            best, grp = score, jnp.zeros_like(lane[:, :1])
        else:
            better = score > best
            grp = jnp.where(better, gidx, grp)
            best = jnp.where(better, score, best)
    m1, i1, m2, i2 = _top2(jnp.where(lane_grp == grp, aff, -1.0), lane)
    den = m1 + m2

    hot1 = (lane == i1).astype(f32)
    hot2 = (lane == i2).astype(f32)
    both = hot1 + hot2
    row = lax.broadcasted_iota(jnp.int32, (tm, tm), 0)
    col = lax.broadcasted_iota(jnp.int32, (tm, tm), 1)
    before = _dot((col < row).astype(bf16), both.astype(bf16)) + carry[...]
    rank1 = jnp.sum(hot1 * before, axis=-1, keepdims=True)
    rank2 = jnp.sum(hot2 * before, axis=-1, keepdims=True)
    carry[...] += jnp.sum(both, axis=0, keepdims=True)
    cnt_ref[...] = carry[...]

    cols = (i1.astype(f32), i2.astype(f32), m1 / den, m2 / den, rank1, rank2)
    info = jnp.zeros(aff.shape, f32)
    for c, v in enumerate(cols):
        info = jnp.where(lane == c, v, info)
    info_ref[...] = info


def moe_router(x, router_w, router_b, *, tm):
    n, d = x.shape
    tm = min(tm, n)
    assert n % tm == 0
    return pl.pallas_call(
        _router_kernel,
        grid=(n // tm,),
        in_specs=[
            pl.BlockSpec((tm, d), lambda i: (i, 0)),
            pl.BlockSpec((d, N_EXPERTS), lambda i: (0, 0)),
            pl.BlockSpec((1, N_EXPERTS), lambda i: (0, 0)),
        ],
        out_specs=[pl.BlockSpec((tm, N_EXPERTS), lambda i: (i, 0)), pl.BlockSpec((1, N_EXPERTS), lambda i: (0, 0))],
        out_shape=[jax.ShapeDtypeStruct((n, N_EXPERTS), f32), jax.ShapeDtypeStruct((1, N_EXPERTS), f32)],
        scratch_shapes=[pltpu.VMEM((1, N_EXPERTS), f32)],
        compiler_params=_params("arbitrary"),
        name="moe_router",
    )(x, router_w, router_b)


def _row_gather(src_hbm, dst, sem, index_of, n_rows):
    def body(r, carry):
        pltpu.make_async_copy(src_hbm.at[pl.ds(index_of(r), 1), :], dst.at[pl.ds(r, 1), :], sem).start()
        return carry

    lax.fori_loop(0, n_rows, body, 0, unroll=8)


def _row_gather_wait(src_hbm, dst, sem):
    pltpu.make_async_copy(src_hbm.at[pl.ds(0, dst.shape[0]), :], dst, sem).wait()


def _moe_kernel(te_ref, nt_ref, src_ref, x_hbm, wg_ref, wu_ref, wd_ref, o_ref, xbuf, sem):
    t = pl.program_id(0)
    n_live = nt_ref[0]
    tm = xbuf.shape[1]

    def start(tile, slot):
        _row_gather(x_hbm, xbuf.at[slot], sem.at[slot], lambda r: src_ref[tile * tm + r], tm)

    @pl.when(t == 0)
    def _():
        start(0, 0)

    @pl.when(t + 1 < n_live)
    def _():
        start(t + 1, (t + 1) % 2)

    @pl.when(t < n_live)
    def _():
        slot = t % 2
        _row_gather_wait(x_hbm, xbuf.at[slot], sem.at[slot])
        x = xbuf[slot].astype(bf16)
        h = jax.nn.silu(_dot(x, wg_ref[0])) * _dot(x, wu_ref[0])
        o_ref[...] = _dot(h.astype(bf16), wd_ref[0])

    @pl.when(t >= n_live)
    def _():
        o_ref[...] = jnp.zeros_like(o_ref)


def moe_experts(x, src, tile_expert, n_live, w_gate, w_up, w_down, *, tm):
    p = src.shape[0]
    d = x.shape[1]
    fdim = w_gate.shape[2]
    grid_spec = pltpu.PrefetchScalarGridSpec(
        num_scalar_prefetch=3,
        grid=(p // tm,),
        in_specs=[
            pl.BlockSpec(memory_space=pl.ANY),
            pl.BlockSpec((1, d, fdim), lambda i, te, nt, sr: (te[i], 0, 0)),
            pl.BlockSpec((1, d, fdim), lambda i, te, nt, sr: (te[i], 0, 0)),
            pl.BlockSpec((1, fdim, d), lambda i, te, nt, sr: (te[i], 0, 0)),
        ],
        out_specs=pl.BlockSpec((tm, d), lambda i, te, nt, sr: (i, 0)),
        scratch_shapes=[pltpu.VMEM((2, tm, d), f32), pltpu.SemaphoreType.DMA((2,))],
    )
    return pl.pallas_call(
        _moe_kernel,
        grid_spec=grid_spec,
        out_shape=jax.ShapeDtypeStruct((p, d), f32),
        compiler_params=_params("arbitrary"),
        name="moe_experts",
    )(tile_expert, n_live, src, x, w_gate, w_up, w_down)


def moe_layer(x, router_w, router_b, w_gate, w_up, w_down, *, tm):
    n = x.shape[0]
    info, counts = moe_router(x, router_w, router_b, tm=512)
    e1, e2 = info[:, 0].astype(jnp.int32), info[:, 1].astype(jnp.int32)
    r1, r2 = info[:, 4].astype(jnp.int32), info[:, 5].astype(jnp.int32)
    cnt = counts[0].astype(jnp.int32)
    padded = (cnt + tm - 1) // tm * tm
    ends = jnp.cumsum(padded)
    starts = ends - padded
    d1, d2 = starts[e1] + r1, starts[e2] + r2
    n_tiles = -(-2 * n // tm) + N_EXPERTS
    p = n_tiles * tm
    tok = jnp.arange(n, dtype=jnp.int32)
    src = jnp.zeros((p,), jnp.int32).at[jnp.concatenate([d1, d2])].set(jnp.concatenate([tok, tok]))
    n_live = ends[-1:] // tm
    tile_ix = jnp.arange(n_tiles, dtype=jnp.int32)
    tile_start = jnp.minimum(tile_ix, n_live[0] - 1) * tm
    tile_expert = jnp.sum((ends[None, :] <= tile_start[:, None]).astype(jnp.int32), axis=1)
    ys = moe_experts(x, src, tile_expert, n_live.astype(jnp.int32), w_gate, w_up, w_down, tm=tm)
    return ys, d1, d2, info


def _ln_ple_kernel(d1_ref, d2_ref, x_ref, info_ref, ys_hbm, g_ref, b_ref, p_ref, wg_ref, wp_ref, o_ref, ybuf, sem):
    i = pl.program_id(0)
    tm = x_ref.shape[0]

    def start(tile, slot):
        _row_gather(ys_hbm, ybuf.at[slot, 0], sem.at[slot], lambda r: d1_ref[tile * tm + r], tm)
        _row_gather(ys_hbm, ybuf.at[slot, 1], sem.at[slot], lambda r: d2_ref[tile * tm + r], tm)

    @pl.when(i == 0)
    def _():
        start(0, 0)

    @pl.when(i + 1 < pl.num_programs(0))
    def _():
        start(i + 1, (i + 1) % 2)

    slot = i % 2
    _row_gather_wait(ys_hbm, ybuf.at[slot, 0], sem.at[slot])
    _row_gather_wait(ys_hbm, ybuf.at[slot, 1], sem.at[slot])
    info = info_ref[...]
    y = info[:, 2:3] * ybuf[slot, 0] + info[:, 3:4] * ybuf[slot, 1]
    x2 = _layer_norm(DN_ALPHA * x_ref[...] + y, g_ref[...], b_ref[...])
    gate = jax.nn.sigmoid(_dot(x2.astype(bf16), wg_ref[...]))
    o_ref[...] = x2 + gate * _dot(p_ref[...].astype(bf16), wp_ref[...])


def ln_ple(x, ys, d1, d2, info, g, b, p, w_gate, w_proj, *, tm):
    n, d = x.shape
    tm = min(tm, n)
    assert n % tm == 0
    pd = p.shape[1]
    row = lambda i, a, c: (i, 0)
    fixed = lambda i, a, c: (0, 0)
    grid_spec = pltpu.PrefetchScalarGridSpec(
        num_scalar_prefetch=2,
        grid=(n // tm,),
        in_specs=[
            pl.BlockSpec((tm, d), row), pl.BlockSpec((tm, info.shape[1]), row), pl.BlockSpec(memory_space=pl.ANY),
            pl.BlockSpec((1, d), fixed), pl.BlockSpec((1, d), fixed),
            pl.BlockSpec((tm, pd), row), pl.BlockSpec((d, d), fixed), pl.BlockSpec((pd, d), fixed),
        ],
        out_specs=pl.BlockSpec((tm, d), row),
        scratch_shapes=[pltpu.VMEM((2, 2, tm, d), f32), pltpu.SemaphoreType.DMA((2,))],
    )
    return pl.pallas_call(
        _ln_ple_kernel,
        grid_spec=grid_spec,
        out_shape=jax.ShapeDtypeStruct((n, d), f32),
        compiler_params=_params("arbitrary"),
        name="ln_ple",
    )(d1, d2, x, info, ys, g, b, p, w_gate, w_proj)


def _retention_decode_kernel(q_ref, k_ref, v_ref, g_ref, gn_ref, dec_ref, s_ref, o_ref, so_ref):
    q = q_ref[0]
    k = k_ref[0] * (RET_DK ** -0.5)
    v = v_ref[0]
    gamma = dec_ref[0]
    s_prev = s_ref[0, 0]
    att = jnp.sum(q.astype(f32) * k.astype(f32), axis=-1, keepdims=True)
    rows = 8
    q8 = jnp.broadcast_to(q, (rows, RET_DK))
    first = (lax.broadcasted_iota(jnp.int32, (rows, 1), 0) == 0).astype(f32)
    k8 = (jnp.broadcast_to(k.astype(f32), (rows, RET_DK)) * first).astype(bf16)
    v8 = jnp.broadcast_to(v, (rows, RET_DV))
    o = att.astype(bf16).astype(f32) * v.astype(f32) + _dot(q8, s_prev.astype(bf16))[:1] * gamma
    so_ref[0, 0] = s_prev * gamma + _dot_tn(k8, v8)
    mu = jnp.mean(o, axis=-1, keepdims=True)
    d = o - mu
    var = jnp.mean(d * d, axis=-1, keepdims=True)
    o = d * lax.rsqrt(var + LN_EPS) * gn_ref[...]
    o_ref[0] = (jax.nn.silu(g_ref[0].astype(f32)) * o).astype(o_ref.dtype)


def retention_decode(qk, vg, gn_g, states, base):
    b = qk.shape[0]
    h = RET_HEADS
    _, qdec, _ = _retention_decay_tables(1)
    qk3, vg3 = qk[:, None, :], vg[:, None, :]
    o, s = pl.pallas_call(
        _retention_decode_kernel,
        grid=(b, h),
        in_specs=[
            pl.BlockSpec((1, 1, RET_DK), lambda i, hh: (i, 0, hh)),
            pl.BlockSpec((1, 1, RET_DK), lambda i, hh: (i, 0, h + hh)),
            pl.BlockSpec((1, 1, RET_DV), lambda i, hh: (i, 0, hh)),
            pl.BlockSpec((1, 1, RET_DV), lambda i, hh: (i, 0, h + hh)),
            pl.BlockSpec((1, RET_DV), lambda i, hh: (0, hh)),
            pl.BlockSpec((1, 1, 1), lambda i, hh: (hh, 0, 0)),
            pl.BlockSpec((1, 1, RET_DK, RET_DV), lambda i, hh: (base + i, hh, 0, 0)),
        ],
        out_specs=[
            pl.BlockSpec((1, 1, RET_DV), lambda i, hh: (i, 0, hh)),
            pl.BlockSpec((1, 1, RET_DK, RET_DV), lambda i, hh: (i, hh, 0, 0)),
        ],
        out_shape=[jax.ShapeDtypeStruct((b, 1, h * RET_DV), bf16), jax.ShapeDtypeStruct((b, h, RET_DK, RET_DV), f32)],
        compiler_params=_params("parallel", "parallel"),
        name="retention_decode",
    )(qk3, qk3, vg3, vg3, gn_g, qdec, states)
    return o[:, 0, :], s


HIST_PAGES_PER_STEP = 16


def _compress_hist_kernel(pt_ref, *refs, n_pages):
    page_refs = refs[:n_pages]
    pe_ref, w1_ref, w2_ref, o_ref, a_buf, o_buf = refs[n_pages:]
    g = NSA_KV
    upp = page_refs[0].shape[1] // CMP_STRIDE
    rows = n_pages * upp * g
    pad = a_buf.shape[0] - rows

    def load(l):
        return jnp.concatenate([pr[0, pl.ds(l, upp, stride=CMP_STRIDE), :, :].reshape(upp * g, NSA_HD)
                                for pr in page_refs], axis=0)

    a, b = _compress_units(load, pe_ref[0], w1_ref)

    @pl.when(pl.program_id(2) == 0)
    def _():
        a_buf[0:pad, :] = jnp.zeros((pad, a_buf.shape[1]), f32)

    @pl.when(pl.program_id(2) > 0)
    def _():
        a_buf[0:pad, :] = a_buf[rows:rows + pad, :]

    a_buf[pad:pad + rows, :] = a
    a_prev = a_buf[pad - g:pad - g + rows, :]
    o_buf[...] = _dot(jax.nn.gelu(a_prev + b).astype(bf16), w2_ref[0])
    for gg in range(g):
        o_ref[0, 0, gg] = o_buf[pl.ds(gg, rows // g, stride=g), :].astype(o_ref.dtype)


def compress_history(pool, page_table, pe, w1, w2):
    b, ppb = page_table.shape
    page = pool.shape[1]
    g = NSA_KV
    n_pages = min(HIST_PAGES_PER_STEP, ppb)
    assert ppb % n_pages == 0 and page % CMP_STRIDE == 0
    upp = page // CMP_STRIDE
    seg = n_pages * upp
    nu = ppb * upp
    sublanes = 8
    page_spec = lambda k: pl.BlockSpec(
        (1, page, None, g, NSA_HD), lambda i, s, ch, pt: (pt[i, ch * n_pages + k], 0, s, 0, 0))
    grid_spec = pltpu.PrefetchScalarGridSpec(
        num_scalar_prefetch=1,
        grid=(b, 2, ppb // n_pages),
        in_specs=[page_spec(k) for k in range(n_pages)] + [
            pl.BlockSpec((1, CMP_BLOCK, NSA_HD), lambda i, s, ch, pt: (s, 0, 0)),
            pl.BlockSpec((1, CMP_BLOCK * NSA_HD, CMP_HIDDEN), lambda i, s, ch, pt: (s, 0, 0)),
            pl.BlockSpec((1, CMP_HIDDEN, NSA_HD), lambda i, s, ch, pt: (s, 0, 0)),
        ],
        out_specs=pl.BlockSpec((1, 1, g, seg, NSA_HD), lambda i, s, ch, pt: (i, s, 0, ch, 0)),
        scratch_shapes=[pltpu.VMEM((seg * g + sublanes, CMP_HIDDEN), f32), pltpu.VMEM((seg * g, NSA_HD), f32)],
    )
    return pl.pallas_call(
        functools.partial(_compress_hist_kernel, n_pages=n_pages),
        grid_spec=grid_spec,
        out_shape=jax.ShapeDtypeStruct((b, 2, g, nu, NSA_HD), bf16),
        compiler_params=_params("parallel", "parallel", "arbitrary"),
        name="compress_history",
    )(page_table, *([pool] * n_pages), pe, w1, w2)


def _nsa_select_decode_kernel(q_ref, kc_ref, vc_ref, msel_ref, oc_ref, top_ref, *, t, ns, n_top):
    r_heads, hd = NSA_HPG, NSA_HD
    q = q_ref[0]
    qb = jnp.concatenate([q[:, r * hd:(r + 1) * hd] for r in range(r_heads)], axis=0)
    kc = kc_ref[0, 0, 0]
    nu = kc.shape[0]
    s = _dot_nt(qb, kc) * NSA_SCALE
    u = lax.broadcasted_iota(jnp.int32, (1, nu), 1)
    p = _masked_softmax(s, (u >= 1) & (u * CMP_STRIDE + (CMP_STRIDE - 1) <= t))
    oc_ref[0, 0] = _dot(p.astype(bf16), vc_ref[0, 0, 0])
    imp = jnp.sum(p, axis=0, keepdims=True)
    sel = jnp.dot(imp, msel_ref[...], precision=lax.Precision.HIGHEST, preferred_element_type=f32)
    jidx = lax.broadcasted_iota(jnp.int32, sel.shape, 1)
    cur = t // SEL_BLOCK
    forced = (jidx == 0) | (jidx == cur) | (jidx == cur - 1)
    sel = jnp.where(jidx * SEL_BLOCK <= t, jnp.where(forced, FORCE_SCORE, sel), NEG)
    sel = jnp.where(jidx < ns, sel, -jnp.inf)
    width = sel.shape[-1]
    lane = lax.broadcasted_iota(jnp.int32, (1, top_ref.shape[-1]), 1)
    top = jnp.zeros(lane.shape, jnp.int32)
    for it in range(n_top):
        mx = jnp.max(sel, axis=-1, keepdims=True)
        idx = jnp.min(jnp.where(sel == mx, jidx, width), axis=-1, keepdims=True)
        top = jnp.where(lane == it, idx, top)
        sel = jnp.where(jidx == idx, -jnp.inf, sel)
    top_ref[0, 0] = top


def nsa_select_decode(q, cmp, t):
    b = q.shape[0]
    g = NSA_KV
    nu = cmp.shape[3]
    ns = (t + 1 + SEL_BLOCK - 1) // SEL_BLOCK
    ns_pad = -(-ns // LANES) * LANES
    per_sel = SEL_BLOCK // CMP_STRIDE
    u = jnp.arange(nu)[:, None]
    j = jnp.arange(ns_pad)[None, :]
    rb = CMP_BLOCK // CMP_STRIDE
    msel = sum(((u - 1 + r) // per_sel == j).astype(f32) for r in range(rb)) / rb
    msel = jnp.where(u >= 1, msel, 0.0)
    gw = NSA_HPG * NSA_HD
    n_top = min(N_SEL, ns)
    cmp_spec = lambda s: pl.BlockSpec((1, 1, 1, nu, NSA_HD), lambda i, gg: (i, s, gg, 0, 0))
    return pl.pallas_call(
        functools.partial(_nsa_select_decode_kernel, t=t, ns=ns, n_top=n_top),
        grid=(b, g),
        in_specs=[
            pl.BlockSpec((1, 1, gw), lambda i, gg: (i, 0, gg)),
            cmp_spec(0), cmp_spec(1),
            pl.BlockSpec((nu, ns_pad), lambda i, gg: (0, 0)),
        ],
        out_specs=[
            pl.BlockSpec((1, 1, NSA_HPG, NSA_HD), lambda i, gg: (i, gg, 0, 0)),
            pl.BlockSpec((1, 1, 1, LANES), lambda i, gg: (i, gg, 0, 0)),
        ],
        out_shape=[jax.ShapeDtypeStruct((b, g, NSA_HPG, NSA_HD), f32), jax.ShapeDtypeStruct((b, g, 1, LANES), jnp.int32)],
        compiler_params=_params("parallel", "parallel"),
        name="nsa_select_decode",
    )(q[:, None, :], cmp, cmp, msel)


def _nsa_attn_decode_kernel(pt_ref, top_ref, *refs, n_top, n_hist_blocks):
    k_refs, v_refs = refs[:n_top], refs[n_top:2 * n_top]
    (q_ref, oc_ref, knew_ref, vnew_ref, kw_ref, vw_ref, kwnew_ref, vwnew_ref, gl_ref, o_ref) = refs[2 * n_top:]
    i, gg = pl.program_id(0), pl.program_id(1)
    r_heads, hd = NSA_HPG, NSA_HD
    q = q_ref[0]
    qb = jnp.concatenate([q[:, r * hd:(r + 1) * hd] for r in range(r_heads)], axis=0)

    def attend(keys, vals, mask, k_new, v_new):
        s = jnp.where(mask, _dot_nt(qb, keys) * NSA_SCALE, NEG)
        s_new = jnp.sum(qb.astype(f32) * k_new.astype(f32), axis=-1, keepdims=True) * NSA_SCALE
        m = jnp.maximum(jnp.max(s, axis=-1, keepdims=True), s_new)
        e = jnp.where(mask, jnp.exp(s - m), 0.0)
        e_new = jnp.exp(s_new - m)
        num = _dot(e.astype(bf16), vals) + e_new.astype(bf16).astype(f32) * v_new.astype(f32)
        return num / (jnp.sum(e, axis=-1, keepdims=True) + e_new)

    g = NSA_KV

    def rows_of(ref):
        v = ref[0]
        return v.reshape(v.shape[0] * g, hd).astype(bf16)

    def own_group(n_rows):
        return lax.broadcasted_iota(jnp.int32, (1, n_rows * g), 1) % g == gg

    sb = k_refs[0].shape[1]
    keys = jnp.concatenate([rows_of(r) for r in k_refs], axis=0)
    vals = jnp.concatenate([rows_of(r) for r in v_refs], axis=0)
    blk_of_lane = lax.broadcasted_iota(jnp.int32, (1, n_top * sb * g), 1) // (sb * g)
    sel_of_lane = jnp.zeros((1, n_top * sb * g), jnp.int32)
    for n in range(n_top):
        sel_of_lane = jnp.where(blk_of_lane == n, top_ref[i, gg, n], sel_of_lane)
    o_s = attend(keys, vals, (sel_of_lane < n_hist_blocks) & own_group(n_top * sb), knew_ref[0], vnew_ref[0])

    wlen = kw_ref.shape[1]
    wmask = (lax.broadcasted_iota(jnp.int32, (1, wlen * g), 1) >= g) & own_group(wlen)
    o_w = attend(rows_of(kw_ref), rows_of(vw_ref), wmask, kwnew_ref[0], vwnew_ref[0])

    gates = jax.nn.sigmoid(gl_ref[0])
    o_c = oc_ref[0, 0]
    outs = []
    for r in range(r_heads):
        outs.append(gates[:, 3 * r:3 * r + 1] * o_c[r:r + 1] + gates[:, 3 * r + 1:3 * r + 2] * o_s[r:r + 1]
                    + gates[:, 3 * r + 2:3 * r + 3] * o_w[r:r + 1])
    o_ref[0] = jnp.concatenate(outs, axis=1).astype(o_ref.dtype)


def nsa_attn_decode(q, o_c, top, kvb, gl, pool, page_table, win, win_base, t):
    b = q.shape[0]
    g = NSA_KV
    n_top = top.shape[-1]
    page = pool.shape[1]
    assert t % SEL_BLOCK == 0 and page % SEL_BLOCK == 0 and win.shape[1] == WINDOW
    n_hist_blocks = t // SEL_BLOCK
    bpp = page // SEL_BLOCK
    gw = NSA_HPG * NSA_HD

    def blk_spec(n, slot):
        def imap(i, gg, pt, tp):
            j = jnp.minimum(tp[i, gg, n], n_hist_blocks - 1)
            return (pt[i, j // bpp], j % bpp, slot, 0, 0)
        return pl.BlockSpec((1, SEL_BLOCK, None, g, NSA_HD), imap)

    new_spec = lambda slot: pl.BlockSpec((1, 1, NSA_HD), lambda i, gg, pt, tp: (i, 0, slot * g + gg))
    win_spec = lambda slot: pl.BlockSpec((1, WINDOW, None, g, NSA_HD),
                                         lambda i, gg, pt, tp: (win_base + i, 0, slot, 0, 0))
    kvb3 = kvb[:, None, :]
    grid_spec = pltpu.PrefetchScalarGridSpec(
        num_scalar_prefetch=2,
        grid=(b, g),
        in_specs=[blk_spec(n, 2) for n in range(n_top)] + [blk_spec(n, 3) for n in range(n_top)] + [
            pl.BlockSpec((1, 1, gw), lambda i, gg, pt, tp: (i, 0, gg)),
            pl.BlockSpec((1, 1, NSA_HPG, NSA_HD), lambda i, gg, pt, tp: (i, gg, 0, 0)),
            new_spec(2), new_spec(3), win_spec(0), win_spec(1), new_spec(4), new_spec(5),
            pl.BlockSpec((1, 1, LANES), lambda i, gg, pt, tp: (i, 0, gg)),
        ],
        out_specs=pl.BlockSpec((1, 1, gw), lambda i, gg, pt, tp: (i, 0, gg)),
    )
    o = pl.pallas_call(
        functools.partial(_nsa_attn_decode_kernel, n_top=n_top, n_hist_blocks=n_hist_blocks),
        grid_spec=grid_spec,
        out_shape=jax.ShapeDtypeStruct((b, 1, NSA_HEADS * NSA_HD), bf16),
        compiler_params=_params("parallel", "parallel"),
        name="nsa_attn_decode",
    )(page_table, top, *([pool] * (2 * n_top)), q[:, None, :], o_c, kvb3, kvb3, win, win, kvb3, kvb3,
      gl[:, None, :])
    return o[:, 0, :]


def _retention_mixer(x, pos, batch, w, states, base):
    cos, sin = _rope_tables(pos, RET_DK)
    tn = PROJ_TILE
    n_qk = 2 * RET_HEADS * RET_DK
    flags = jnp.ones((n_qk // tn,), jnp.int32)
    qk, = mm_rope(x, w["in"], (0, n_qk), cos, sin, flags, [bf16], hd=RET_DK, tm=PROJ_TILE, tn=tn)
    vg = mm(x, w["in"], (n_qk, 2 * RET_HEADS * RET_DV), bf16, tm=PROJ_TILE, tn=PROJ_TILE)
    if states is None:
        return retention_prompt(qk, vg, w["gn"], batch)
    return retention_decode(qk, vg, w["gn"], states, base)


def _nsa_projections(x, pos, w):
    cos, sin = _rope_tables(pos, NSA_HD)
    tn = NSA_KV * NSA_HD
    n_q = NSA_HEADS * NSA_HD
    n_kv = 6 * NSA_KV * NSA_HD
    q, = mm_rope(x, w["in"], (0, n_q), cos, sin, jnp.ones((n_q // tn,), jnp.int32), [bf16],
                 hd=NSA_HD, tm=PROJ_TILE, tn=tn)
    kvf, kvb = mm_rope(x, w["in"], (n_q, n_kv), cos, sin, jnp.array([1, 0] * 3, jnp.int32), [f32, bf16],
                       hd=NSA_HD, tm=PROJ_TILE, tn=tn)
    gl = mm(x, w["gl"], (0, w["gl"].shape[1]), f32, tm=PROJ_TILE, tn=tn)
    return q, kvf, kvb, gl


def _layer_tail(x, h, p, w, tm_moe):
    x1 = mm_ln(h, w["out"], x, w["ln_g"][0:1], w["ln_b"][0:1], tm=512, tk=2048)
    ys, d1, d2, info = moe_layer(x1, w["router_w"], w["router_b"], w["moe_gate"], w["moe_up"], w["moe_down"],
                                 tm=tm_moe)
    return ln_ple(x1, ys, d1, d2, info, w["ln_g"][1:2], w["ln_b"][1:2], p, w["ple_gate"], w["ple_proj"], tm=256)


def kernel(x_prompt, x_sample, state_ret, cache_nsa_kv, state_nsa_win, page_table, p_prompt, p_sample, ret_w_in, ret_w_out, ret_gn_g, nsa_w_in, nsa_w_out, nsa_cmp_pos, nsa_cmp_w1, nsa_cmp_w2, ln_g, ln_b, router_w, router_b, moe_w_gate, moe_w_up, moe_w_down, ple_w_gate, ple_w_proj):
    bp, tp, d = x_prompt.shape
    bs, ts, _ = x_sample.shape
    assert ts == 1
    n_pool, page = cache_nsa_kv.shape[1], cache_nsa_kv.shape[2]
    past = page_table.shape[1] * page
    g, hd = NSA_KV, NSA_HD
    kv_cols = N_KV_SLOTS * g * hd

    xp = x_prompt.reshape(bp * tp, d)
    xs = x_sample.reshape(bs * ts, d)
    pos_p = jnp.arange(tp, dtype=jnp.int32)
    pos_s = jnp.full((bs,), past, jnp.int32)
    states = state_ret.reshape((-1,) + state_ret.shape[2:])
    pool = cache_nsa_kv.reshape((-1,) + cache_nsa_kv.shape[2:])
    wins = state_nsa_win.reshape((-1,) + state_nsa_win.shape[2:])

    ret_p, ret_s, kv_p, kv_s, win_p, win_s = [], [], [], [], [], []
    for i in range(DEPTH):
        j = i // 2
        w = {
            "ln_g": ln_g[i], "ln_b": ln_b[i],
            "router_w": router_w, "router_b": router_b[None, :],
            "moe_gate": moe_w_gate[i].astype(bf16), "moe_up": moe_w_up[i].astype(bf16),
            "moe_down": moe_w_down[i].astype(bf16),
            "ple_gate": ple_w_gate[i].astype(bf16), "ple_proj": ple_w_proj[i].astype(bf16),
        }
        if i % 2 == 0:
            w.update({"in": ret_w_in[j].astype(bf16), "out": ret_w_out[j].astype(bf16), "gn": ret_gn_g[j][None, :]})
            hp, sp = _retention_mixer(xp, pos_p, bp, w, None, 0)
            hs, ss = _retention_mixer(xs, pos_s, bs, w, states, j * bs)
            ret_p.append(sp)
            ret_s.append(ss)
        else:
            nq = NSA_HEADS * hd
            w_in = nsa_w_in[j]
            gl = w_in[:, nq + 6 * g * hd:].reshape(d, g, NSA_HPG * 3)
            gl = jnp.pad(gl, ((0, 0), (0, 0), (0, LANES - NSA_HPG * 3))).reshape(d, g * LANES)
            w.update({"in": w_in.astype(bf16), "gl": gl.astype(bf16), "out": nsa_w_out[j].astype(bf16)})
            pe = nsa_cmp_pos[j]
            w1 = nsa_cmp_w1[j].reshape(2, CMP_BLOCK * hd, CMP_HIDDEN).astype(bf16)
            w2 = nsa_cmp_w2[j].astype(bf16)
            q, kvf, kvb, glp = _nsa_projections(xp, pos_p, w)
            cmp = compress_prompt(kvf, pe, w1, w2, bp)
            hp = nsa_attn_prompt(q, cmp, kvb, glp, bp)
            kv_p.append(kvf[:, :kv_cols].reshape(bp, tp, N_KV_SLOTS, g, hd))
            keep = min(WINDOW, tp)
            win_p.append(kvf.reshape(bp, tp, -1)[:, tp - keep:, kv_cols:].reshape(bp, keep, 2, g, hd))
            q, kvf, kvb, gls = _nsa_projections(xs, pos_s, w)
            pt = page_table + j * n_pool
            cmp = compress_history(pool, pt, pe, w1, w2)
            o_c, top = nsa_select_decode(q, cmp, past)
            top = top[:, :, 0, :min(N_SEL, past // SEL_BLOCK + 1)]
            hs = nsa_attn_decode(q, o_c, top, kvb, gls, pool, pt, wins, j * bs, past)
            kv_s.append(kvf[:, :kv_cols].reshape(bs, ts, N_KV_SLOTS, g, hd))
            new_win = kvf[:, kv_cols:].reshape(bs, ts, 2, g, hd)
            win_s.append(jnp.concatenate([state_nsa_win[j][:, ts:], new_win], axis=1))
        xp = _layer_tail(xp, hp, p_prompt[i].reshape(bp * tp, -1), w, MOE_TILE)
        xs = _layer_tail(xs, hs, p_sample[i].reshape(bs * ts, -1), w, 16)

    return (xp.reshape(bp, tp, d), xs.reshape(bs, ts, d), jnp.stack(ret_p), jnp.stack(ret_s),
            jnp.stack(kv_p), jnp.stack(kv_s), jnp.stack(win_p), jnp.stack(win_s))
```

```python
import functools
import math

import jax
import jax.numpy as jnp
from jax import lax
from jax.experimental import pallas as pl
from jax.experimental.pallas import tpu as pltpu

f32 = jnp.float32
bf16 = jnp.bfloat16

D_MODEL = 2048
DEPTH = 4
RET_HEADS = 8
RET_DK = D_MODEL // RET_HEADS
RET_DV = 2 * RET_DK
RET_CHUNK = 128
NSA_HEADS = 16
NSA_HD = D_MODEL // NSA_HEADS
NSA_KV = 4
NSA_HPG = NSA_HEADS // NSA_KV
CMP_BLOCK = 32
CMP_STRIDE = 16
CMP_HIDDEN = 4 * NSA_HD
SEL_BLOCK = 64
N_SEL = 16
WINDOW = 512
Q_BLOCK = 128
N_KV_SLOTS = 4
N_EXPERTS = 16
N_GROUPS = 4
EXPERTS_PER_GROUP = N_EXPERTS // N_GROUPS
D_EXPERT = 1408
PLE_DIM = 256
ROPE_THETA = 10000.0
LN_EPS = 1e-5
DN_ALPHA = (2 * DEPTH) ** 0.25
NEG = -1e30
FORCE_SCORE = float(NSA_HPG + 1)
NSA_SCALE = NSA_HD ** -0.5

V7X_VMEM_BYTES = 64 * 1024 * 1024
VMEM_LIMIT = V7X_VMEM_BYTES - 8 * 1024 * 1024
LANES = 128
MOE_TILE = 256
PROJ_TILE = 1024
RET_HEADS_PER_STEP = 2


def _params(*sem):
    return pltpu.CompilerParams(dimension_semantics=sem, vmem_limit_bytes=VMEM_LIMIT)


def _layer_norm(v, g, b):
    mu = jnp.mean(v, axis=-1, keepdims=True)
    d = v - mu
    var = jnp.mean(d * d, axis=-1, keepdims=True)
    return d * lax.rsqrt(var + LN_EPS) * g + b


def _dot(a, b):
    return jnp.dot(a, b, preferred_element_type=f32)


def _dot_nt(a, b):
    return lax.dot_general(a, b, (((1,), (1,)), ((), ())), preferred_element_type=f32)


def _dot_tn(a, b):
    return lax.dot_general(a, b, (((0,), (0,)), ((), ())), preferred_element_type=f32)


def _rotate_half(v, hd):
    pieces = []
    for c in range(0, v.shape[1], hd):
        if hd == 2 * LANES:
            pieces += [v[:, c + LANES:c + hd], v[:, c:c + LANES]]
        else:
            pieces.append(pltpu.roll(v[:, c:c + hd], hd // 2, 1))
    return jnp.concatenate(pieces, axis=1)


def _mm_rope_kernel(flags_ref, x_ref, w_ref, cos_ref, sin_ref, *o_refs, hd):
    acc = _dot(x_ref[...].astype(bf16), w_ref[...])
    roped = flags_ref[pl.program_id(1)] == 1

    @pl.when(roped)
    def _():
        reps = acc.shape[1] // hd
        cos = jnp.concatenate([cos_ref[...]] * reps, axis=1)
        sin = jnp.concatenate([sin_ref[...]] * reps, axis=1)
        v = acc * cos + _rotate_half(acc, hd) * sin
        for o in o_refs:
            o[...] = v.astype(o.dtype)

    @pl.when(jnp.logical_not(roped))
    def _():
        for o in o_refs:
            o[...] = acc.astype(o.dtype)


def mm_rope(x, w, cols, cos, sin, flags, out_dtypes, *, hd, tm, tn):
    n, k = x.shape
    c0, m = cols
    t = cos.shape[0]
    tm = min(tm, n)
    assert n % tm == 0 and m % tn == 0 and c0 % tn == 0 and t % tm == 0 and tn % hd == 0
    tper = t // tm
    j0 = c0 // tn
    grid_spec = pltpu.PrefetchScalarGridSpec(
        num_scalar_prefetch=1,
        grid=(n // tm, m // tn),
        in_specs=[
            pl.BlockSpec((tm, k), lambda i, j, f: (i, 0)),
            pl.BlockSpec((k, tn), lambda i, j, f: (0, j0 + j)),
            pl.BlockSpec((tm, hd), lambda i, j, f: (i % tper, 0)),
            pl.BlockSpec((tm, hd), lambda i, j, f: (i % tper, 0)),
        ],
        out_specs=[pl.BlockSpec((tm, tn), lambda i, j, f: (i, j)) for _ in out_dtypes],
    )
    return pl.pallas_call(
        functools.partial(_mm_rope_kernel, hd=hd),
        grid_spec=grid_spec,
        out_shape=[jax.ShapeDtypeStruct((n, m), d) for d in out_dtypes],
        compiler_params=_params("parallel", "arbitrary"),
        name="mm_rope",
    )(flags, x, w, cos, sin)


def _mm_kernel(x_ref, w_ref, o_ref):
    o_ref[...] = _dot(x_ref[...].astype(bf16), w_ref[...]).astype(o_ref.dtype)


def mm(x, w, cols, out_dtype, *, tm, tn):
    n, k = x.shape
    c0, m = cols
    tm = min(tm, n)
    tn = min(tn, m)
    assert n % tm == 0 and m % tn == 0 and c0 % tn == 0
    j0 = c0 // tn
    return pl.pallas_call(
        _mm_kernel,
        grid=(n // tm, m // tn),
        in_specs=[pl.BlockSpec((tm, k), lambda i, j: (i, 0)), pl.BlockSpec((k, tn), lambda i, j: (0, j0 + j))],
        out_specs=pl.BlockSpec((tm, tn), lambda i, j: (i, j)),
        out_shape=jax.ShapeDtypeStruct((n, m), out_dtype),
        compiler_params=_params("parallel", "arbitrary"),
        name="mm",
    )(x, w)


def _mm_ln_kernel(a_ref, w_ref, x_ref, g_ref, b_ref, o_ref, acc_ref):
    kk = pl.program_id(1)

    @pl.when(kk == 0)
    def _():
        acc_ref[...] = jnp.zeros_like(acc_ref)

    acc_ref[...] += _dot(a_ref[...].astype(bf16), w_ref[...])

    @pl.when(kk == pl.num_programs(1) - 1)
    def _():
        o_ref[...] = _layer_norm(DN_ALPHA * x_ref[...] + acc_ref[...], g_ref[...], b_ref[...])


def _mm_ln_single_kernel(a_ref, w_ref, x_ref, g_ref, b_ref, o_ref):
    acc = _dot(a_ref[...].astype(bf16), w_ref[...])
    o_ref[...] = _layer_norm(DN_ALPHA * x_ref[...] + acc, g_ref[...], b_ref[...])


def mm_ln(a, w, x, g, b, *, tm, tk):
    n, k = a.shape
    d = w.shape[1]
    tm = min(tm, n)
    tk = min(tk, k)
    assert n % tm == 0 and k % tk == 0
    if tk == k:
        return pl.pallas_call(
            _mm_ln_single_kernel,
            grid=(n // tm,),
            in_specs=[
                pl.BlockSpec((tm, k), lambda i: (i, 0)),
                pl.BlockSpec((k, d), lambda i: (0, 0)),
                pl.BlockSpec((tm, d), lambda i: (i, 0)),
                pl.BlockSpec((1, d), lambda i: (0, 0)),
                pl.BlockSpec((1, d), lambda i: (0, 0)),
            ],
            out_specs=pl.BlockSpec((tm, d), lambda i: (i, 0)),
            out_shape=jax.ShapeDtypeStruct((n, d), f32),
            compiler_params=_params("parallel"),
            name="mm_ln",
        )(a, w, x, g, b)
    return pl.pallas_call(
        _mm_ln_kernel,
        grid=(n // tm, k // tk),
        in_specs=[
            pl.BlockSpec((tm, tk), lambda i, kk: (i, kk)),
            pl.BlockSpec((tk, d), lambda i, kk: (kk, 0)),
            pl.BlockSpec((tm, d), lambda i, kk: (i, 0)),
            pl.BlockSpec((1, d), lambda i, kk: (0, 0)),
            pl.BlockSpec((1, d), lambda i, kk: (0, 0)),
        ],
        out_specs=pl.BlockSpec((tm, d), lambda i, kk: (i, 0)),
        out_shape=jax.ShapeDtypeStruct((n, d), f32),
        scratch_shapes=[pltpu.VMEM((tm, d), f32)],
        compiler_params=_params("parallel", "arbitrary"),
        name="mm_ln",
    )(a, w, x, g, b)


def _retention_kernel(q_ref, k_ref, v_ref, g_ref, gn_ref, intra_ref, qdec_ref, kdec_ref, o_ref, s_ref, state):
    c = pl.program_id(2)

    @pl.when(c == 0)
    def _():
        state[...] = jnp.zeros_like(state)

    for hh in range(state.shape[0]):
        kcols = slice(hh * RET_DK, (hh + 1) * RET_DK)
        vcols = slice(hh * RET_DV, (hh + 1) * RET_DV)
        q = q_ref[:, kcols]
        k = k_ref[:, kcols] * (RET_DK ** -0.5)
        v = v_ref[:, vcols]
        qdec = qdec_ref[hh]
        kdec = kdec_ref[hh]
        cdec = qdec[-1:, :]
        s_prev = state[hh]
        att = _dot_nt(q, k) * intra_ref[hh]
        o = _dot(att.astype(bf16), v) + _dot(q, s_prev.astype(bf16)) * qdec
        kd = (k.astype(f32) * kdec).astype(bf16)
        s_new = s_prev * cdec + _dot_tn(kd, v)
        state[hh] = s_new

        mu = jnp.mean(o, axis=-1, keepdims=True)
        d = o - mu
        var = jnp.mean(d * d, axis=-1, keepdims=True)
        o = d * lax.rsqrt(var + LN_EPS) * gn_ref[:, vcols]
        o_ref[:, vcols] = (jax.nn.silu(g_ref[:, vcols].astype(f32)) * o).astype(o_ref.dtype)

    @pl.when(c == pl.num_programs(2) - 1)
    def _():
        s_ref[0] = state[...]


def _retention_decay_tables(chunk):
    h = jnp.arange(RET_HEADS, dtype=f32)
    log_gamma = jnp.log1p(-jnp.exp2(-5.0 - h))
    idx = jnp.arange(chunk, dtype=f32)
    rel = idx[:, None] - idx[None, :]
    intra = jnp.where(rel >= 0, jnp.exp(jnp.maximum(rel, 0.0)[None] * log_gamma[:, None, None]), 0.0)
    qdec = jnp.exp((idx[None, :] + 1.0) * log_gamma[:, None])[..., None]
    kdec = jnp.exp((chunk - 1.0 - idx[None, :]) * log_gamma[:, None])[..., None]
    return intra, qdec, kdec


def retention_prompt(qk, vg, gn_g, batch):
    n = qk.shape[0]
    t = n // batch
    chunk = math.gcd(t, RET_CHUNK)
    nch = t // chunk
    h = RET_HEADS
    hps = RET_HEADS_PER_STEP
    hb = h // hps
    intra, qdec, kdec = _retention_decay_tables(chunk)
    row = lambda b, hh, c: b * nch + c
    return pl.pallas_call(
        _retention_kernel,
        grid=(batch, hb, nch),
        in_specs=[
            pl.BlockSpec((chunk, hps * RET_DK), lambda b, hh, c: (row(b, hh, c), hh)),
            pl.BlockSpec((chunk, hps * RET_DK), lambda b, hh, c: (row(b, hh, c), hb + hh)),
            pl.BlockSpec((chunk, hps * RET_DV), lambda b, hh, c: (row(b, hh, c), hh)),
            pl.BlockSpec((chunk, hps * RET_DV), lambda b, hh, c: (row(b, hh, c), hb + hh)),
            pl.BlockSpec((1, hps * RET_DV), lambda b, hh, c: (0, hh)),
            pl.BlockSpec((hps, chunk, chunk), lambda b, hh, c: (hh, 0, 0)),
            pl.BlockSpec((hps, chunk, 1), lambda b, hh, c: (hh, 0, 0)),
            pl.BlockSpec((hps, chunk, 1), lambda b, hh, c: (hh, 0, 0)),
        ],
        out_specs=[
            pl.BlockSpec((chunk, hps * RET_DV), lambda b, hh, c: (row(b, hh, c), hh)),
            pl.BlockSpec((1, hps, RET_DK, RET_DV), lambda b, hh, c: (b, hh, 0, 0)),
        ],
        out_shape=[
            jax.ShapeDtypeStruct((n, h * RET_DV), bf16),
            jax.ShapeDtypeStruct((batch, h, RET_DK, RET_DV), f32),
        ],
        scratch_shapes=[pltpu.VMEM((hps, RET_DK, RET_DV), f32)],
        compiler_params=_params("parallel", "parallel", "arbitrary"),
        name="retention_prompt",
    )(qk, qk, vg, vg, gn_g, intra, qdec, kdec)


def _rope_tables(pos, hd):
    half = hd // 2
    inv = ROPE_THETA ** (-jnp.arange(half, dtype=f32) / half)
    ang = pos.astype(f32)[:, None] * inv[None, :]
    cos, sin = jnp.cos(ang), jnp.sin(ang)
    return jnp.concatenate([cos, cos], axis=1), jnp.concatenate([-sin, sin], axis=1)


def _compress_units(load_unit_row, pe, w1_ref):
    xa, xb = [], []
    for l in range(CMP_STRIDE):
        xl = load_unit_row(l)
        xa.append((xl + pe[l:l + 1]).astype(bf16))
        xb.append((xl + pe[CMP_STRIDE + l:CMP_STRIDE + l + 1]).astype(bf16))
    half = CMP_STRIDE * NSA_HD
    a = _dot(jnp.concatenate(xa, axis=1), w1_ref[0, :half])
    b = _dot(jnp.concatenate(xb, axis=1), w1_ref[0, half:])
    return a, b


def _compress_kernel(kv_ref, pe_ref, w1_ref, w2_ref, o_ref):
    nu = kv_ref.shape[0] // CMP_STRIDE
    a, b = _compress_units(lambda l: kv_ref[pl.ds(l, nu, stride=CMP_STRIDE), :], pe_ref[0], w1_ref)
    hid = a + pltpu.roll(b, nu - 1, 0)
    o_ref[0, 0, 0] = _dot(jax.nn.gelu(hid).astype(bf16), w2_ref[0]).astype(o_ref.dtype)


def compress_prompt(kvf, pe, w1, w2, batch):
    n = kvf.shape[0]
    t = n // batch
    nu = t // CMP_STRIDE
    g = NSA_KV
    return pl.pallas_call(
        _compress_kernel,
        grid=(batch, 2, g),
        in_specs=[
            pl.BlockSpec((t, NSA_HD), lambda b, s, gg: (b, s * g + gg)),
            pl.BlockSpec((1, CMP_BLOCK, NSA_HD), lambda b, s, gg: (s, 0, 0)),
            pl.BlockSpec((1, CMP_BLOCK * NSA_HD, CMP_HIDDEN), lambda b, s, gg: (s, 0, 0)),
            pl.BlockSpec((1, CMP_HIDDEN, NSA_HD), lambda b, s, gg: (s, 0, 0)),
        ],
        out_specs=pl.BlockSpec((1, 1, 1, nu, NSA_HD), lambda b, s, gg: (b, s, gg, 0, 0)),
        out_shape=jax.ShapeDtypeStruct((batch, 2, g, nu, NSA_HD), bf16),
        compiler_params=_params("parallel", "parallel", "parallel"),
        name="compress_prompt",
    )(kvf, pe, w1, w2)


def _select_blocks(sel, n_top, axis):
    ns = sel.shape[axis]
    jidx = lax.broadcasted_iota(jnp.int32, sel.shape, axis)
    chosen = jnp.zeros(sel.shape, f32)
    for _ in range(n_top):
        mx = jnp.max(sel, axis=axis, keepdims=True)
        idx = jnp.min(jnp.where(sel == mx, jidx, ns), axis=axis, keepdims=True)
        hit = jidx == idx
        chosen = jnp.where(hit, 1.0, chosen)
        sel = jnp.where(hit, -jnp.inf, sel)
    return chosen


def _masked_softmax(s, mask):
    s = jnp.where(mask, s, NEG)
    e = jnp.exp(s - jnp.max(s, axis=-1, keepdims=True))
    return jnp.where(mask, e / jnp.sum(e, axis=-1, keepdims=True), 0.0)


def _nsa_attn_kernel(q_ref, kc_ref, vc_ref, ks_ref, vs_ref, kw_ref, vw_ref, gl_ref, msel_ref, exp_ref,
                     o_ref, vs_t, vw_t, vc_t, *, n_top):
    qi = pl.program_id(2)
    blk = q_ref.shape[0]
    r_heads, hd = NSA_HPG, NSA_HD
    t_len = ks_ref.shape[0]
    nc = kc_ref.shape[3]
    ck = 4 * blk

    def transposed(ref_rows):
        return ref_rows.astype(f32).T.astype(bf16)

    @pl.when(qi == 0)
    def _():
        def tr(c, carry):
            off = pl.multiple_of(c * blk, blk)
            vs_t[0:hd, pl.ds(off, blk)] = transposed(vs_ref[pl.ds(off, blk), :])
            vw_t[0:hd, pl.ds(off, blk)] = transposed(vw_ref[pl.ds(off, blk), :])
            return carry

        lax.fori_loop(0, t_len // blk, tr, 0)
        vs_t[hd:, :] = jnp.ones((vs_t.shape[0] - hd, t_len), bf16)
        vw_t[hd:, :] = jnp.ones((vw_t.shape[0] - hd, t_len), bf16)
        for c in range(nc // blk):
            vc_t[:, c * blk:(c + 1) * blk] = transposed(vc_ref[0, 0, 0, c * blk:(c + 1) * blk, :])

    def head(x, r):
        return x[:, r * blk:(r + 1) * blk]

    def per_head(fn):
        return jnp.concatenate([fn(r) for r in range(r_heads)], axis=1)

    q = q_ref[...].astype(f32) * (NSA_SCALE * math.log2(math.e))
    q_t = per_head(lambda r: q[:, r * hd:(r + 1) * hd].T).astype(bf16)
    t_q = qi * blk + lax.broadcasted_iota(jnp.int32, (1, blk), 1)

    z = _dot(kc_ref[0, 0, 0], q_t)
    c_end = lax.broadcasted_iota(jnp.int32, (nc, 1), 0) * CMP_STRIDE + (CMP_BLOCK - 1)
    mc = c_end <= t_q
    z = per_head(lambda r: jnp.where(mc, head(z, r), NEG))
    e = jnp.exp2(z - jnp.max(z, axis=0, keepdims=True))
    p = e * (1.0 / jnp.sum(e, axis=0, keepdims=True))
    p = per_head(lambda r: jnp.where(mc, head(p, r), 0.0))
    oc_t = _dot(vc_t[...], p.astype(bf16))

    span = WINDOW + blk
    w_off = pl.multiple_of(jnp.clip(qi * blk - WINDOW, 0, t_len - span), blk)
    kpos = w_off + lax.broadcasted_iota(jnp.int32, (span, 1), 0)
    in_win = (kpos <= t_q) & (kpos > t_q - WINDOW)
    z = _dot(kw_ref[pl.ds(w_off, span), :], q_t)
    z = per_head(lambda r: jnp.where(in_win, head(z, r), NEG))
    e = jnp.exp2(z - jnp.max(z, axis=0, keepdims=True))
    pv = _dot(vw_t[:, pl.ds(w_off, span)], e.astype(bf16))
    ow_t = pv[0:hd] * (1.0 / pv[hd:hd + 1])

    imp = head(p, 0)
    for r in range(1, r_heads):
        imp = imp + head(p, r)
    sel = jnp.dot(msel_ref[...], imp, precision=lax.Precision.HIGHEST, preferred_element_type=f32)
    jidx = lax.broadcasted_iota(jnp.int32, sel.shape, 0)
    cur = lax.shift_right_logical(t_q, int(math.log2(SEL_BLOCK)))
    forced = (jidx == 0) | (jidx == cur) | (jidx == cur - 1)
    sel = jnp.where(jidx * SEL_BLOCK <= t_q, jnp.where(forced, FORCE_SCORE, sel), NEG)
    chosen = _select_blocks(sel, n_top, 0).astype(bf16)

    def flash(k_ref, vt_ref, n_chunks, chunk_of, mask_fn):
        def body(i, carry):
            m, acc = carry
            off = pl.multiple_of(chunk_of(i) * ck, ck)
            z = _dot(k_ref[pl.ds(off, ck), :], q_t)
            msk = mask_fn(off, off + lax.broadcasted_iota(jnp.int32, (ck, 1), 0))
            z = per_head(lambda r: jnp.where(msk, head(z, r), NEG))
            m_new = jnp.maximum(m, jnp.max(z, axis=0, keepdims=True))
            alpha = jnp.exp2(m - m_new)
            e = jnp.exp2(z - m_new)
            return m_new, acc * alpha + _dot(vt_ref[:, pl.ds(off, ck)], e.astype(bf16))

        width = r_heads * blk
        init = (jnp.full((1, width), NEG, f32), jnp.zeros((vt_ref.shape[0], width), f32))
        _, acc = lax.fori_loop(0, n_chunks, body, init)
        return acc[0:hd] * (1.0 / acc[hd:hd + 1])

    os_t = flash(ks_ref, vs_t, qi // (ck // blk) + 1, lambda i: i,
                 lambda off, kpos: (_dot(exp_ref[pl.ds(off, ck), :], chosen) > 0.5) & (kpos <= t_q))

    g_t = jax.nn.sigmoid(gl_ref[...]).T
    outs = []
    for r in range(r_heads):
        o_r = (g_t[3 * r:3 * r + 1] * head(oc_t, r) + g_t[3 * r + 1:3 * r + 2] * head(os_t, r)
               + g_t[3 * r + 2:3 * r + 3] * head(ow_t, r))
        outs.append(o_r.T)
    o_ref[...] = jnp.concatenate(outs, axis=1).astype(o_ref.dtype)


def _selection_constants(nu, ns, nkeys):
    c = jnp.arange(nu)[None, :]
    j = jnp.arange(ns)[:, None]
    per_sel = SEL_BLOCK // CMP_STRIDE
    rb = CMP_BLOCK // CMP_STRIDE
    msel = sum(((c + r) // per_sel == j).astype(f32) for r in range(rb)) / rb
    msel = jnp.where(c < nu - rb + 1, msel, 0.0)
    expand = (jnp.arange(nkeys)[:, None] // SEL_BLOCK == jnp.arange(ns)[None, :]).astype(bf16)
    return msel, expand


def nsa_attn_prompt(q, cmp, kvb, gl, batch):
    n = q.shape[0]
    t = n // batch
    blk = math.gcd(t, Q_BLOCK)
    nq = t // blk
    g = NSA_KV
    nu = t // CMP_STRIDE
    ns = t // SEL_BLOCK
    msel, expand = _selection_constants(nu, ns, t)
    gw = NSA_HPG * NSA_HD
    ones_rows = 16
    kv_spec = lambda slot: pl.BlockSpec((t, NSA_HD), lambda b, gg, qi: (b, slot * g + gg))
    cmp_spec = lambda s: pl.BlockSpec((1, 1, 1, nu, NSA_HD), lambda b, gg, qi: (b, s, gg, 0, 0))
    return pl.pallas_call(
        functools.partial(_nsa_attn_kernel, n_top=min(N_SEL, ns)),
        grid=(batch, g, nq),
        in_specs=[
            pl.BlockSpec((blk, gw), lambda b, gg, qi: (b * nq + qi, gg)),
            cmp_spec(0), cmp_spec(1),
            kv_spec(2), kv_spec(3), kv_spec(4), kv_spec(5),
            pl.BlockSpec((blk, LANES), lambda b, gg, qi: (b * nq + qi, gg)),
            pl.BlockSpec((ns, nu), lambda b, gg, qi: (0, 0)),
            pl.BlockSpec((t, ns), lambda b, gg, qi: (0, 0)),
        ],
        out_specs=pl.BlockSpec((blk, gw), lambda b, gg, qi: (b * nq + qi, gg)),
        out_shape=jax.ShapeDtypeStruct((n, NSA_HEADS * NSA_HD), bf16),
        scratch_shapes=[pltpu.VMEM((NSA_HD + ones_rows, t), bf16), pltpu.VMEM((NSA_HD + ones_rows, t), bf16),
                        pltpu.VMEM((NSA_HD, nu), bf16)],
        compiler_params=_params("parallel", "parallel", "arbitrary"),
        name="nsa_attn_prompt",
    )(q, cmp, cmp, kvb, kvb, kvb, kvb, gl, msel, expand)


def _top2(vals, lane):
    width = vals.shape[-1]
    m1 = jnp.max(vals, axis=-1, keepdims=True)
    i1 = jnp.min(jnp.where(vals == m1, lane, width), axis=-1, keepdims=True)
    rest = jnp.where(lane == i1, -2.0, vals)
    m2 = jnp.max(rest, axis=-1, keepdims=True)
    i2 = jnp.min(jnp.where(rest == m2, lane, width), axis=-1, keepdims=True)
    return m1, i1, m2, i2


def _router_kernel(x_ref, w_ref, b_ref, info_ref, cnt_ref, carry):
    @pl.when(pl.program_id(0) == 0)
    def _():
        carry[...] = jnp.zeros_like(carry)

    logits = jnp.dot(x_ref[...], w_ref[...], precision=lax.Precision.HIGHEST, preferred_element_type=f32)
    logits = logits + b_ref[...]
    e = jnp.exp(logits - jnp.max(logits, axis=-1, keepdims=True))
    aff = e / jnp.sum(e, axis=-1, keepdims=True)
    tm = aff.shape[0]
    lane = lax.broadcasted_iota(jnp.int32, aff.shape, 1)
    lane_grp = lax.shift_right_logical(lane, int(math.log2(EXPERTS_PER_GROUP)))

    best, grp = None, None
    for gidx in range(N_GROUPS):
        m1, _, m2, _ = _top2(jnp.where(lane_grp == gidx, aff, -1.0), lane)
        score = m1 + m2
        if gidx == 0:
            best, grp = score, jnp.zeros_like(lane[:, :1])
        else:
            better = score > best
            grp = jnp.where(better, gidx, grp)
            best = jnp.where(better, score, best)
    m1, i1, m2, i2 = _top2(jnp.where(lane_grp == grp, aff, -1.0), lane)
    den = m1 + m2

    hot1 = (lane == i1).astype(f32)
    hot2 = (lane == i2).astype(f32)
    both = hot1 + hot2
    row = lax.broadcasted_iota(jnp.int32, (tm, tm), 0)
    col = lax.broadcasted_iota(jnp.int32, (tm, tm), 1)
    before = _dot((col < row).astype(bf16), both.astype(bf16)) + carry[...]
    rank1 = jnp.sum(hot1 * before, axis=-1, keepdims=True)
    rank2 = jnp.sum(hot2 * before, axis=-1, keepdims=True)
    carry[...] += jnp.sum(both, axis=0, keepdims=True)
    cnt_ref[...] = carry[...]

    cols = (i1.astype(f32), i2.astype(f32), m1 / den, m2 / den, rank1, rank2)
    info = jnp.zeros(aff.shape, f32)
    for c, v in enumerate(cols):
        info = jnp.where(lane == c, v, info)
    info_ref[...] = info


def moe_router(x, router_w, router_b, *, tm):
    n, d = x.shape
    tm = min(tm, n)
    assert n % tm == 0
    return pl.pallas_call(
        _router_kernel,
        grid=(n // tm,),
        in_specs=[
            pl.BlockSpec((tm, d), lambda i: (i, 0)),
            pl.BlockSpec((d, N_EXPERTS), lambda i: (0, 0)),
            pl.BlockSpec((1, N_EXPERTS), lambda i: (0, 0)),
        ],
        out_specs=[pl.BlockSpec((tm, N_EXPERTS), lambda i: (i, 0)), pl.BlockSpec((1, N_EXPERTS), lambda i: (0, 0))],
        out_shape=[jax.ShapeDtypeStruct((n, N_EXPERTS), f32), jax.ShapeDtypeStruct((1, N_EXPERTS), f32)],
        scratch_shapes=[pltpu.VMEM((1, N_EXPERTS), f32)],
        compiler_params=_params("arbitrary"),
        name="moe_router",
    )(x, router_w, router_b)


def _row_gather(src_hbm, dst, sem, index_of, n_rows):
    def body(r, carry):
        pltpu.make_async_copy(src_hbm.at[pl.ds(index_of(r), 1), :], dst.at[pl.ds(r, 1), :], sem).start()
        return carry

    lax.fori_loop(0, n_rows, body, 0, unroll=8)


def _row_gather_wait(src_hbm, dst, sem):
    pltpu.make_async_copy(src_hbm.at[pl.ds(0, dst.shape[0]), :], dst, sem).wait()


def _moe_kernel(te_ref, nt_ref, src_ref, x_hbm, wg_ref, wu_ref, wd_ref, o_ref, xbuf, sem):
    t = pl.program_id(0)
    n_live = nt_ref[0]
    tm = xbuf.shape[1]

    def start(tile, slot):
        _row_gather(x_hbm, xbuf.at[slot], sem.at[slot], lambda r: src_ref[tile * tm + r], tm)

    @pl.when(t == 0)
    def _():
        start(0, 0)

    @pl.when(t + 1 < n_live)
    def _():
        start(t + 1, (t + 1) % 2)

    @pl.when(t < n_live)
    def _():
        slot = t % 2
        _row_gather_wait(x_hbm, xbuf.at[slot], sem.at[slot])
        x = xbuf[slot].astype(bf16)
        h = jax.nn.silu(_dot(x, wg_ref[0])) * _dot(x, wu_ref[0])
        o_ref[...] = _dot(h.astype(bf16), wd_ref[0])

    @pl.when(t >= n_live)
    def _():
        o_ref[...] = jnp.zeros_like(o_ref)


def moe_experts(x, src, tile_expert, n_live, w_gate, w_up, w_down, e_base, *, tm):
    p = src.shape[0]
    d = x.shape[1]
    fdim = w_gate.shape[2]
    grid_spec = pltpu.PrefetchScalarGridSpec(
        num_scalar_prefetch=3,
        grid=(p // tm,),
        in_specs=[
            pl.BlockSpec(memory_space=pl.ANY),
            pl.BlockSpec((1, d, fdim), lambda i, te, nt, sr: (e_base + te[i], 0, 0)),
            pl.BlockSpec((1, d, fdim), lambda i, te, nt, sr: (e_base + te[i], 0, 0)),
            pl.BlockSpec((1, fdim, d), lambda i, te, nt, sr: (e_base + te[i], 0, 0)),
        ],
        out_specs=pl.BlockSpec((tm, d), lambda i, te, nt, sr: (i, 0)),
        scratch_shapes=[pltpu.VMEM((2, tm, d), f32), pltpu.SemaphoreType.DMA((2,))],
    )
    return pl.pallas_call(
        _moe_kernel,
        grid_spec=grid_spec,
        out_shape=jax.ShapeDtypeStruct((p, d), f32),
        compiler_params=_params("arbitrary"),
        name="moe_experts",
    )(tile_expert, n_live, src, x, w_gate, w_up, w_down)


def moe_layer(x, router_w, router_b, w_gate, w_up, w_down, e_base, *, tm):
    n = x.shape[0]
    info, counts = moe_router(x, router_w, router_b, tm=512)
    e1, e2 = info[:, 0].astype(jnp.int32), info[:, 1].astype(jnp.int32)
    r1, r2 = info[:, 4].astype(jnp.int32), info[:, 5].astype(jnp.int32)
    cnt = counts[0].astype(jnp.int32)
    padded = (cnt + tm - 1) // tm * tm
    ends = jnp.cumsum(padded)
    starts = ends - padded
    d1, d2 = starts[e1] + r1, starts[e2] + r2
    n_tiles = -(-2 * n // tm) + N_EXPERTS
    p = n_tiles * tm
    tok = jnp.arange(n, dtype=jnp.int32)
    src = jnp.zeros((p,), jnp.int32).at[jnp.concatenate([d1, d2])].set(jnp.concatenate([tok, tok]))
    n_live = ends[-1:] // tm
    tile_ix = jnp.arange(n_tiles, dtype=jnp.int32)
    tile_start = jnp.minimum(tile_ix, n_live[0] - 1) * tm
    tile_expert = jnp.sum((ends[None, :] <= tile_start[:, None]).astype(jnp.int32), axis=1)
    ys = moe_experts(x, src, tile_expert, n_live.astype(jnp.int32), w_gate, w_up, w_down, e_base, tm=tm)
    return ys, d1, d2, info


def _ln_ple_kernel(d1_ref, d2_ref, x_ref, info_ref, ys_hbm, g_ref, b_ref, p_ref, wg_ref, wp_ref, o_ref, ybuf, sem):
    i = pl.program_id(0)
    tm = x_ref.shape[0]

    def start(tile, slot):
        _row_gather(ys_hbm, ybuf.at[slot, 0], sem.at[slot], lambda r: d1_ref[tile * tm + r], tm)
        _row_gather(ys_hbm, ybuf.at[slot, 1], sem.at[slot], lambda r: d2_ref[tile * tm + r], tm)

    @pl.when(i == 0)
    def _():
        start(0, 0)

    @pl.when(i + 1 < pl.num_programs(0))
    def _():
        start(i + 1, (i + 1) % 2)

    slot = i % 2
    _row_gather_wait(ys_hbm, ybuf.at[slot, 0], sem.at[slot])
    _row_gather_wait(ys_hbm, ybuf.at[slot, 1], sem.at[slot])
    info = info_ref[...]
    y = info[:, 2:3] * ybuf[slot, 0] + info[:, 3:4] * ybuf[slot, 1]
    x2 = _layer_norm(DN_ALPHA * x_ref[...] + y, g_ref[...], b_ref[...])
    gate = jax.nn.sigmoid(_dot(x2.astype(bf16), wg_ref[...]))
    o_ref[...] = x2 + gate * _dot(p_ref[...].astype(bf16), wp_ref[...])


def ln_ple(x, ys, d1, d2, info, g, b, p, w_gate, w_proj, *, tm):
    n, d = x.shape
    tm = min(tm, n)
    assert n % tm == 0
    pd = p.shape[1]
    row = lambda i, a, c: (i, 0)
    fixed = lambda i, a, c: (0, 0)
    grid_spec = pltpu.PrefetchScalarGridSpec(
        num_scalar_prefetch=2,
        grid=(n // tm,),
        in_specs=[
            pl.BlockSpec((tm, d), row), pl.BlockSpec((tm, info.shape[1]), row), pl.BlockSpec(memory_space=pl.ANY),
            pl.BlockSpec((1, d), fixed), pl.BlockSpec((1, d), fixed),
            pl.BlockSpec((tm, pd), row), pl.BlockSpec((d, d), fixed), pl.BlockSpec((pd, d), fixed),
        ],
        out_specs=pl.BlockSpec((tm, d), row),
        scratch_shapes=[pltpu.VMEM((2, 2, tm, d), f32), pltpu.SemaphoreType.DMA((2,))],
    )
    return pl.pallas_call(
        _ln_ple_kernel,
        grid_spec=grid_spec,
        out_shape=jax.ShapeDtypeStruct((n, d), f32),
        compiler_params=_params("arbitrary"),
        name="ln_ple",
    )(d1, d2, x, info, ys, g, b, p, w_gate, w_proj)


def _retention_decode_kernel(q_ref, k_ref, v_ref, g_ref, gn_ref, dec_ref, s_ref, o_ref, so_ref):
    q = q_ref[0]
    k = k_ref[0] * (RET_DK ** -0.5)
    v = v_ref[0]
    gamma = dec_ref[0]
    s_prev = s_ref[0, 0]
    att = jnp.sum(q.astype(f32) * k.astype(f32), axis=-1, keepdims=True)
    rows = 8
    q8 = jnp.broadcast_to(q, (rows, RET_DK))
    first = (lax.broadcasted_iota(jnp.int32, (rows, 1), 0) == 0).astype(f32)
    k8 = (jnp.broadcast_to(k.astype(f32), (rows, RET_DK)) * first).astype(bf16)
    v8 = jnp.broadcast_to(v, (rows, RET_DV))
    o = att.astype(bf16).astype(f32) * v.astype(f32) + _dot(q8, s_prev.astype(bf16))[:1] * gamma
    so_ref[0, 0] = s_prev * gamma + _dot_tn(k8, v8)
    mu = jnp.mean(o, axis=-1, keepdims=True)
    d = o - mu
    var = jnp.mean(d * d, axis=-1, keepdims=True)
    o = d * lax.rsqrt(var + LN_EPS) * gn_ref[...]
    o_ref[0] = (jax.nn.silu(g_ref[0].astype(f32)) * o).astype(o_ref.dtype)


def retention_decode(qk, vg, gn_g, states, base):
    b = qk.shape[0]
    h = RET_HEADS
    _, qdec, _ = _retention_decay_tables(1)
    qk3, vg3 = qk[:, None, :], vg[:, None, :]
    o, s = pl.pallas_call(
        _retention_decode_kernel,
        grid=(b, h),
        in_specs=[
            pl.BlockSpec((1, 1, RET_DK), lambda i, hh: (i, 0, hh)),
            pl.BlockSpec((1, 1, RET_DK), lambda i, hh: (i, 0, h + hh)),
            pl.BlockSpec((1, 1, RET_DV), lambda i, hh: (i, 0, hh)),
            pl.BlockSpec((1, 1, RET_DV), lambda i, hh: (i, 0, h + hh)),
            pl.BlockSpec((1, RET_DV), lambda i, hh: (0, hh)),
            pl.BlockSpec((1, 1, 1), lambda i, hh: (hh, 0, 0)),
            pl.BlockSpec((1, 1, RET_DK, RET_DV), lambda i, hh: (base + i, hh, 0, 0)),
        ],
        out_specs=[
            pl.BlockSpec((1, 1, RET_DV), lambda i, hh: (i, 0, hh)),
            pl.BlockSpec((1, 1, RET_DK, RET_DV), lambda i, hh: (i, hh, 0, 0)),
        ],
        out_shape=[jax.ShapeDtypeStruct((b, 1, h * RET_DV), bf16), jax.ShapeDtypeStruct((b, h, RET_DK, RET_DV), f32)],
        compiler_params=_params("parallel", "parallel"),
        name="retention_decode",
    )(qk3, qk3, vg3, vg3, gn_g, qdec, states)
    return o[:, 0, :], s


HIST_PAGES_PER_STEP = 32


def _compress_hist_kernel(pt_ref, *refs, n_pages):
    page_refs = refs[:n_pages]
    pe_ref, w1_ref, w2_ref, o_ref, a_buf, o_buf = refs[n_pages:]
    g = NSA_KV
    upp = page_refs[0].shape[1] // CMP_STRIDE
    rows = n_pages * upp * g
    pad = a_buf.shape[0] - rows

    def load(l):
        return jnp.concatenate([pr[0, pl.ds(l, upp, stride=CMP_STRIDE), :, :].reshape(upp * g, NSA_HD)
                                for pr in page_refs], axis=0)

    a, b = _compress_units(load, pe_ref[0], w1_ref)

    @pl.when(pl.program_id(2) == 0)
    def _():
        a_buf[0:pad, :] = jnp.zeros((pad, a_buf.shape[1]), f32)

    @pl.when(pl.program_id(2) > 0)
    def _():
        a_buf[0:pad, :] = a_buf[rows:rows + pad, :]

    a_buf[pad:pad + rows, :] = a
    a_prev = a_buf[pad - g:pad - g + rows, :]
    o_buf[...] = _dot(jax.nn.gelu(a_prev + b).astype(bf16), w2_ref[0])
    for gg in range(g):
        o_ref[0, 0, gg] = o_buf[pl.ds(gg, rows // g, stride=g), :].astype(o_ref.dtype)


def compress_history(pool, page_table, pe, w1, w2):
    b, ppb = page_table.shape
    page = pool.shape[1]
    g = NSA_KV
    n_pages = min(HIST_PAGES_PER_STEP, ppb)
    assert ppb % n_pages == 0 and page % CMP_STRIDE == 0
    upp = page // CMP_STRIDE
    seg = n_pages * upp
    nu = ppb * upp
    sublanes = 8
    page_spec = lambda k: pl.BlockSpec(
        (1, page, None, g, NSA_HD), lambda i, s, ch, pt: (pt[i, ch * n_pages + k], 0, s, 0, 0))
    grid_spec = pltpu.PrefetchScalarGridSpec(
        num_scalar_prefetch=1,
        grid=(b, 2, ppb // n_pages),
        in_specs=[page_spec(k) for k in range(n_pages)] + [
            pl.BlockSpec((1, CMP_BLOCK, NSA_HD), lambda i, s, ch, pt: (s, 0, 0)),
            pl.BlockSpec((1, CMP_BLOCK * NSA_HD, CMP_HIDDEN), lambda i, s, ch, pt: (s, 0, 0)),
            pl.BlockSpec((1, CMP_HIDDEN, NSA_HD), lambda i, s, ch, pt: (s, 0, 0)),
        ],
        out_specs=pl.BlockSpec((1, 1, g, seg, NSA_HD), lambda i, s, ch, pt: (i, s, 0, ch, 0)),
        scratch_shapes=[pltpu.VMEM((seg * g + sublanes, CMP_HIDDEN), f32), pltpu.VMEM((seg * g, NSA_HD), f32)],
    )
    return pl.pallas_call(
        functools.partial(_compress_hist_kernel, n_pages=n_pages),
        grid_spec=grid_spec,
        out_shape=jax.ShapeDtypeStruct((b, 2, g, nu, NSA_HD), bf16),
        compiler_params=_params("parallel", "parallel", "arbitrary"),
        name="compress_history",
    )(page_table, *([pool] * n_pages), pe, w1, w2)


def _nsa_select_decode_kernel(q_ref, kc_ref, vc_ref, msel_ref, oc_ref, top_ref, *, t, ns, n_top):
    r_heads, hd = NSA_HPG, NSA_HD
    q = q_ref[0]
    qb = jnp.concatenate([q[:, r * hd:(r + 1) * hd] for r in range(r_heads)], axis=0)
    kc = kc_ref[0, 0, 0]
    nu = kc.shape[0]
    s = _dot_nt(qb, kc) * NSA_SCALE
    u = lax.broadcasted_iota(jnp.int32, (1, nu), 1)
    p = _masked_softmax(s, (u >= 1) & (u * CMP_STRIDE + (CMP_STRIDE - 1) <= t))
    oc_ref[0, 0] = _dot(p.astype(bf16), vc_ref[0, 0, 0])
    imp = jnp.sum(p, axis=0, keepdims=True)
    sel = jnp.dot(imp, msel_ref[...], precision=lax.Precision.HIGHEST, preferred_element_type=f32)
    jidx = lax.broadcasted_iota(jnp.int32, sel.shape, 1)
    cur = t // SEL_BLOCK
    forced = (jidx == 0) | (jidx == cur) | (jidx == cur - 1)
    sel = jnp.where(jidx * SEL_BLOCK <= t, jnp.where(forced, FORCE_SCORE, sel), NEG)
    sel = jnp.where(jidx < ns, sel, -jnp.inf)
    width = sel.shape[-1]
    lane = lax.broadcasted_iota(jnp.int32, (1, top_ref.shape[-1]), 1)
    top = jnp.zeros(lane.shape, jnp.int32)
    for it in range(n_top):
        mx = jnp.max(sel, axis=-1, keepdims=True)
        idx = jnp.min(jnp.where(sel == mx, jidx, width), axis=-1, keepdims=True)
        top = jnp.where(lane == it, idx, top)
        sel = jnp.where(jidx == idx, -jnp.inf, sel)
    top_ref[0, 0] = top


def nsa_select_decode(q, cmp, t):
    b = q.shape[0]
    g = NSA_KV
    nu = cmp.shape[3]
    ns = (t + 1 + SEL_BLOCK - 1) // SEL_BLOCK
    ns_pad = -(-ns // LANES) * LANES
    per_sel = SEL_BLOCK // CMP_STRIDE
    u = jnp.arange(nu)[:, None]
    j = jnp.arange(ns_pad)[None, :]
    rb = CMP_BLOCK // CMP_STRIDE
    msel = sum(((u - 1 + r) // per_sel == j).astype(f32) for r in range(rb)) / rb
    msel = jnp.where(u >= 1, msel, 0.0)
    gw = NSA_HPG * NSA_HD
    n_top = min(N_SEL, ns)
    cmp_spec = lambda s: pl.BlockSpec((1, 1, 1, nu, NSA_HD), lambda i, gg: (i, s, gg, 0, 0))
    return pl.pallas_call(
        functools.partial(_nsa_select_decode_kernel, t=t, ns=ns, n_top=n_top),
        grid=(b, g),
        in_specs=[
            pl.BlockSpec((1, 1, gw), lambda i, gg: (i, 0, gg)),
            cmp_spec(0), cmp_spec(1),
            pl.BlockSpec((nu, ns_pad), lambda i, gg: (0, 0)),
        ],
        out_specs=[
            pl.BlockSpec((1, 1, NSA_HPG, NSA_HD), lambda i, gg: (i, gg, 0, 0)),
            pl.BlockSpec((1, 1, 1, LANES), lambda i, gg: (i, gg, 0, 0)),
        ],
        out_shape=[jax.ShapeDtypeStruct((b, g, NSA_HPG, NSA_HD), f32), jax.ShapeDtypeStruct((b, g, 1, LANES), jnp.int32)],
        compiler_params=_params("parallel", "parallel"),
        name="nsa_select_decode",
    )(q[:, None, :], cmp, cmp, msel)


def _nsa_attn_decode_kernel(pt_ref, top_ref, *refs, n_top, n_hist_blocks):
    k_refs, v_refs = refs[:n_top], refs[n_top:2 * n_top]
    (q_ref, oc_ref, knew_ref, vnew_ref, kw_ref, vw_ref, kwnew_ref, vwnew_ref, gl_ref, o_ref) = refs[2 * n_top:]
    i, gg = pl.program_id(0), pl.program_id(1)
    r_heads, hd = NSA_HPG, NSA_HD
    q = q_ref[0]
    qb = jnp.concatenate([q[:, r * hd:(r + 1) * hd] for r in range(r_heads)], axis=0)

    def attend(keys, vals, mask, k_new, v_new):
        s = jnp.where(mask, _dot_nt(qb, keys) * NSA_SCALE, NEG)
        s_new = jnp.sum(qb.astype(f32) * k_new.astype(f32), axis=-1, keepdims=True) * NSA_SCALE
        m = jnp.maximum(jnp.max(s, axis=-1, keepdims=True), s_new)
        e = jnp.where(mask, jnp.exp(s - m), 0.0)
        e_new = jnp.exp(s_new - m)
        num = _dot(e.astype(bf16), vals) + e_new.astype(bf16).astype(f32) * v_new.astype(f32)
        return num / (jnp.sum(e, axis=-1, keepdims=True) + e_new)

    g = NSA_KV

    def rows_of(ref):
        v = ref[0]
        return v.reshape(v.shape[0] * g, hd).astype(bf16)

    def own_group(n_rows):
        return lax.broadcasted_iota(jnp.int32, (1, n_rows * g), 1) % g == gg

    sb = k_refs[0].shape[1]
    keys = jnp.concatenate([rows_of(r) for r in k_refs], axis=0)
    vals = jnp.concatenate([rows_of(r) for r in v_refs], axis=0)
    blk_of_lane = lax.broadcasted_iota(jnp.int32, (1, n_top * sb * g), 1) // (sb * g)
    sel_of_lane = jnp.zeros((1, n_top * sb * g), jnp.int32)
    for n in range(n_top):
        sel_of_lane = jnp.where(blk_of_lane == n, top_ref[i, gg, n], sel_of_lane)
    o_s = attend(keys, vals, (sel_of_lane < n_hist_blocks) & own_group(n_top * sb), knew_ref[0], vnew_ref[0])

    wlen = kw_ref.shape[1]
    wmask = (lax.broadcasted_iota(jnp.int32, (1, wlen * g), 1) >= g) & own_group(wlen)
    o_w = attend(rows_of(kw_ref), rows_of(vw_ref), wmask, kwnew_ref[0], vwnew_ref[0])

    gates = jax.nn.sigmoid(gl_ref[0])
    o_c = oc_ref[0, 0]
    outs = []
    for r in range(r_heads):
        outs.append(gates[:, 3 * r:3 * r + 1] * o_c[r:r + 1] + gates[:, 3 * r + 1:3 * r + 2] * o_s[r:r + 1]
                    + gates[:, 3 * r + 2:3 * r + 3] * o_w[r:r + 1])
    o_ref[0] = jnp.concatenate(outs, axis=1).astype(o_ref.dtype)


def nsa_attn_decode(q, o_c, top, kvb, gl, pool, page_table, win, win_base, t):
    b = q.shape[0]
    g = NSA_KV
    n_top = top.shape[-1]
    page = pool.shape[1]
    assert t % SEL_BLOCK == 0 and page % SEL_BLOCK == 0 and win.shape[1] == WINDOW
    n_hist_blocks = t // SEL_BLOCK
    bpp = page // SEL_BLOCK
    gw = NSA_HPG * NSA_HD

    def blk_spec(n, slot):
        def imap(i, gg, pt, tp):
            j = jnp.minimum(tp[i, gg, n], n_hist_blocks - 1)
            return (pt[i, j // bpp], j % bpp, slot, 0, 0)
        return pl.BlockSpec((1, SEL_BLOCK, None, g, NSA_HD), imap)

    new_spec = lambda slot: pl.BlockSpec((1, 1, NSA_HD), lambda i, gg, pt, tp: (i, 0, slot * g + gg))
    win_spec = lambda slot: pl.BlockSpec((1, WINDOW, None, g, NSA_HD),
                                         lambda i, gg, pt, tp: (win_base + i, 0, slot, 0, 0))
    kvb3 = kvb[:, None, :]
    grid_spec = pltpu.PrefetchScalarGridSpec(
        num_scalar_prefetch=2,
        grid=(b, g),
        in_specs=[blk_spec(n, 2) for n in range(n_top)] + [blk_spec(n, 3) for n in range(n_top)] + [
            pl.BlockSpec((1, 1, gw), lambda i, gg, pt, tp: (i, 0, gg)),
            pl.BlockSpec((1, 1, NSA_HPG, NSA_HD), lambda i, gg, pt, tp: (i, gg, 0, 0)),
            new_spec(2), new_spec(3), win_spec(0), win_spec(1), new_spec(4), new_spec(5),
            pl.BlockSpec((1, 1, LANES), lambda i, gg, pt, tp: (i, 0, gg)),
        ],
        out_specs=pl.BlockSpec((1, 1, gw), lambda i, gg, pt, tp: (i, 0, gg)),
    )
    o = pl.pallas_call(
        functools.partial(_nsa_attn_decode_kernel, n_top=n_top, n_hist_blocks=n_hist_blocks),
        grid_spec=grid_spec,
        out_shape=jax.ShapeDtypeStruct((b, 1, NSA_HEADS * NSA_HD), bf16),
        compiler_params=_params("parallel", "parallel"),
        name="nsa_attn_decode",
    )(page_table, top, *([pool] * (2 * n_top)), q[:, None, :], o_c, kvb3, kvb3, win, win, kvb3, kvb3,
      gl[:, None, :])
    return o[:, 0, :]


def _retention_mixer(x, pos, batch, w, states, base):
    cos, sin = _rope_tables(pos, RET_DK)
    tn = PROJ_TILE
    n_qk = 2 * RET_HEADS * RET_DK
    flags = jnp.ones((n_qk // tn,), jnp.int32)
    qk, = mm_rope(x, w["in"], (0, n_qk), cos, sin, flags, [bf16], hd=RET_DK, tm=PROJ_TILE, tn=tn)
    vg = mm(x, w["in"], (n_qk, 2 * RET_HEADS * RET_DV), bf16, tm=PROJ_TILE, tn=PROJ_TILE)
    if states is None:
        return retention_prompt(qk, vg, w["gn"], batch)
    return retention_decode(qk, vg, w["gn"], states, base)


def _nsa_projections(x, pos, w):
    cos, sin = _rope_tables(pos, NSA_HD)
    tn = NSA_KV * NSA_HD
    n_q = NSA_HEADS * NSA_HD
    n_kv = 6 * NSA_KV * NSA_HD
    q, = mm_rope(x, w["in"], (0, n_q), cos, sin, jnp.ones((n_q // tn,), jnp.int32), [bf16],
                 hd=NSA_HD, tm=PROJ_TILE, tn=tn)
    kvf, kvb = mm_rope(x, w["in"], (n_q, n_kv), cos, sin, jnp.array([1, 0] * 3, jnp.int32), [f32, bf16],
                       hd=NSA_HD, tm=PROJ_TILE, tn=tn)
    gl = mm(x, w["gl"], (0, w["gl"].shape[1]), f32, tm=PROJ_TILE, tn=tn)
    return q, kvf, kvb, gl


def _layer_tail(x, h, p, w, tm_moe):
    x1 = mm_ln(h, w["out"], x, w["ln_g"][0:1], w["ln_b"][0:1], tm=512, tk=2048)
    ys, d1, d2, info = moe_layer(x1, w["router_w"], w["router_b"], w["moe_gate"], w["moe_up"], w["moe_down"],
                                 w["moe_base"], tm=tm_moe)
    return ln_ple(x1, ys, d1, d2, info, w["ln_g"][1:2], w["ln_b"][1:2], p, w["ple_gate"], w["ple_proj"], tm=256)


def kernel(x_prompt, x_sample, state_ret, cache_nsa_kv, state_nsa_win, page_table, p_prompt, p_sample, ret_w_in, ret_w_out, ret_gn_g, nsa_w_in, nsa_w_out, nsa_cmp_pos, nsa_cmp_w1, nsa_cmp_w2, ln_g, ln_b, router_w, router_b, moe_w_gate, moe_w_up, moe_w_down, ple_w_gate, ple_w_proj):
    bp, tp, d = x_prompt.shape
    bs, ts, _ = x_sample.shape
    assert ts == 1
    n_pool, page = cache_nsa_kv.shape[1], cache_nsa_kv.shape[2]
    past = page_table.shape[1] * page
    g, hd = NSA_KV, NSA_HD
    kv_cols = N_KV_SLOTS * g * hd

    xp = x_prompt.reshape(bp * tp, d)
    xs = x_sample.reshape(bs * ts, d)
    pos_p = jnp.arange(tp, dtype=jnp.int32)
    pos_s = jnp.full((bs,), past, jnp.int32)
    states = state_ret.reshape((-1,) + state_ret.shape[2:])
    pool = cache_nsa_kv.reshape((-1,) + cache_nsa_kv.shape[2:])
    wins = state_nsa_win.reshape((-1,) + state_nsa_win.shape[2:])

    moe_gate = moe_w_gate.astype(bf16).reshape((-1,) + moe_w_gate.shape[2:])
    moe_up = moe_w_up.astype(bf16).reshape((-1,) + moe_w_up.shape[2:])
    moe_down = moe_w_down.astype(bf16).reshape((-1,) + moe_w_down.shape[2:])

    ret_p, ret_s, kv_p, kv_s, win_p, win_s = [], [], [], [], [], []
    for i in range(DEPTH):
        j = i // 2
        w = {
            "ln_g": ln_g[i], "ln_b": ln_b[i],
            "router_w": router_w, "router_b": router_b[None, :],
            "moe_gate": moe_gate, "moe_up": moe_up, "moe_down": moe_down, "moe_base": i * N_EXPERTS,
            "ple_gate": ple_w_gate[i].astype(bf16), "ple_proj": ple_w_proj[i].astype(bf16),
        }
        if i % 2 == 0:
            w.update({"in": ret_w_in[j].astype(bf16), "out": ret_w_out[j].astype(bf16), "gn": ret_gn_g[j][None, :]})
            hp, sp = _retention_mixer(xp, pos_p, bp, w, None, 0)
            hs, ss = _retention_mixer(xs, pos_s, bs, w, states, j * bs)
            ret_p.append(sp)
            ret_s.append(ss)
        else:
            nq = NSA_HEADS * hd
            w_in = nsa_w_in[j]
            gl = w_in[:, nq + 6 * g * hd:].reshape(d, g, NSA_HPG * 3)
            gl = jnp.pad(gl, ((0, 0), (0, 0), (0, LANES - NSA_HPG * 3))).reshape(d, g * LANES)
            w.update({"in": w_in.astype(bf16), "gl": gl.astype(bf16), "out": nsa_w_out[j].astype(bf16)})
            pe = nsa_cmp_pos[j]
            w1 = nsa_cmp_w1[j].reshape(2, CMP_BLOCK * hd, CMP_HIDDEN).astype(bf16)
            w2 = nsa_cmp_w2[j].astype(bf16)
            q, kvf, kvb, glp = _nsa_projections(xp, pos_p, w)
            cmp = compress_prompt(kvf, pe, w1, w2, bp)
            hp = nsa_attn_prompt(q, cmp, kvb, glp, bp)
            kv_p.append(kvf[:, :kv_cols].reshape(bp, tp, N_KV_SLOTS, g, hd))
            keep = min(WINDOW, tp)
            win_p.append(kvf.reshape(bp, tp, -1)[:, tp - keep:, kv_cols:].reshape(bp, keep, 2, g, hd))
            q, kvf, kvb, gls = _nsa_projections(xs, pos_s, w)
            pt = page_table + j * n_pool
            cmp = compress_history(pool, pt, pe, w1, w2)
            o_c, top = nsa_select_decode(q, cmp, past)
            top = top[:, :, 0, :min(N_SEL, past // SEL_BLOCK + 1)]
            hs = nsa_attn_decode(q, o_c, top, kvb, gls, pool, pt, wins, j * bs, past)
            kv_s.append(kvf[:, :kv_cols].reshape(bs, ts, N_KV_SLOTS, g, hd))
            new_win = kvf[:, kv_cols:].reshape(bs, ts, 2, g, hd)
            win_s.append(jnp.concatenate([state_nsa_win[j][:, ts:], new_win], axis=1))
        xp = _layer_tail(xp, hp, p_prompt[i].reshape(bp * tp, -1), w, MOE_TILE)
        xs = _layer_tail(xs, hs, p_sample[i].reshape(bs * ts, -1), w, 16)

    return (xp.reshape(bp, tp, d), xs.reshape(bs, ts, d), jnp.stack(ret_p), jnp.stack(ret_s),
            jnp.stack(kv_p), jnp.stack(kv_s), jnp.stack(win_p), jnp.stack(win_s))
```

```python
import functools
import math

import jax
import jax.numpy as jnp
from jax import lax
from jax.experimental import pallas as pl
from jax.experimental.pallas import tpu as pltpu

f32 = jnp.float32
bf16 = jnp.bfloat16

D_MODEL = 2048
DEPTH = 4
RET_HEADS = 8
RET_DK = D_MODEL // RET_HEADS
RET_DV = 2 * RET_DK
RET_CHUNK = 128
NSA_HEADS = 16
NSA_HD = D_MODEL // NSA_HEADS
NSA_KV = 4
NSA_HPG = NSA_HEADS // NSA_KV
CMP_BLOCK = 32
CMP_STRIDE = 16
CMP_HIDDEN = 4 * NSA_HD
SEL_BLOCK = 64
N_SEL = 16
WINDOW = 512
Q_BLOCK = 128
N_KV_SLOTS = 4
N_EXPERTS = 16
N_GROUPS = 4
EXPERTS_PER_GROUP = N_EXPERTS // N_GROUPS
D_EXPERT = 1408
PLE_DIM = 256
ROPE_THETA = 10000.0
LN_EPS = 1e-5
DN_ALPHA = (2 * DEPTH) ** 0.25
NEG = -1e30
FORCE_SCORE = float(NSA_HPG + 1)
NSA_SCALE = NSA_HD ** -0.5

V7X_VMEM_BYTES = 64 * 1024 * 1024
VMEM_LIMIT = V7X_VMEM_BYTES - 8 * 1024 * 1024
LANES = 128
MOE_TILE = 256
PROJ_TILE = 1024
RET_HEADS_PER_STEP = 2


def _params(*sem):
    return pltpu.CompilerParams(dimension_semantics=sem, vmem_limit_bytes=VMEM_LIMIT)


def _layer_norm(v, g, b):
    mu = jnp.mean(v, axis=-1, keepdims=True)
    d = v - mu
    var = jnp.mean(d * d, axis=-1, keepdims=True)
    return d * lax.rsqrt(var + LN_EPS) * g + b


def _dot(a, b):
    return jnp.dot(a, b, preferred_element_type=f32)


def _dot_nt(a, b):
    return lax.dot_general(a, b, (((1,), (1,)), ((), ())), preferred_element_type=f32)


def _dot_tn(a, b):
    return lax.dot_general(a, b, (((0,), (0,)), ((), ())), preferred_element_type=f32)


def _rotate_half(v, hd):
    pieces = []
    for c in range(0, v.shape[1], hd):
        if hd == 2 * LANES:
            pieces += [v[:, c + LANES:c + hd], v[:, c:c + LANES]]
        else:
            pieces.append(pltpu.roll(v[:, c:c + hd], hd // 2, 1))
    return jnp.concatenate(pieces, axis=1)


def _mm_rope_kernel(flags_ref, x_ref, w_ref, cos_ref, sin_ref, *o_refs, hd):
    acc = _dot(x_ref[...].astype(bf16), w_ref[...])
    roped = flags_ref[pl.program_id(1)] == 1

    @pl.when(roped)
    def _():
        reps = acc.shape[1] // hd
        cos = jnp.concatenate([cos_ref[...]] * reps, axis=1)
        sin = jnp.concatenate([sin_ref[...]] * reps, axis=1)
        v = acc * cos + _rotate_half(acc, hd) * sin
        for o in o_refs:
            o[...] = v.astype(o.dtype)

    @pl.when(jnp.logical_not(roped))
    def _():
        for o in o_refs:
            o[...] = acc.astype(o.dtype)


def mm_rope(x, w, cols, cos, sin, flags, out_dtypes, *, hd, tm, tn):
    n, k = x.shape
    c0, m = cols
    t = cos.shape[0]
    tm = min(tm, n)
    assert n % tm == 0 and m % tn == 0 and c0 % tn == 0 and t % tm == 0 and tn % hd == 0
    tper = t // tm
    j0 = c0 // tn
    grid_spec = pltpu.PrefetchScalarGridSpec(
        num_scalar_prefetch=1,
        grid=(n // tm, m // tn),
        in_specs=[
            pl.BlockSpec((tm, k), lambda i, j, f: (i, 0)),
            pl.BlockSpec((k, tn), lambda i, j, f: (0, j0 + j)),
            pl.BlockSpec((tm, hd), lambda i, j, f: (i % tper, 0)),
            pl.BlockSpec((tm, hd), lambda i, j, f: (i % tper, 0)),
        ],
        out_specs=[pl.BlockSpec((tm, tn), lambda i, j, f: (i, j)) for _ in out_dtypes],
    )
    return pl.pallas_call(
        functools.partial(_mm_rope_kernel, hd=hd),
        grid_spec=grid_spec,
        out_shape=[jax.ShapeDtypeStruct((n, m), d) for d in out_dtypes],
        compiler_params=_params("parallel", "arbitrary"),
        name="mm_rope",
    )(flags, x, w, cos, sin)


def _mm_kernel(x_ref, w_ref, o_ref):
    o_ref[...] = _dot(x_ref[...].astype(bf16), w_ref[...]).astype(o_ref.dtype)


def mm(x, w, cols, out_dtype, *, tm, tn):
    n, k = x.shape
    c0, m = cols
    tm = min(tm, n)
    tn = min(tn, m)
    assert n % tm == 0 and m % tn == 0 and c0 % tn == 0
    j0 = c0 // tn
    return pl.pallas_call(
        _mm_kernel,
        grid=(n // tm, m // tn),
        in_specs=[pl.BlockSpec((tm, k), lambda i, j: (i, 0)), pl.BlockSpec((k, tn), lambda i, j: (0, j0 + j))],
        out_specs=pl.BlockSpec((tm, tn), lambda i, j: (i, j)),
        out_shape=jax.ShapeDtypeStruct((n, m), out_dtype),
        compiler_params=_params("parallel", "arbitrary"),
        name="mm",
    )(x, w)


def _mm_ln_kernel(a_ref, w_ref, x_ref, g_ref, b_ref, o_ref, acc_ref):
    kk = pl.program_id(1)

    @pl.when(kk == 0)
    def _():
        acc_ref[...] = jnp.zeros_like(acc_ref)

    acc_ref[...] += _dot(a_ref[...].astype(bf16), w_ref[...])

    @pl.when(kk == pl.num_programs(1) - 1)
    def _():
        o_ref[...] = _layer_norm(DN_ALPHA * x_ref[...] + acc_ref[...], g_ref[...], b_ref[...])


def _mm_ln_single_kernel(a_ref, w_ref, x_ref, g_ref, b_ref, o_ref):
    acc = _dot(a_ref[...].astype(bf16), w_ref[...])
    o_ref[...] = _layer_norm(DN_ALPHA * x_ref[...] + acc, g_ref[...], b_ref[...])


def mm_ln(a, w, x, g, b, *, tm, tk):
    n, k = a.shape
    d = w.shape[1]
    tm = min(tm, n)
    tk = min(tk, k)
    assert n % tm == 0 and k % tk == 0
    if tk == k:
        return pl.pallas_call(
            _mm_ln_single_kernel,
            grid=(n // tm,),
            in_specs=[
                pl.BlockSpec((tm, k), lambda i: (i, 0)),
                pl.BlockSpec((k, d), lambda i: (0, 0)),
                pl.BlockSpec((tm, d), lambda i: (i, 0)),
                pl.BlockSpec((1, d), lambda i: (0, 0)),
                pl.BlockSpec((1, d), lambda i: (0, 0)),
            ],
            out_specs=pl.BlockSpec((tm, d), lambda i: (i, 0)),
            out_shape=jax.ShapeDtypeStruct((n, d), f32),
            compiler_params=_params("parallel"),
            name="mm_ln",
        )(a, w, x, g, b)
    return pl.pallas_call(
        _mm_ln_kernel,
        grid=(n // tm, k // tk),
        in_specs=[
            pl.BlockSpec((tm, tk), lambda i, kk: (i, kk)),
            pl.BlockSpec((tk, d), lambda i, kk: (kk, 0)),
            pl.BlockSpec((tm, d), lambda i, kk: (i, 0)),
            pl.BlockSpec((1, d), lambda i, kk: (0, 0)),
            pl.BlockSpec((1, d), lambda i, kk: (0, 0)),
        ],
        out_specs=pl.BlockSpec((tm, d), lambda i, kk: (i, 0)),
        out_shape=jax.ShapeDtypeStruct((n, d), f32),
        scratch_shapes=[pltpu.VMEM((tm, d), f32)],
        compiler_params=_params("parallel", "arbitrary"),
        name="mm_ln",
    )(a, w, x, g, b)


def _retention_kernel(q_ref, k_ref, v_ref, g_ref, gn_ref, intra_ref, qdec_ref, kdec_ref, o_ref, s_ref, state):
    c = pl.program_id(2)

    @pl.when(c == 0)
    def _():
        state[...] = jnp.zeros_like(state)

    for hh in range(state.shape[0]):
        kcols = slice(hh * RET_DK, (hh + 1) * RET_DK)
        vcols = slice(hh * RET_DV, (hh + 1) * RET_DV)
        q = q_ref[:, kcols]
        k = k_ref[:, kcols] * (RET_DK ** -0.5)
        v = v_ref[:, vcols]
        qdec = qdec_ref[hh]
        kdec = kdec_ref[hh]
        cdec = qdec[-1:, :]
        s_prev = state[hh]
        att = _dot_nt(q, k) * intra_ref[hh]
        o = _dot(att.astype(bf16), v) + _dot(q, s_prev.astype(bf16)) * qdec
        kd = (k.astype(f32) * kdec).astype(bf16)
        s_new = s_prev * cdec + _dot_tn(kd, v)
        state[hh] = s_new

        mu = jnp.mean(o, axis=-1, keepdims=True)
        d = o - mu
        var = jnp.mean(d * d, axis=-1, keepdims=True)
        o = d * lax.rsqrt(var + LN_EPS) * gn_ref[:, vcols]
        o_ref[:, vcols] = (jax.nn.silu(g_ref[:, vcols].astype(f32)) * o).astype(o_ref.dtype)

    @pl.when(c == pl.num_programs(2) - 1)
    def _():
        s_ref[0] = state[...]


def _retention_decay_tables(chunk):
    h = jnp.arange(RET_HEADS, dtype=f32)
    log_gamma = jnp.log1p(-jnp.exp2(-5.0 - h))
    idx = jnp.arange(chunk, dtype=f32)
    rel = idx[:, None] - idx[None, :]
    intra = jnp.where(rel >= 0, jnp.exp(jnp.maximum(rel, 0.0)[None] * log_gamma[:, None, None]), 0.0)
    qdec = jnp.exp((idx[None, :] + 1.0) * log_gamma[:, None])[..., None]
    kdec = jnp.exp((chunk - 1.0 - idx[None, :]) * log_gamma[:, None])[..., None]
    return intra, qdec, kdec


def retention_prompt(qk, vg, gn_g, batch):
    n = qk.shape[0]
    t = n // batch
    chunk = math.gcd(t, RET_CHUNK)
    nch = t // chunk
    h = RET_HEADS
    hps = RET_HEADS_PER_STEP
    hb = h // hps
    intra, qdec, kdec = _retention_decay_tables(chunk)
    row = lambda b, hh, c: b * nch + c
    return pl.pallas_call(
        _retention_kernel,
        grid=(batch, hb, nch),
        in_specs=[
            pl.BlockSpec((chunk, hps * RET_DK), lambda b, hh, c: (row(b, hh, c), hh)),
            pl.BlockSpec((chunk, hps * RET_DK), lambda b, hh, c: (row(b, hh, c), hb + hh)),
            pl.BlockSpec((chunk, hps * RET_DV), lambda b, hh, c: (row(b, hh, c), hh)),
            pl.BlockSpec((chunk, hps * RET_DV), lambda b, hh, c: (row(b, hh, c), hb + hh)),
            pl.BlockSpec((1, hps * RET_DV), lambda b, hh, c: (0, hh)),
            pl.BlockSpec((hps, chunk, chunk), lambda b, hh, c: (hh, 0, 0)),
            pl.BlockSpec((hps, chunk, 1), lambda b, hh, c: (hh, 0, 0)),
            pl.BlockSpec((hps, chunk, 1), lambda b, hh, c: (hh, 0, 0)),
        ],
        out_specs=[
            pl.BlockSpec((chunk, hps * RET_DV), lambda b, hh, c: (row(b, hh, c), hh)),
            pl.BlockSpec((1, hps, RET_DK, RET_DV), lambda b, hh, c: (b, hh, 0, 0)),
        ],
        out_shape=[
            jax.ShapeDtypeStruct((n, h * RET_DV), bf16),
            jax.ShapeDtypeStruct((batch, h, RET_DK, RET_DV), f32),
        ],
        scratch_shapes=[pltpu.VMEM((hps, RET_DK, RET_DV), f32)],
        compiler_params=_params("parallel", "parallel", "arbitrary"),
        name="retention_prompt",
    )(qk, qk, vg, vg, gn_g, intra, qdec, kdec)


def _rope_tables(pos, hd):
    half = hd // 2
    inv = ROPE_THETA ** (-jnp.arange(half, dtype=f32) / half)
    ang = pos.astype(f32)[:, None] * inv[None, :]
    cos, sin = jnp.cos(ang), jnp.sin(ang)
    return jnp.concatenate([cos, cos], axis=1), jnp.concatenate([-sin, sin], axis=1)


def _compress_units(load_unit_row, pe, w1_ref):
    xa, xb = [], []
    for l in range(CMP_STRIDE):
        xl = load_unit_row(l)
        xa.append((xl + pe[l:l + 1]).astype(bf16))
        xb.append((xl + pe[CMP_STRIDE + l:CMP_STRIDE + l + 1]).astype(bf16))
    half = CMP_STRIDE * NSA_HD
    a = _dot(jnp.concatenate(xa, axis=1), w1_ref[0, :half])
    b = _dot(jnp.concatenate(xb, axis=1), w1_ref[0, half:])
    return a, b


def _compress_kernel(kv_ref, pe_ref, w1_ref, w2_ref, o_ref):
    nu = kv_ref.shape[0] // CMP_STRIDE
    a, b = _compress_units(lambda l: kv_ref[pl.ds(l, nu, stride=CMP_STRIDE), :], pe_ref[0], w1_ref)
    hid = a + pltpu.roll(b, nu - 1, 0)
    o_ref[0, 0, 0] = _dot(jax.nn.gelu(hid).astype(bf16), w2_ref[0]).astype(o_ref.dtype)


def compress_prompt(kvf, pe, w1, w2, batch):
    n = kvf.shape[0]
    t = n // batch
    nu = t // CMP_STRIDE
    g = NSA_KV
    return pl.pallas_call(
        _compress_kernel,
        grid=(batch, 2, g),
        in_specs=[
            pl.BlockSpec((t, NSA_HD), lambda b, s, gg: (b, s * g + gg)),
            pl.BlockSpec((1, CMP_BLOCK, NSA_HD), lambda b, s, gg: (s, 0, 0)),
            pl.BlockSpec((1, CMP_BLOCK * NSA_HD, CMP_HIDDEN), lambda b, s, gg: (s, 0, 0)),
            pl.BlockSpec((1, CMP_HIDDEN, NSA_HD), lambda b, s, gg: (s, 0, 0)),
        ],
        out_specs=pl.BlockSpec((1, 1, 1, nu, NSA_HD), lambda b, s, gg: (b, s, gg, 0, 0)),
        out_shape=jax.ShapeDtypeStruct((batch, 2, g, nu, NSA_HD), bf16),
        compiler_params=_params("parallel", "parallel", "parallel"),
        name="compress_prompt",
    )(kvf, pe, w1, w2)


def _select_blocks(sel, n_top, axis):
    ns = sel.shape[axis]
    jidx = lax.broadcasted_iota(jnp.int32, sel.shape, axis)
    chosen = jnp.zeros(sel.shape, f32)
    for _ in range(n_top):
        mx = jnp.max(sel, axis=axis, keepdims=True)
        idx = jnp.min(jnp.where(sel == mx, jidx, ns), axis=axis, keepdims=True)
        hit = jidx == idx
        chosen = jnp.where(hit, 1.0, chosen)
        sel = jnp.where(hit, -jnp.inf, sel)
    return chosen


def _masked_softmax(s, mask):
    s = jnp.where(mask, s, NEG)
    e = jnp.exp(s - jnp.max(s, axis=-1, keepdims=True))
    return jnp.where(mask, e / jnp.sum(e, axis=-1, keepdims=True), 0.0)


def _nsa_attn_kernel(q_ref, kc_ref, vc_ref, ks_ref, vs_ref, kw_ref, vw_ref, gl_ref, msel_ref, exp_ref,
                     o_ref, vs_t, vw_t, vc_t, *, n_top):
    qi = pl.program_id(2)
    blk = q_ref.shape[0]
    r_heads, hd = NSA_HPG, NSA_HD
    t_len = ks_ref.shape[0]
    nc = kc_ref.shape[3]
    ck = 4 * blk

    def transposed(ref_rows):
        return ref_rows.astype(f32).T.astype(bf16)

    @pl.when(qi == 0)
    def _():
        def tr(c, carry):
            off = pl.multiple_of(c * blk, blk)
            vs_t[0:hd, pl.ds(off, blk)] = transposed(vs_ref[pl.ds(off, blk), :])
            vw_t[0:hd, pl.ds(off, blk)] = transposed(vw_ref[pl.ds(off, blk), :])
            return carry

        lax.fori_loop(0, t_len // blk, tr, 0)
        vs_t[hd:, :] = jnp.ones((vs_t.shape[0] - hd, t_len), bf16)
        vw_t[hd:, :] = jnp.ones((vw_t.shape[0] - hd, t_len), bf16)
        for c in range(nc // blk):
            vc_t[:, c * blk:(c + 1) * blk] = transposed(vc_ref[0, 0, 0, c * blk:(c + 1) * blk, :])

    def head(x, r):
        return x[:, r * blk:(r + 1) * blk]

    def per_head(fn):
        return jnp.concatenate([fn(r) for r in range(r_heads)], axis=1)

    q = q_ref[...].astype(f32) * (NSA_SCALE * math.log2(math.e))
    q_t = per_head(lambda r: q[:, r * hd:(r + 1) * hd].T).astype(bf16)
    t_q = qi * blk + lax.broadcasted_iota(jnp.int32, (1, blk), 1)

    z = _dot(kc_ref[0, 0, 0], q_t)
    c_end = lax.broadcasted_iota(jnp.int32, (nc, 1), 0) * CMP_STRIDE + (CMP_BLOCK - 1)
    mc = c_end <= t_q
    z = per_head(lambda r: jnp.where(mc, head(z, r), NEG))
    e = jnp.exp2(z - jnp.max(z, axis=0, keepdims=True))
    p = e * (1.0 / jnp.sum(e, axis=0, keepdims=True))
    p = per_head(lambda r: jnp.where(mc, head(p, r), 0.0))
    oc_t = _dot(vc_t[...], p.astype(bf16))

    span = WINDOW + blk
    w_off = pl.multiple_of(jnp.clip(qi * blk - WINDOW, 0, t_len - span), blk)
    kpos = w_off + lax.broadcasted_iota(jnp.int32, (span, 1), 0)
    in_win = (kpos <= t_q) & (kpos > t_q - WINDOW)
    z = _dot(kw_ref[pl.ds(w_off, span), :], q_t)
    z = per_head(lambda r: jnp.where(in_win, head(z, r), NEG))
    e = jnp.exp2(z - jnp.max(z, axis=0, keepdims=True))
    pv = _dot(vw_t[:, pl.ds(w_off, span)], e.astype(bf16))
    ow_t = pv[0:hd] * (1.0 / pv[hd:hd + 1])

    imp = head(p, 0)
    for r in range(1, r_heads):
        imp = imp + head(p, r)
    sel = jnp.dot(msel_ref[...], imp, precision=lax.Precision.HIGHEST, preferred_element_type=f32)
    jidx = lax.broadcasted_iota(jnp.int32, sel.shape, 0)
    cur = lax.shift_right_logical(t_q, int(math.log2(SEL_BLOCK)))
    forced = (jidx == 0) | (jidx == cur) | (jidx == cur - 1)
    sel = jnp.where(jidx * SEL_BLOCK <= t_q, jnp.where(forced, FORCE_SCORE, sel), NEG)
    chosen = _select_blocks(sel, n_top, 0).astype(bf16)

    def flash(k_ref, vt_ref, n_chunks, chunk_of, mask_fn):
        def body(i, carry):
            m, acc = carry
            off = pl.multiple_of(chunk_of(i) * ck, ck)
            z = _dot(k_ref[pl.ds(off, ck), :], q_t)
            msk = mask_fn(off, off + lax.broadcasted_iota(jnp.int32, (ck, 1), 0))
            z = per_head(lambda r: jnp.where(msk, head(z, r), NEG))
            m_new = jnp.maximum(m, jnp.max(z, axis=0, keepdims=True))
            alpha = jnp.exp2(m - m_new)
            e = jnp.exp2(z - m_new)
            return m_new, acc * alpha + _dot(vt_ref[:, pl.ds(off, ck)], e.astype(bf16))

        width = r_heads * blk
        init = (jnp.full((1, width), NEG, f32), jnp.zeros((vt_ref.shape[0], width), f32))
        _, acc = lax.fori_loop(0, n_chunks, body, init)
        return acc[0:hd] * (1.0 / acc[hd:hd + 1])

    os_t = flash(ks_ref, vs_t, qi // (ck // blk) + 1, lambda i: i,
                 lambda off, kpos: (_dot(exp_ref[pl.ds(off, ck), :], chosen) > 0.5) & (kpos <= t_q))

    g_t = jax.nn.sigmoid(gl_ref[...]).T
    outs = []
    for r in range(r_heads):
        o_r = (g_t[3 * r:3 * r + 1] * head(oc_t, r) + g_t[3 * r + 1:3 * r + 2] * head(os_t, r)
               + g_t[3 * r + 2:3 * r + 3] * head(ow_t, r))
        outs.append(o_r.T)
    o_ref[...] = jnp.concatenate(outs, axis=1).astype(o_ref.dtype)


def _selection_constants(nu, ns, nkeys):
    c = jnp.arange(nu)[None, :]
    j = jnp.arange(ns)[:, None]
    per_sel = SEL_BLOCK // CMP_STRIDE
    rb = CMP_BLOCK // CMP_STRIDE
    msel = sum(((c + r) // per_sel == j).astype(f32) for r in range(rb)) / rb
    msel = jnp.where(c < nu - rb + 1, msel, 0.0)
    expand = (jnp.arange(nkeys)[:, None] // SEL_BLOCK == jnp.arange(ns)[None, :]).astype(bf16)
    return msel, expand


def nsa_attn_prompt(q, cmp, kvb, gl, batch):
    n = q.shape[0]
    t = n // batch
    blk = math.gcd(t, Q_BLOCK)
    nq = t // blk
    g = NSA_KV
    nu = t // CMP_STRIDE
    ns = t // SEL_BLOCK
    msel, expand = _selection_constants(nu, ns, t)
    gw = NSA_HPG * NSA_HD
    ones_rows = 16
    kv_spec = lambda slot: pl.BlockSpec((t, NSA_HD), lambda b, gg, qi: (b, slot * g + gg))
    cmp_spec = lambda s: pl.BlockSpec((1, 1, 1, nu, NSA_HD), lambda b, gg, qi: (b, s, gg, 0, 0))
    return pl.pallas_call(
        functools.partial(_nsa_attn_kernel, n_top=min(N_SEL, ns)),
        grid=(batch, g, nq),
        in_specs=[
            pl.BlockSpec((blk, gw), lambda b, gg, qi: (b * nq + qi, gg)),
            cmp_spec(0), cmp_spec(1),
            kv_spec(2), kv_spec(3), kv_spec(4), kv_spec(5),
            pl.BlockSpec((blk, LANES), lambda b, gg, qi: (b * nq + qi, gg)),
            pl.BlockSpec((ns, nu), lambda b, gg, qi: (0, 0)),
            pl.BlockSpec((t, ns), lambda b, gg, qi: (0, 0)),
        ],
        out_specs=pl.BlockSpec((blk, gw), lambda b, gg, qi: (b * nq + qi, gg)),
        out_shape=jax.ShapeDtypeStruct((n, NSA_HEADS * NSA_HD), bf16),
        scratch_shapes=[pltpu.VMEM((NSA_HD + ones_rows, t), bf16), pltpu.VMEM((NSA_HD + ones_rows, t), bf16),
                        pltpu.VMEM((NSA_HD, nu), bf16)],
        compiler_params=_params("parallel", "parallel", "arbitrary"),
        name="nsa_attn_prompt",
    )(q, cmp, cmp, kvb, kvb, kvb, kvb, gl, msel, expand)


def _top2(vals, lane):
    width = vals.shape[-1]
    m1 = jnp.max(vals, axis=-1, keepdims=True)
    i1 = jnp.min(jnp.where(vals == m1, lane, width), axis=-1, keepdims=True)
    rest = jnp.where(lane == i1, -2.0, vals)
    m2 = jnp.max(rest, axis=-1, keepdims=True)
    i2 = jnp.min(jnp.where(rest == m2, lane, width), axis=-1, keepdims=True)
    return m1, i1, m2, i2


def _router_kernel(x_ref, w_ref, b_ref, info_ref, cnt_ref, carry):
    @pl.when(pl.program_id(0) == 0)
    def _():
        carry[...] = jnp.zeros_like(carry)

    logits = jnp.dot(x_ref[...], w_ref[...], precision=lax.Precision.HIGHEST, preferred_element_type=f32)
    logits = logits + b_ref[...]
    e = jnp.exp(logits - jnp.max(logits, axis=-1, keepdims=True))
    aff = e / jnp.sum(e, axis=-1, keepdims=True)
    tm = aff.shape[0]
    lane = lax.broadcasted_iota(jnp.int32, aff.shape, 1)
    lane_grp = lax.shift_right_logical(lane, int(math.log2(EXPERTS_PER_GROUP)))

    best, grp = None, None
    for gidx in range(N_GROUPS):
        m1, _, m2, _ = _top2(jnp.where(lane_grp == gidx, aff, -1.0), lane)
        score = m1 + m2
        if gidx == 0:
            best, grp = score, jnp.zeros_like(lane[:, :1])
        else:
            better = score > best
            grp = jnp.where(better, gidx, grp)
            best = jnp.where(better, score, best)
    m1, i1, m2, i2 = _top2(jnp.where(lane_grp == grp, aff, -1.0), lane)
    den = m1 + m2

    hot1 = (lane == i1).astype(f32)
    hot2 = (lane == i2).astype(f32)
    both = hot1 + hot2
    row = lax.broadcasted_iota(jnp.int32, (tm, tm), 0)
    col = lax.broadcasted_iota(jnp.int32, (tm, tm), 1)
    before = _dot((col < row).astype(bf16), both.astype(bf16)) + carry[...]
    rank1 = jnp.sum(hot1 * before, axis=-1, keepdims=True)
    rank2 = jnp.sum(hot2 * before, axis=-1, keepdims=True)
    carry[...] += jnp.sum(both, axis=0, keepdims=True)
    cnt_ref[...] = carry[...]

    cols = (i1.astype(f32), i2.astype(f32), m1 / den, m2 / den, rank1, rank2)
    info = jnp.zeros(aff.shape, f32)
    for c, v in enumerate(cols):
        info = jnp.where(lane == c, v, info)
    info_ref[...] = info


def moe_router(x, router_w, router_b, *, tm):
    n, d = x.shape
    tm = min(tm, n)
    assert n % tm == 0
    return pl.pallas_call(
        _router_kernel,
        grid=(n // tm,),
        in_specs=[
            pl.BlockSpec((tm, d), lambda i: (i, 0)),
            pl.BlockSpec((d, N_EXPERTS), lambda i: (0, 0)),
            pl.BlockSpec((1, N_EXPERTS), lambda i: (0, 0)),
        ],
        out_specs=[pl.BlockSpec((tm, N_EXPERTS), lambda i: (i, 0)), pl.BlockSpec((1, N_EXPERTS), lambda i: (0, 0))],
        out_shape=[jax.ShapeDtypeStruct((n, N_EXPERTS), f32), jax.ShapeDtypeStruct((1, N_EXPERTS), f32)],
        scratch_shapes=[pltpu.VMEM((1, N_EXPERTS), f32)],
        compiler_params=_params("arbitrary"),
        name="moe_router",
    )(x, router_w, router_b)


def _row_gather(src_hbm, dst, sem, index_of, n_rows, straight_line=False):
    def body(r, carry):
        pltpu.make_async_copy(src_hbm.at[pl.ds(index_of(r), 1), :], dst.at[pl.ds(r, 1), :], sem).start()
        return carry

    if straight_line:
        for r in range(n_rows):
            body(r, 0)
    else:
        lax.fori_loop(0, n_rows, body, 0, unroll=8)


def _row_gather_wait(src_hbm, dst, sem):
    pltpu.make_async_copy(src_hbm.at[pl.ds(0, dst.shape[0]), :], dst, sem).wait()


def _moe_kernel(te_ref, nt_ref, src_ref, x_hbm, wg_ref, wu_ref, wd_ref, o_ref, xbuf, sem):
    t = pl.program_id(0)
    n_live = nt_ref[0]
    tm = xbuf.shape[1]

    @pl.when(t == 0)
    def _():
        _row_gather(x_hbm, xbuf.at[0], sem.at[0], lambda r: src_ref[r], tm)

    @pl.when(t < n_live)
    def _():
        slot = t % 2
        _row_gather_wait(x_hbm, xbuf.at[slot], sem.at[slot])
        nxt = jnp.minimum(t + 1, n_live - 1)
        _row_gather(x_hbm, xbuf.at[1 - slot], sem.at[1 - slot], lambda r: src_ref[nxt * tm + r], tm,
                    straight_line=True)
        x = xbuf[slot].astype(bf16)
        h = jax.nn.silu(_dot(x, wg_ref[0])) * _dot(x, wu_ref[0])
        o_ref[...] = _dot(h.astype(bf16), wd_ref[0])

    @pl.when(t == n_live)
    def _():
        _row_gather_wait(x_hbm, xbuf.at[t % 2], sem.at[t % 2])

    @pl.when(t >= n_live)
    def _():
        o_ref[...] = jnp.zeros_like(o_ref)


def moe_experts(x, src, tile_expert, n_live, w_gate, w_up, w_down, e_base, *, tm):
    p = src.shape[0]
    d = x.shape[1]
    fdim = w_gate.shape[2]
    grid_spec = pltpu.PrefetchScalarGridSpec(
        num_scalar_prefetch=3,
        grid=(p // tm,),
        in_specs=[
            pl.BlockSpec(memory_space=pl.ANY),
            pl.BlockSpec((1, d, fdim), lambda i, te, nt, sr: (e_base + te[i], 0, 0)),
            pl.BlockSpec((1, d, fdim), lambda i, te, nt, sr: (e_base + te[i], 0, 0)),
            pl.BlockSpec((1, fdim, d), lambda i, te, nt, sr: (e_base + te[i], 0, 0)),
        ],
        out_specs=pl.BlockSpec((tm, d), lambda i, te, nt, sr: (i, 0)),
        scratch_shapes=[pltpu.VMEM((2, tm, d), f32), pltpu.SemaphoreType.DMA((2,))],
    )
    return pl.pallas_call(
        _moe_kernel,
        grid_spec=grid_spec,
        out_shape=jax.ShapeDtypeStruct((p, d), f32),
        compiler_params=_params("arbitrary"),
        name="moe_experts",
    )(tile_expert, n_live, src, x, w_gate, w_up, w_down)


def moe_layer(x, router_w, router_b, w_gate, w_up, w_down, e_base, *, tm):
    n = x.shape[0]
    info, counts = moe_router(x, router_w, router_b, tm=512)
    e1, e2 = info[:, 0].astype(jnp.int32), info[:, 1].astype(jnp.int32)
    r1, r2 = info[:, 4].astype(jnp.int32), info[:, 5].astype(jnp.int32)
    cnt = counts[0].astype(jnp.int32)
    padded = (cnt + tm - 1) // tm * tm
    ends = jnp.cumsum(padded)
    starts = ends - padded
    d1, d2 = starts[e1] + r1, starts[e2] + r2
    n_tiles = -(-2 * n // tm) + N_EXPERTS + 1
    p = n_tiles * tm
    tok = jnp.arange(n, dtype=jnp.int32)
    src = jnp.zeros((p,), jnp.int32).at[jnp.concatenate([d1, d2])].set(jnp.concatenate([tok, tok]))
    n_live = ends[-1:] // tm
    tile_ix = jnp.arange(n_tiles, dtype=jnp.int32)
    tile_start = jnp.minimum(tile_ix, n_live[0] - 1) * tm
    tile_expert = jnp.sum((ends[None, :] <= tile_start[:, None]).astype(jnp.int32), axis=1)
    ys = moe_experts(x, src, tile_expert, n_live.astype(jnp.int32), w_gate, w_up, w_down, e_base, tm=tm)
    return ys, d1, d2, info


def _ln_ple_kernel(d1_ref, d2_ref, x_ref, info_ref, ys_hbm, g_ref, b_ref, p_ref, wg_ref, wp_ref, o_ref, ybuf, sem):
    i = pl.program_id(0)
    tm = x_ref.shape[0]

    last = pl.num_programs(0) - 1

    def start(tile, slot, straight_line):
        _row_gather(ys_hbm, ybuf.at[slot, 0], sem.at[slot], lambda r: d1_ref[tile * tm + r], tm, straight_line)
        _row_gather(ys_hbm, ybuf.at[slot, 1], sem.at[slot], lambda r: d2_ref[tile * tm + r], tm, straight_line)

    def wait(slot):
        _row_gather_wait(ys_hbm, ybuf.at[slot, 0], sem.at[slot])
        _row_gather_wait(ys_hbm, ybuf.at[slot, 1], sem.at[slot])

    @pl.when(i == 0)
    def _():
        start(0, 0, False)

    slot = i % 2
    wait(slot)
    start(jnp.minimum(i + 1, last), 1 - slot, True)
    info = info_ref[...]
    y = info[:, 2:3] * ybuf[slot, 0] + info[:, 3:4] * ybuf[slot, 1]
    x2 = _layer_norm(DN_ALPHA * x_ref[...] + y, g_ref[...], b_ref[...])
    gate = jax.nn.sigmoid(_dot(x2.astype(bf16), wg_ref[...]))
    o_ref[...] = x2 + gate * _dot(p_ref[...].astype(bf16), wp_ref[...])

    @pl.when(i == last)
    def _():
        wait(1 - slot)


def ln_ple(x, ys, d1, d2, info, g, b, p, w_gate, w_proj, *, tm):
    n, d = x.shape
    tm = min(tm, n)
    assert n % tm == 0
    pd = p.shape[1]
    row = lambda i, a, c: (i, 0)
    fixed = lambda i, a, c: (0, 0)
    grid_spec = pltpu.PrefetchScalarGridSpec(
        num_scalar_prefetch=2,
        grid=(n // tm,),
        in_specs=[
            pl.BlockSpec((tm, d), row), pl.BlockSpec((tm, info.shape[1]), row), pl.BlockSpec(memory_space=pl.ANY),
            pl.BlockSpec((1, d), fixed), pl.BlockSpec((1, d), fixed),
            pl.BlockSpec((tm, pd), row), pl.BlockSpec((d, d), fixed), pl.BlockSpec((pd, d), fixed),
        ],
        out_specs=pl.BlockSpec((tm, d), row),
        scratch_shapes=[pltpu.VMEM((2, 2, tm, d), f32), pltpu.SemaphoreType.DMA((2,))],
    )
    return pl.pallas_call(
        _ln_ple_kernel,
        grid_spec=grid_spec,
        out_shape=jax.ShapeDtypeStruct((n, d), f32),
        compiler_params=_params("arbitrary"),
        name="ln_ple",
    )(d1, d2, x, info, ys, g, b, p, w_gate, w_proj)


def _retention_decode_kernel(q_ref, k_ref, v_ref, g_ref, gn_ref, dec_ref, s_ref, o_ref, so_ref):
    q = q_ref[0]
    k = k_ref[0] * (RET_DK ** -0.5)
    v = v_ref[0]
    gamma = dec_ref[0]
    s_prev = s_ref[0, 0]
    att = jnp.sum(q.astype(f32) * k.astype(f32), axis=-1, keepdims=True)
    rows = 8
    q8 = jnp.broadcast_to(q, (rows, RET_DK))
    first = (lax.broadcasted_iota(jnp.int32, (rows, 1), 0) == 0).astype(f32)
    k8 = (jnp.broadcast_to(k.astype(f32), (rows, RET_DK)) * first).astype(bf16)
    v8 = jnp.broadcast_to(v, (rows, RET_DV))
    o = att.astype(bf16).astype(f32) * v.astype(f32) + _dot(q8, s_prev.astype(bf16))[:1] * gamma
    so_ref[0, 0] = s_prev * gamma + _dot_tn(k8, v8)
    mu = jnp.mean(o, axis=-1, keepdims=True)
    d = o - mu
    var = jnp.mean(d * d, axis=-1, keepdims=True)
    o = d * lax.rsqrt(var + LN_EPS) * gn_ref[...]
    o_ref[0] = (jax.nn.silu(g_ref[0].astype(f32)) * o).astype(o_ref.dtype)


def retention_decode(qk, vg, gn_g, states, base):
    b = qk.shape[0]
    h = RET_HEADS
    _, qdec, _ = _retention_decay_tables(1)
    qk3, vg3 = qk[:, None, :], vg[:, None, :]
    o, s = pl.pallas_call(
        _retention_decode_kernel,
        grid=(b, h),
        in_specs=[
            pl.BlockSpec((1, 1, RET_DK), lambda i, hh: (i, 0, hh)),
            pl.BlockSpec((1, 1, RET_DK), lambda i, hh: (i, 0, h + hh)),
            pl.BlockSpec((1, 1, RET_DV), lambda i, hh: (i, 0, hh)),
            pl.BlockSpec((1, 1, RET_DV), lambda i, hh: (i, 0, h + hh)),
            pl.BlockSpec((1, RET_DV), lambda i, hh: (0, hh)),
            pl.BlockSpec((1, 1, 1), lambda i, hh: (hh, 0, 0)),
            pl.BlockSpec((1, 1, RET_DK, RET_DV), lambda i, hh: (base + i, hh, 0, 0)),
        ],
        out_specs=[
            pl.BlockSpec((1, 1, RET_DV), lambda i, hh: (i, 0, hh)),
            pl.BlockSpec((1, 1, RET_DK, RET_DV), lambda i, hh: (i, hh, 0, 0)),
        ],
        out_shape=[jax.ShapeDtypeStruct((b, 1, h * RET_DV), bf16), jax.ShapeDtypeStruct((b, h, RET_DK, RET_DV), f32)],
        compiler_params=_params("parallel", "parallel"),
        name="retention_decode",
    )(qk3, qk3, vg3, vg3, gn_g, qdec, states)
    return o[:, 0, :], s


HIST_PAGES_PER_STEP = 32


def _compress_hist_kernel(pt_ref, *refs, n_pages):
    page_refs = refs[:n_pages]
    pe_ref, w1_ref, w2_ref, o_ref, a_buf, o_buf = refs[n_pages:]
    g = NSA_KV
    upp = page_refs[0].shape[1] // CMP_STRIDE
    rows = n_pages * upp * g
    pad = a_buf.shape[0] - rows

    def load(l):
        return jnp.concatenate([pr[0, pl.ds(l, upp, stride=CMP_STRIDE), :, :].reshape(upp * g, NSA_HD)
                                for pr in page_refs], axis=0)

    a, b = _compress_units(load, pe_ref[0], w1_ref)

    @pl.when(pl.program_id(2) == 0)
    def _():
        a_buf[0:pad, :] = jnp.zeros((pad, a_buf.shape[1]), f32)

    @pl.when(pl.program_id(2) > 0)
    def _():
        a_buf[0:pad, :] = a_buf[rows:rows + pad, :]

    a_buf[pad:pad + rows, :] = a
    a_prev = a_buf[pad - g:pad - g + rows, :]
    o_buf[...] = _dot(jax.nn.gelu(a_prev + b).astype(bf16), w2_ref[0])
    for gg in range(g):
        o_ref[0, 0, gg] = o_buf[pl.ds(gg, rows // g, stride=g), :].astype(o_ref.dtype)


def compress_history(pool, page_table, pe, w1, w2):
    b, ppb = page_table.shape
    page = pool.shape[1]
    g = NSA_KV
    n_pages = min(HIST_PAGES_PER_STEP, ppb)
    assert ppb % n_pages == 0 and page % CMP_STRIDE == 0
    upp = page // CMP_STRIDE
    seg = n_pages * upp
    nu = ppb * upp
    sublanes = 8
    page_spec = lambda k: pl.BlockSpec(
        (1, page, None, g, NSA_HD), lambda i, s, ch, pt: (pt[i, ch * n_pages + k], 0, s, 0, 0))
    grid_spec = pltpu.PrefetchScalarGridSpec(
        num_scalar_prefetch=1,
        grid=(b, 2, ppb // n_pages),
        in_specs=[page_spec(k) for k in range(n_pages)] + [
            pl.BlockSpec((1, CMP_BLOCK, NSA_HD), lambda i, s, ch, pt: (s, 0, 0)),
            pl.BlockSpec((1, CMP_BLOCK * NSA_HD, CMP_HIDDEN), lambda i, s, ch, pt: (s, 0, 0)),
            pl.BlockSpec((1, CMP_HIDDEN, NSA_HD), lambda i, s, ch, pt: (s, 0, 0)),
        ],
        out_specs=pl.BlockSpec((1, 1, g, seg, NSA_HD), lambda i, s, ch, pt: (i, s, 0, ch, 0)),
        scratch_shapes=[pltpu.VMEM((seg * g + sublanes, CMP_HIDDEN), f32), pltpu.VMEM((seg * g, NSA_HD), f32)],
    )
    return pl.pallas_call(
        functools.partial(_compress_hist_kernel, n_pages=n_pages),
        grid_spec=grid_spec,
        out_shape=jax.ShapeDtypeStruct((b, 2, g, nu, NSA_HD), bf16),
        compiler_params=_params("parallel", "parallel", "arbitrary"),
        name="compress_history",
    )(page_table, *([pool] * n_pages), pe, w1, w2)


def _nsa_select_decode_kernel(q_ref, kc_ref, vc_ref, msel_ref, oc_ref, top_ref, *, t, ns, n_top):
    r_heads, hd = NSA_HPG, NSA_HD
    q = q_ref[0]
    qb = jnp.concatenate([q[:, r * hd:(r + 1) * hd] for r in range(r_heads)], axis=0)
    kc = kc_ref[0, 0, 0]
    nu = kc.shape[0]
    s = _dot_nt(qb, kc) * NSA_SCALE
    u = lax.broadcasted_iota(jnp.int32, (1, nu), 1)
    p = _masked_softmax(s, (u >= 1) & (u * CMP_STRIDE + (CMP_STRIDE - 1) <= t))
    oc_ref[0, 0] = _dot(p.astype(bf16), vc_ref[0, 0, 0])
    imp = jnp.sum(p, axis=0, keepdims=True)
    sel = jnp.dot(imp, msel_ref[...], precision=lax.Precision.HIGHEST, preferred_element_type=f32)
    jidx = lax.broadcasted_iota(jnp.int32, sel.shape, 1)
    cur = t // SEL_BLOCK
    forced = (jidx == 0) | (jidx == cur) | (jidx == cur - 1)
    sel = jnp.where(jidx * SEL_BLOCK <= t, jnp.where(forced, FORCE_SCORE, sel), NEG)
    sel = jnp.where(jidx < ns, sel, -jnp.inf)
    width = sel.shape[-1]
    lane = lax.broadcasted_iota(jnp.int32, (1, top_ref.shape[-1]), 1)
    top = jnp.zeros(lane.shape, jnp.int32)
    for it in range(n_top):
        mx = jnp.max(sel, axis=-1, keepdims=True)
        idx = jnp.min(jnp.where(sel == mx, jidx, width), axis=-1, keepdims=True)
        top = jnp.where(lane == it, idx, top)
        sel = jnp.where(jidx == idx, -jnp.inf, sel)
    top_ref[0, 0] = top


def nsa_select_decode(q, cmp, t):
    b = q.shape[0]
    g = NSA_KV
    nu = cmp.shape[3]
    ns = (t + 1 + SEL_BLOCK - 1) // SEL_BLOCK
    ns_pad = -(-ns // LANES) * LANES
    per_sel = SEL_BLOCK // CMP_STRIDE
    u = jnp.arange(nu)[:, None]
    j = jnp.arange(ns_pad)[None, :]
    rb = CMP_BLOCK // CMP_STRIDE
    msel = sum(((u - 1 + r) // per_sel == j).astype(f32) for r in range(rb)) / rb
    msel = jnp.where(u >= 1, msel, 0.0)
    gw = NSA_HPG * NSA_HD
    n_top = min(N_SEL, ns)
    cmp_spec = lambda s: pl.BlockSpec((1, 1, 1, nu, NSA_HD), lambda i, gg: (i, s, gg, 0, 0))
    return pl.pallas_call(
        functools.partial(_nsa_select_decode_kernel, t=t, ns=ns, n_top=n_top),
        grid=(b, g),
        in_specs=[
            pl.BlockSpec((1, 1, gw), lambda i, gg: (i, 0, gg)),
            cmp_spec(0), cmp_spec(1),
            pl.BlockSpec((nu, ns_pad), lambda i, gg: (0, 0)),
        ],
        out_specs=[
            pl.BlockSpec((1, 1, NSA_HPG, NSA_HD), lambda i, gg: (i, gg, 0, 0)),
            pl.BlockSpec((1, 1, 1, LANES), lambda i, gg: (i, gg, 0, 0)),
        ],
        out_shape=[jax.ShapeDtypeStruct((b, g, NSA_HPG, NSA_HD), f32), jax.ShapeDtypeStruct((b, g, 1, LANES), jnp.int32)],
        compiler_params=_params("parallel", "parallel"),
        name="nsa_select_decode",
    )(q[:, None, :], cmp, cmp, msel)


def _nsa_attn_decode_kernel(pt_ref, top_ref, *refs, n_top, n_hist_blocks):
    k_refs, v_refs = refs[:n_top], refs[n_top:2 * n_top]
    (q_ref, oc_ref, knew_ref, vnew_ref, kw_ref, vw_ref, kwnew_ref, vwnew_ref, gl_ref, o_ref) = refs[2 * n_top:]
    i, gg = pl.program_id(0), pl.program_id(1)
    r_heads, hd = NSA_HPG, NSA_HD
    q = q_ref[0]
    qb = jnp.concatenate([q[:, r * hd:(r + 1) * hd] for r in range(r_heads)], axis=0)

    def attend(keys, vals, mask, k_new, v_new):
        s = jnp.where(mask, _dot_nt(qb, keys) * NSA_SCALE, NEG)
        s_new = jnp.sum(qb.astype(f32) * k_new.astype(f32), axis=-1, keepdims=True) * NSA_SCALE
        m = jnp.maximum(jnp.max(s, axis=-1, keepdims=True), s_new)
        e = jnp.where(mask, jnp.exp(s - m), 0.0)
        e_new = jnp.exp(s_new - m)
        num = _dot(e.astype(bf16), vals) + e_new.astype(bf16).astype(f32) * v_new.astype(f32)
        return num / (jnp.sum(e, axis=-1, keepdims=True) + e_new)

    g = NSA_KV

    def rows_of(ref):
        v = ref[0]
        return v.reshape(v.shape[0] * g, hd).astype(bf16)

    def own_group(n_rows):
        return lax.broadcasted_iota(jnp.int32, (1, n_rows * g), 1) % g == gg

    sb = k_refs[0].shape[1]
    keys = jnp.concatenate([rows_of(r) for r in k_refs], axis=0)
    vals = jnp.concatenate([rows_of(r) for r in v_refs], axis=0)
    blk_of_lane = lax.broadcasted_iota(jnp.int32, (1, n_top * sb * g), 1) // (sb * g)
    sel_of_lane = jnp.zeros((1, n_top * sb * g), jnp.int32)
    for n in range(n_top):
        sel_of_lane = jnp.where(blk_of_lane == n, top_ref[i, gg, n], sel_of_lane)
    o_s = attend(keys, vals, (sel_of_lane < n_hist_blocks) & own_group(n_top * sb), knew_ref[0], vnew_ref[0])

    wlen = kw_ref.shape[1]
    wmask = (lax.broadcasted_iota(jnp.int32, (1, wlen * g), 1) >= g) & own_group(wlen)
    o_w = attend(rows_of(kw_ref), rows_of(vw_ref), wmask, kwnew_ref[0], vwnew_ref[0])

    gates = jax.nn.sigmoid(gl_ref[0])
    o_c = oc_ref[0, 0]
    outs = []
    for r in range(r_heads):
        outs.append(gates[:, 3 * r:3 * r + 1] * o_c[r:r + 1] + gates[:, 3 * r + 1:3 * r + 2] * o_s[r:r + 1]
                    + gates[:, 3 * r + 2:3 * r + 3] * o_w[r:r + 1])
    o_ref[0] = jnp.concatenate(outs, axis=1).astype(o_ref.dtype)


def nsa_attn_decode(q, o_c, top, kvb, gl, pool, page_table, win, win_base, t):
    b = q.shape[0]
    g = NSA_KV
    n_top = top.shape[-1]
    page = pool.shape[1]
    assert t % SEL_BLOCK == 0 and page % SEL_BLOCK == 0 and win.shape[1] == WINDOW
    n_hist_blocks = t // SEL_BLOCK
    bpp = page // SEL_BLOCK
    gw = NSA_HPG * NSA_HD

    def blk_spec(n, slot):
        def imap(i, gg, pt, tp):
            j = jnp.minimum(tp[i, gg, n], n_hist_blocks - 1)
            return (pt[i, j // bpp], j % bpp, slot, 0, 0)
        return pl.BlockSpec((1, SEL_BLOCK, None, g, NSA_HD), imap)

    new_spec = lambda slot: pl.BlockSpec((1, 1, NSA_HD), lambda i, gg, pt, tp: (i, 0, slot * g + gg))
    win_spec = lambda slot: pl.BlockSpec((1, WINDOW, None, g, NSA_HD),
                                         lambda i, gg, pt, tp: (win_base + i, 0, slot, 0, 0))
    kvb3 = kvb[:, None, :]
    grid_spec = pltpu.PrefetchScalarGridSpec(
        num_scalar_prefetch=2,
        grid=(b, g),
        in_specs=[blk_spec(n, 2) for n in range(n_top)] + [blk_spec(n, 3) for n in range(n_top)] + [
            pl.BlockSpec((1, 1, gw), lambda i, gg, pt, tp: (i, 0, gg)),
            pl.BlockSpec((1, 1, NSA_HPG, NSA_HD), lambda i, gg, pt, tp: (i, gg, 0, 0)),
            new_spec(2), new_spec(3), win_spec(0), win_spec(1), new_spec(4), new_spec(5),
            pl.BlockSpec((1, 1, LANES), lambda i, gg, pt, tp: (i, 0, gg)),
        ],
        out_specs=pl.BlockSpec((1, 1, gw), lambda i, gg, pt, tp: (i, 0, gg)),
    )
    o = pl.pallas_call(
        functools.partial(_nsa_attn_decode_kernel, n_top=n_top, n_hist_blocks=n_hist_blocks),
        grid_spec=grid_spec,
        out_shape=jax.ShapeDtypeStruct((b, 1, NSA_HEADS * NSA_HD), bf16),
        compiler_params=_params("parallel", "parallel"),
        name="nsa_attn_decode",
    )(page_table, top, *([pool] * (2 * n_top)), q[:, None, :], o_c, kvb3, kvb3, win, win, kvb3, kvb3,
      gl[:, None, :])
    return o[:, 0, :]


def _retention_mixer(x, pos, batch, w, states, base):
    cos, sin = _rope_tables(pos, RET_DK)
    tn = PROJ_TILE
    n_qk = 2 * RET_HEADS * RET_DK
    flags = jnp.ones((n_qk // tn,), jnp.int32)
    qk, = mm_rope(x, w["in"], (0, n_qk), cos, sin, flags, [bf16], hd=RET_DK, tm=PROJ_TILE, tn=tn)
    vg = mm(x, w["in"], (n_qk, 2 * RET_HEADS * RET_DV), bf16, tm=PROJ_TILE, tn=PROJ_TILE)
    if states is None:
        return retention_prompt(qk, vg, w["gn"], batch)
    return retention_decode(qk, vg, w["gn"], states, base)


def _nsa_projections(x, pos, w):
    cos, sin = _rope_tables(pos, NSA_HD)
    tn = NSA_KV * NSA_HD
    n_q = NSA_HEADS * NSA_HD
    n_kv = 6 * NSA_KV * NSA_HD
    q, = mm_rope(x, w["in"], (0, n_q), cos, sin, jnp.ones((n_q // tn,), jnp.int32), [bf16],
                 hd=NSA_HD, tm=PROJ_TILE, tn=tn)
    kvf, kvb = mm_rope(x, w["in"], (n_q, n_kv), cos, sin, jnp.array([1, 0] * 3, jnp.int32), [f32, bf16],
                       hd=NSA_HD, tm=PROJ_TILE, tn=tn)
    gl = mm(x, w["gl"], (0, w["gl"].shape[1]), f32, tm=PROJ_TILE, tn=tn)
    return q, kvf, kvb, gl


def _layer_tail(x, h, p, w, tm_moe):
    x1 = mm_ln(h, w["out"], x, w["ln_g"][0:1], w["ln_b"][0:1], tm=512, tk=2048)
    ys, d1, d2, info = moe_layer(x1, w["router_w"], w["router_b"], w["moe_gate"], w["moe_up"], w["moe_down"],
                                 w["moe_base"], tm=tm_moe)
    return ln_ple(x1, ys, d1, d2, info, w["ln_g"][1:2], w["ln_b"][1:2], p, w["ple_gate"], w["ple_proj"], tm=256)


def kernel(x_prompt, x_sample, state_ret, cache_nsa_kv, state_nsa_win, page_table, p_prompt, p_sample, ret_w_in, ret_w_out, ret_gn_g, nsa_w_in, nsa_w_out, nsa_cmp_pos, nsa_cmp_w1, nsa_cmp_w2, ln_g, ln_b, router_w, router_b, moe_w_gate, moe_w_up, moe_w_down, ple_w_gate, ple_w_proj):
    bp, tp, d = x_prompt.shape
    bs, ts, _ = x_sample.shape
    assert ts == 1
    n_pool, page = cache_nsa_kv.shape[1], cache_nsa_kv.shape[2]
    past = page_table.shape[1] * page
    g, hd = NSA_KV, NSA_HD
    kv_cols = N_KV_SLOTS * g * hd

    xp = x_prompt.reshape(bp * tp, d)
    xs = x_sample.reshape(bs * ts, d)
    pos_p = jnp.arange(tp, dtype=jnp.int32)
    pos_s = jnp.full((bs,), past, jnp.int32)
    states = state_ret.reshape((-1,) + state_ret.shape[2:])
    pool = cache_nsa_kv.reshape((-1,) + cache_nsa_kv.shape[2:])
    wins = state_nsa_win.reshape((-1,) + state_nsa_win.shape[2:])

    moe_gate = moe_w_gate.astype(bf16).reshape((-1,) + moe_w_gate.shape[2:])
    moe_up = moe_w_up.astype(bf16).reshape((-1,) + moe_w_up.shape[2:])
    moe_down = moe_w_down.astype(bf16).reshape((-1,) + moe_w_down.shape[2:])

    ret_p, ret_s, kv_p, kv_s, win_p, win_s = [], [], [], [], [], []
    for i in range(DEPTH):
        j = i // 2
        w = {
            "ln_g": ln_g[i], "ln_b": ln_b[i],
            "router_w": router_w, "router_b": router_b[None, :],
            "moe_gate": moe_gate, "moe_up": moe_up, "moe_down": moe_down, "moe_base": i * N_EXPERTS,
            "ple_gate": ple_w_gate[i].astype(bf16), "ple_proj": ple_w_proj[i].astype(bf16),
        }
        if i % 2 == 0:
            w.update({"in": ret_w_in[j].astype(bf16), "out": ret_w_out[j].astype(bf16), "gn": ret_gn_g[j][None, :]})
            hp, sp = _retention_mixer(xp, pos_p, bp, w, None, 0)
            hs, ss = _retention_mixer(xs, pos_s, bs, w, states, j * bs)
            ret_p.append(sp)
            ret_s.append(ss)
        else:
            nq = NSA_HEADS * hd
            w_in = nsa_w_in[j]
            gl = w_in[:, nq + 6 * g * hd:].reshape(d, g, NSA_HPG * 3)
            gl = jnp.pad(gl, ((0, 0), (0, 0), (0, LANES - NSA_HPG * 3))).reshape(d, g * LANES)
            w.update({"in": w_in.astype(bf16), "gl": gl.astype(bf16), "out": nsa_w_out[j].astype(bf16)})
            pe = nsa_cmp_pos[j]
            w1 = nsa_cmp_w1[j].reshape(2, CMP_BLOCK * hd, CMP_HIDDEN).astype(bf16)
            w2 = nsa_cmp_w2[j].astype(bf16)
            q, kvf, kvb, glp = _nsa_projections(xp, pos_p, w)
            cmp = compress_prompt(kvf, pe, w1, w2, bp)
            hp = nsa_attn_prompt(q, cmp, kvb, glp, bp)
            kv_p.append(kvf[:, :kv_cols].reshape(bp, tp, N_KV_SLOTS, g, hd))
            keep = min(WINDOW, tp)
            win_p.append(kvf.reshape(bp, tp, -1)[:, tp - keep:, kv_cols:].reshape(bp, keep, 2, g, hd))
            q, kvf, kvb, gls = _nsa_projections(xs, pos_s, w)
            pt = page_table + j * n_pool
            cmp = compress_history(pool, pt, pe, w1, w2)
            o_c, top = nsa_select_decode(q, cmp, past)
            top = top[:, :, 0, :min(N_SEL, past // SEL_BLOCK + 1)]
            hs = nsa_attn_decode(q, o_c, top, kvb, gls, pool, pt, wins, j * bs, past)
            kv_s.append(kvf[:, :kv_cols].reshape(bs, ts, N_KV_SLOTS, g, hd))
            new_win = kvf[:, kv_cols:].reshape(bs, ts, 2, g, hd)
            win_s.append(jnp.concatenate([state_nsa_win[j][:, ts:], new_win], axis=1))
        xp = _layer_tail(xp, hp, p_prompt[i].reshape(bp * tp, -1), w, MOE_TILE)
        xs = _layer_tail(xs, hs, p_sample[i].reshape(bs * ts, -1), w, 16)

    return (xp.reshape(bp, tp, d), xs.reshape(bs, ts, d), jnp.stack(ret_p), jnp.stack(ret_s),
            jnp.stack(kv_p), jnp.stack(kv_s), jnp.stack(win_p), jnp.stack(win_s))
```

```python
import functools
import math

import jax
import jax.numpy as jnp
from jax import lax
from jax.experimental import pallas as pl
from jax.experimental.pallas import tpu as pltpu

f32 = jnp.float32
bf16 = jnp.bfloat16

D_MODEL = 2048
DEPTH = 4
RET_HEADS = 8
RET_DK = D_MODEL // RET_HEADS
RET_DV = 2 * RET_DK
RET_CHUNK = 128
NSA_HEADS = 16
NSA_HD = D_MODEL // NSA_HEADS
NSA_KV = 4
NSA_HPG = NSA_HEADS // NSA_KV
CMP_BLOCK = 32
CMP_STRIDE = 16
CMP_HIDDEN = 4 * NSA_HD
SEL_BLOCK = 64
N_SEL = 16
WINDOW = 512
Q_BLOCK = 128
N_KV_SLOTS = 4
N_EXPERTS = 16
N_GROUPS = 4
EXPERTS_PER_GROUP = N_EXPERTS // N_GROUPS
D_EXPERT = 1408
PLE_DIM = 256
ROPE_THETA = 10000.0
LN_EPS = 1e-5
DN_ALPHA = (2 * DEPTH) ** 0.25
NEG = -1e30
FORCE_SCORE = float(NSA_HPG + 1)
NSA_SCALE = NSA_HD ** -0.5

V7X_VMEM_BYTES = 64 * 1024 * 1024
VMEM_LIMIT = V7X_VMEM_BYTES - 8 * 1024 * 1024
LANES = 128
MOE_TILE = 256
PROJ_TILE = 1024
RET_HEADS_PER_STEP = 4


def _params(*sem):
    return pltpu.CompilerParams(dimension_semantics=sem, vmem_limit_bytes=VMEM_LIMIT)


def _layer_norm(v, g, b):
    mu = jnp.mean(v, axis=-1, keepdims=True)
    d = v - mu
    var = jnp.mean(d * d, axis=-1, keepdims=True)
    return d * lax.rsqrt(var + LN_EPS) * g + b


def _dot(a, b):
    return jnp.dot(a, b, preferred_element_type=f32)


def _dot_nt(a, b):
    return lax.dot_general(a, b, (((1,), (1,)), ((), ())), preferred_element_type=f32)


def _dot_tn(a, b):
    return lax.dot_general(a, b, (((0,), (0,)), ((), ())), preferred_element_type=f32)


def _rotate_half(v, hd):
    pieces = []
    for c in range(0, v.shape[1], hd):
        if hd == 2 * LANES:
            pieces += [v[:, c + LANES:c + hd], v[:, c:c + LANES]]
        else:
            pieces.append(pltpu.roll(v[:, c:c + hd], hd // 2, 1))
    return jnp.concatenate(pieces, axis=1)


def _mm_rope_kernel(flags_ref, x_ref, w_ref, cos_ref, sin_ref, *o_refs, hd):
    acc = _dot(x_ref[...].astype(bf16), w_ref[...])
    roped = flags_ref[pl.program_id(1)] == 1

    @pl.when(roped)
    def _():
        reps = acc.shape[1] // hd
        cos = jnp.concatenate([cos_ref[...]] * reps, axis=1)
        sin = jnp.concatenate([sin_ref[...]] * reps, axis=1)
        v = acc * cos + _rotate_half(acc, hd) * sin
        for o in o_refs:
            o[...] = v.astype(o.dtype)

    @pl.when(jnp.logical_not(roped))
    def _():
        for o in o_refs:
            o[...] = acc.astype(o.dtype)


def mm_rope(x, w, cols, cos, sin, flags, out_dtypes, *, hd, tm, tn):
    n, k = x.shape
    c0, m = cols
    t = cos.shape[0]
    tm = min(tm, n)
    assert n % tm == 0 and m % tn == 0 and c0 % tn == 0 and t % tm == 0 and tn % hd == 0
    tper = t // tm
    j0 = c0 // tn
    grid_spec = pltpu.PrefetchScalarGridSpec(
        num_scalar_prefetch=1,
        grid=(n // tm, m // tn),
        in_specs=[
            pl.BlockSpec((tm, k), lambda i, j, f: (i, 0)),
            pl.BlockSpec((k, tn), lambda i, j, f: (0, j0 + j)),
            pl.BlockSpec((tm, hd), lambda i, j, f: (i % tper, 0)),
            pl.BlockSpec((tm, hd), lambda i, j, f: (i % tper, 0)),
        ],
        out_specs=[pl.BlockSpec((tm, tn), lambda i, j, f: (i, j)) for _ in out_dtypes],
    )
    return pl.pallas_call(
        functools.partial(_mm_rope_kernel, hd=hd),
        grid_spec=grid_spec,
        out_shape=[jax.ShapeDtypeStruct((n, m), d) for d in out_dtypes],
        compiler_params=_params("parallel", "arbitrary"),
        name="mm_rope",
    )(flags, x, w, cos, sin)


def _mm_kernel(x_ref, w_ref, o_ref):
    o_ref[...] = _dot(x_ref[...].astype(bf16), w_ref[...]).astype(o_ref.dtype)


def mm(x, w, cols, out_dtype, *, tm, tn):
    n, k = x.shape
    c0, m = cols
    tm = min(tm, n)
    tn = min(tn, m)
    assert n % tm == 0 and m % tn == 0 and c0 % tn == 0
    j0 = c0 // tn
    return pl.pallas_call(
        _mm_kernel,
        grid=(n // tm, m // tn),
        in_specs=[pl.BlockSpec((tm, k), lambda i, j: (i, 0)), pl.BlockSpec((k, tn), lambda i, j: (0, j0 + j))],
        out_specs=pl.BlockSpec((tm, tn), lambda i, j: (i, j)),
        out_shape=jax.ShapeDtypeStruct((n, m), out_dtype),
        compiler_params=_params("parallel", "arbitrary"),
        name="mm",
    )(x, w)


def _mm_ln_kernel(a_ref, w_ref, x_ref, g_ref, b_ref, o_ref, acc_ref):
    kk = pl.program_id(1)

    @pl.when(kk == 0)
    def _():
        acc_ref[...] = jnp.zeros_like(acc_ref)

    acc_ref[...] += _dot(a_ref[...].astype(bf16), w_ref[...])

    @pl.when(kk == pl.num_programs(1) - 1)
    def _():
        o_ref[...] = _layer_norm(DN_ALPHA * x_ref[...] + acc_ref[...], g_ref[...], b_ref[...])


def _mm_ln_single_kernel(a_ref, w_ref, x_ref, g_ref, b_ref, o_ref):
    acc = _dot(a_ref[...].astype(bf16), w_ref[...])
    o_ref[...] = _layer_norm(DN_ALPHA * x_ref[...] + acc, g_ref[...], b_ref[...])


def mm_ln(a, w, x, g, b, *, tm, tk):
    n, k = a.shape
    d = w.shape[1]
    tm = min(tm, n)
    tk = min(tk, k)
    assert n % tm == 0 and k % tk == 0
    if tk == k:
        return pl.pallas_call(
            _mm_ln_single_kernel,
            grid=(n // tm,),
            in_specs=[
                pl.BlockSpec((tm, k), lambda i: (i, 0)),
                pl.BlockSpec((k, d), lambda i: (0, 0)),
                pl.BlockSpec((tm, d), lambda i: (i, 0)),
                pl.BlockSpec((1, d), lambda i: (0, 0)),
                pl.BlockSpec((1, d), lambda i: (0, 0)),
            ],
            out_specs=pl.BlockSpec((tm, d), lambda i: (i, 0)),
            out_shape=jax.ShapeDtypeStruct((n, d), f32),
            compiler_params=_params("parallel"),
            name="mm_ln",
        )(a, w, x, g, b)
    return pl.pallas_call(
        _mm_ln_kernel,
        grid=(n // tm, k // tk),
        in_specs=[
            pl.BlockSpec((tm, tk), lambda i, kk: (i, kk)),
            pl.BlockSpec((tk, d), lambda i, kk: (kk, 0)),
            pl.BlockSpec((tm, d), lambda i, kk: (i, 0)),
            pl.BlockSpec((1, d), lambda i, kk: (0, 0)),
            pl.BlockSpec((1, d), lambda i, kk: (0, 0)),
        ],
        out_specs=pl.BlockSpec((tm, d), lambda i, kk: (i, 0)),
        out_shape=jax.ShapeDtypeStruct((n, d), f32),
        scratch_shapes=[pltpu.VMEM((tm, d), f32)],
        compiler_params=_params("parallel", "arbitrary"),
        name="mm_ln",
    )(a, w, x, g, b)


def _retention_kernel(q_ref, k_ref, v_ref, g_ref, gn_ref, intra_ref, qdec_ref, kdec_ref, o_ref, s_ref, state):
    c = pl.program_id(2)

    @pl.when(c == 0)
    def _():
        state[...] = jnp.zeros_like(state)

    for hh in range(state.shape[0]):
        kcols = slice(hh * RET_DK, (hh + 1) * RET_DK)
        vcols = slice(hh * RET_DV, (hh + 1) * RET_DV)
        q = q_ref[:, kcols]
        k = k_ref[:, kcols] * (RET_DK ** -0.5)
        v = v_ref[:, vcols]
        qdec = qdec_ref[hh]
        kdec = kdec_ref[hh]
        cdec = qdec[-1:, :]
        s_prev = state[hh]
        att = _dot_nt(q, k) * intra_ref[hh]
        o = _dot(att.astype(bf16), v) + _dot(q, s_prev.astype(bf16)) * qdec
        kd = (k.astype(f32) * kdec).astype(bf16)
        s_new = s_prev * cdec + _dot_tn(kd, v)
        state[hh] = s_new

        mu = jnp.mean(o, axis=-1, keepdims=True)
        d = o - mu
        var = jnp.mean(d * d, axis=-1, keepdims=True)
        o = d * lax.rsqrt(var + LN_EPS) * gn_ref[:, vcols]
        o_ref[:, vcols] = (jax.nn.silu(g_ref[:, vcols].astype(f32)) * o).astype(o_ref.dtype)

    @pl.when(c == pl.num_programs(2) - 1)
    def _():
        s_ref[0] = state[...]


def _retention_decay_tables(chunk):
    h = jnp.arange(RET_HEADS, dtype=f32)
    log_gamma = jnp.log1p(-jnp.exp2(-5.0 - h))
    idx = jnp.arange(chunk, dtype=f32)
    rel = idx[:, None] - idx[None, :]
    intra = jnp.where(rel >= 0, jnp.exp(jnp.maximum(rel, 0.0)[None] * log_gamma[:, None, None]), 0.0)
    qdec = jnp.exp((idx[None, :] + 1.0) * log_gamma[:, None])[..., None]
    kdec = jnp.exp((chunk - 1.0 - idx[None, :]) * log_gamma[:, None])[..., None]
    return intra, qdec, kdec


def retention_prompt(qk, vg, gn_g, batch):
    n = qk.shape[0]
    t = n // batch
    chunk = math.gcd(t, RET_CHUNK)
    nch = t // chunk
    h = RET_HEADS
    hps = RET_HEADS_PER_STEP
    hb = h // hps
    intra, qdec, kdec = _retention_decay_tables(chunk)
    row = lambda b, hh, c: b * nch + c
    return pl.pallas_call(
        _retention_kernel,
        grid=(batch, hb, nch),
        in_specs=[
            pl.BlockSpec((chunk, hps * RET_DK), lambda b, hh, c: (row(b, hh, c), hh)),
            pl.BlockSpec((chunk, hps * RET_DK), lambda b, hh, c: (row(b, hh, c), hb + hh)),
            pl.BlockSpec((chunk, hps * RET_DV), lambda b, hh, c: (row(b, hh, c), hh)),
            pl.BlockSpec((chunk, hps * RET_DV), lambda b, hh, c: (row(b, hh, c), hb + hh)),
            pl.BlockSpec((1, hps * RET_DV), lambda b, hh, c: (0, hh)),
            pl.BlockSpec((hps, chunk, chunk), lambda b, hh, c: (hh, 0, 0)),
            pl.BlockSpec((hps, chunk, 1), lambda b, hh, c: (hh, 0, 0)),
            pl.BlockSpec((hps, chunk, 1), lambda b, hh, c: (hh, 0, 0)),
        ],
        out_specs=[
            pl.BlockSpec((chunk, hps * RET_DV), lambda b, hh, c: (row(b, hh, c), hh)),
            pl.BlockSpec((1, hps, RET_DK, RET_DV), lambda b, hh, c: (b, hh, 0, 0)),
        ],
        out_shape=[
            jax.ShapeDtypeStruct((n, h * RET_DV), bf16),
            jax.ShapeDtypeStruct((batch, h, RET_DK, RET_DV), f32),
        ],
        scratch_shapes=[pltpu.VMEM((hps, RET_DK, RET_DV), f32)],
        compiler_params=_params("parallel", "parallel", "arbitrary"),
        name="retention_prompt",
    )(qk, qk, vg, vg, gn_g, intra, qdec, kdec)


def _rope_tables(pos, hd):
    half = hd // 2
    inv = ROPE_THETA ** (-jnp.arange(half, dtype=f32) / half)
    ang = pos.astype(f32)[:, None] * inv[None, :]
    cos, sin = jnp.cos(ang), jnp.sin(ang)
    return jnp.concatenate([cos, cos], axis=1), jnp.concatenate([-sin, sin], axis=1)


def _compress_units(load_unit_row, pe, w1_ref):
    xa, xb = [], []
    for l in range(CMP_STRIDE):
        xl = load_unit_row(l)
        xa.append((xl + pe[l:l + 1]).astype(bf16))
        xb.append((xl + pe[CMP_STRIDE + l:CMP_STRIDE + l + 1]).astype(bf16))
    half = CMP_STRIDE * NSA_HD
    a = _dot(jnp.concatenate(xa, axis=1), w1_ref[0, :half])
    b = _dot(jnp.concatenate(xb, axis=1), w1_ref[0, half:])
    return a, b


def _compress_kernel(kv_ref, pe_ref, w1_ref, w2_ref, o_ref):
    nu = kv_ref.shape[0] // CMP_STRIDE
    a, b = _compress_units(lambda l: kv_ref[pl.ds(l, nu, stride=CMP_STRIDE), :], pe_ref[0], w1_ref)
    hid = a + pltpu.roll(b, nu - 1, 0)
    o_ref[0, 0, 0] = _dot(jax.nn.gelu(hid).astype(bf16), w2_ref[0]).astype(o_ref.dtype)


def compress_prompt(kvf, pe, w1, w2, batch):
    n = kvf.shape[0]
    t = n // batch
    nu = t // CMP_STRIDE
    g = NSA_KV
    return pl.pallas_call(
        _compress_kernel,
        grid=(batch, 2, g),
        in_specs=[
            pl.BlockSpec((t, NSA_HD), lambda b, s, gg: (b, s * g + gg)),
            pl.BlockSpec((1, CMP_BLOCK, NSA_HD), lambda b, s, gg: (s, 0, 0)),
            pl.BlockSpec((1, CMP_BLOCK * NSA_HD, CMP_HIDDEN), lambda b, s, gg: (s, 0, 0)),
            pl.BlockSpec((1, CMP_HIDDEN, NSA_HD), lambda b, s, gg: (s, 0, 0)),
        ],
        out_specs=pl.BlockSpec((1, 1, 1, nu, NSA_HD), lambda b, s, gg: (b, s, gg, 0, 0)),
        out_shape=jax.ShapeDtypeStruct((batch, 2, g, nu, NSA_HD), bf16),
        compiler_params=_params("parallel", "parallel", "parallel"),
        name="compress_prompt",
    )(kvf, pe, w1, w2)


def _select_blocks(sel, n_top, axis):
    ns = sel.shape[axis]
    jidx = lax.broadcasted_iota(jnp.int32, sel.shape, axis)
    chosen = jnp.zeros(sel.shape, f32)
    for _ in range(n_top):
        mx = jnp.max(sel, axis=axis, keepdims=True)
        idx = jnp.min(jnp.where(sel == mx, jidx, ns), axis=axis, keepdims=True)
        hit = jidx == idx
        chosen = jnp.where(hit, 1.0, chosen)
        sel = jnp.where(hit, -jnp.inf, sel)
    return chosen


def _masked_softmax(s, mask):
    s = jnp.where(mask, s, NEG)
    e = jnp.exp(s - jnp.max(s, axis=-1, keepdims=True))
    return jnp.where(mask, e / jnp.sum(e, axis=-1, keepdims=True), 0.0)


def _nsa_attn_kernel(q_ref, kc_ref, vc_ref, ks_ref, vs_ref, kw_ref, vw_ref, gl_ref, msel_ref, exp_ref,
                     o_ref, vs_t, vw_t, vc_t, *, n_top):
    qi = pl.program_id(2)
    blk = q_ref.shape[0]
    r_heads, hd = NSA_HPG, NSA_HD
    t_len = ks_ref.shape[0]
    nc = kc_ref.shape[3]
    ck = 4 * blk

    def transposed(ref_rows):
        return ref_rows.astype(f32).T.astype(bf16)

    @pl.when(qi == 0)
    def _():
        def tr(c, carry):
            off = pl.multiple_of(c * blk, blk)
            vs_t[0:hd, pl.ds(off, blk)] = transposed(vs_ref[pl.ds(off, blk), :])
            vw_t[0:hd, pl.ds(off, blk)] = transposed(vw_ref[pl.ds(off, blk), :])
            return carry

        lax.fori_loop(0, t_len // blk, tr, 0)
        vs_t[hd:, :] = jnp.ones((vs_t.shape[0] - hd, t_len), bf16)
        vw_t[hd:, :] = jnp.ones((vw_t.shape[0] - hd, t_len), bf16)
        for c in range(nc // blk):
            vc_t[:, c * blk:(c + 1) * blk] = transposed(vc_ref[0, 0, 0, c * blk:(c + 1) * blk, :])

    def head(x, r):
        return x[:, r * blk:(r + 1) * blk]

    def per_head(fn):
        return jnp.concatenate([fn(r) for r in range(r_heads)], axis=1)

    q = q_ref[...].astype(f32) * (NSA_SCALE * math.log2(math.e))
    q_t = per_head(lambda r: q[:, r * hd:(r + 1) * hd].T).astype(bf16)
    t_q = qi * blk + lax.broadcasted_iota(jnp.int32, (1, blk), 1)

    z = _dot(kc_ref[0, 0, 0], q_t)
    c_end = lax.broadcasted_iota(jnp.int32, (nc, 1), 0) * CMP_STRIDE + (CMP_BLOCK - 1)
    mc = c_end <= t_q
    z = per_head(lambda r: jnp.where(mc, head(z, r), NEG))
    e = jnp.exp2(z - jnp.max(z, axis=0, keepdims=True))
    p = e * (1.0 / jnp.sum(e, axis=0, keepdims=True))
    p = per_head(lambda r: jnp.where(mc, head(p, r), 0.0))
    oc_t = _dot(vc_t[...], p.astype(bf16))

    span = WINDOW + blk
    w_off = pl.multiple_of(jnp.clip(qi * blk - WINDOW, 0, t_len - span), blk)
    kpos = w_off + lax.broadcasted_iota(jnp.int32, (span, 1), 0)
    in_win = (kpos <= t_q) & (kpos > t_q - WINDOW)
    z = _dot(kw_ref[pl.ds(w_off, span), :], q_t)
    z = per_head(lambda r: jnp.where(in_win, head(z, r), NEG))
    e = jnp.exp2(z - jnp.max(z, axis=0, keepdims=True))
    pv = _dot(vw_t[:, pl.ds(w_off, span)], e.astype(bf16))
    ow_t = pv[0:hd] * (1.0 / pv[hd:hd + 1])

    imp = head(p, 0)
    for r in range(1, r_heads):
        imp = imp + head(p, r)
    sel = jnp.dot(msel_ref[...], imp, precision=lax.Precision.HIGHEST, preferred_element_type=f32)
    jidx = lax.broadcasted_iota(jnp.int32, sel.shape, 0)
    cur = lax.shift_right_logical(t_q, int(math.log2(SEL_BLOCK)))
    forced = (jidx == 0) | (jidx == cur) | (jidx == cur - 1)
    sel = jnp.where(jidx * SEL_BLOCK <= t_q, jnp.where(forced, FORCE_SCORE, sel), NEG)
    chosen = _select_blocks(sel, n_top, 0).astype(bf16)

    def flash(k_ref, vt_ref, n_chunks, chunk_of, mask_fn):
        def body(i, carry):
            m, acc = carry
            off = pl.multiple_of(chunk_of(i) * ck, ck)
            z = _dot(k_ref[pl.ds(off, ck), :], q_t)
            msk = mask_fn(off, off + lax.broadcasted_iota(jnp.int32, (ck, 1), 0))
            z = per_head(lambda r: jnp.where(msk, head(z, r), NEG))
            m_new = jnp.maximum(m, jnp.max(z, axis=0, keepdims=True))
            alpha = jnp.exp2(m - m_new)
            e = jnp.exp2(z - m_new)
            return m_new, acc * alpha + _dot(vt_ref[:, pl.ds(off, ck)], e.astype(bf16))

        width = r_heads * blk
        init = (jnp.full((1, width), NEG, f32), jnp.zeros((vt_ref.shape[0], width), f32))
        _, acc = lax.fori_loop(0, n_chunks, body, init)
        return acc[0:hd] * (1.0 / acc[hd:hd + 1])

    os_t = flash(ks_ref, vs_t, qi // (ck // blk) + 1, lambda i: i,
                 lambda off, kpos: (_dot(exp_ref[pl.ds(off, ck), :], chosen) > 0.5) & (kpos <= t_q))

    g_t = jax.nn.sigmoid(gl_ref[...]).T
    outs = []
    for r in range(r_heads):
        o_r = (g_t[3 * r:3 * r + 1] * head(oc_t, r) + g_t[3 * r + 1:3 * r + 2] * head(os_t, r)
               + g_t[3 * r + 2:3 * r + 3] * head(ow_t, r))
        outs.append(o_r.T)
    o_ref[...] = jnp.concatenate(outs, axis=1).astype(o_ref.dtype)


def _selection_constants(nu, ns, nkeys):
    c = jnp.arange(nu)[None, :]
    j = jnp.arange(ns)[:, None]
    per_sel = SEL_BLOCK // CMP_STRIDE
    rb = CMP_BLOCK // CMP_STRIDE
    msel = sum(((c + r) // per_sel == j).astype(f32) for r in range(rb)) / rb
    msel = jnp.where(c < nu - rb + 1, msel, 0.0)
    expand = (jnp.arange(nkeys)[:, None] // SEL_BLOCK == jnp.arange(ns)[None, :]).astype(bf16)
    return msel, expand


def nsa_attn_prompt(q, cmp, kvb, gl, batch):
    n = q.shape[0]
    t = n // batch
    blk = math.gcd(t, Q_BLOCK)
    nq = t // blk
    g = NSA_KV
    nu = t // CMP_STRIDE
    ns = t // SEL_BLOCK
    msel, expand = _selection_constants(nu, ns, t)
    gw = NSA_HPG * NSA_HD
    ones_rows = 16
    kv_spec = lambda slot: pl.BlockSpec((t, NSA_HD), lambda b, gg, qi: (b, slot * g + gg))
    cmp_spec = lambda s: pl.BlockSpec((1, 1, 1, nu, NSA_HD), lambda b, gg, qi: (b, s, gg, 0, 0))
    return pl.pallas_call(
        functools.partial(_nsa_attn_kernel, n_top=min(N_SEL, ns)),
        grid=(batch, g, nq),
        in_specs=[
            pl.BlockSpec((blk, gw), lambda b, gg, qi: (b * nq + qi, gg)),
            cmp_spec(0), cmp_spec(1),
            kv_spec(2), kv_spec(3), kv_spec(4), kv_spec(5),
            pl.BlockSpec((blk, LANES), lambda b, gg, qi: (b * nq + qi, gg)),
            pl.BlockSpec((ns, nu), lambda b, gg, qi: (0, 0)),
            pl.BlockSpec((t, ns), lambda b, gg, qi: (0, 0)),
        ],
        out_specs=pl.BlockSpec((blk, gw), lambda b, gg, qi: (b * nq + qi, gg)),
        out_shape=jax.ShapeDtypeStruct((n, NSA_HEADS * NSA_HD), bf16),
        scratch_shapes=[pltpu.VMEM((NSA_HD + ones_rows, t), bf16), pltpu.VMEM((NSA_HD + ones_rows, t), bf16),
                        pltpu.VMEM((NSA_HD, nu), bf16)],
        compiler_params=_params("parallel", "parallel", "arbitrary"),
        name="nsa_attn_prompt",
    )(q, cmp, cmp, kvb, kvb, kvb, kvb, gl, msel, expand)


def _top2(vals, lane):
    width = vals.shape[-1]
    m1 = jnp.max(vals, axis=-1, keepdims=True)
    i1 = jnp.min(jnp.where(vals == m1, lane, width), axis=-1, keepdims=True)
    rest = jnp.where(lane == i1, -2.0, vals)
    m2 = jnp.max(rest, axis=-1, keepdims=True)
    i2 = jnp.min(jnp.where(rest == m2, lane, width), axis=-1, keepdims=True)
    return m1, i1, m2, i2


def _router_kernel(x_ref, w_ref, b_ref, info_ref, cnt_ref, carry):
    @pl.when(pl.program_id(0) == 0)
    def _():
        carry[...] = jnp.zeros_like(carry)

    logits = jnp.dot(x_ref[...], w_ref[...], precision=lax.Precision.HIGHEST, preferred_element_type=f32)
    logits = logits + b_ref[...]
    e = jnp.exp(logits - jnp.max(logits, axis=-1, keepdims=True))
    aff = e / jnp.sum(e, axis=-1, keepdims=True)
    tm = aff.shape[0]
    lane = lax.broadcasted_iota(jnp.int32, aff.shape, 1)
    lane_grp = lax.shift_right_logical(lane, int(math.log2(EXPERTS_PER_GROUP)))

    best, grp = None, None
    for gidx in range(N_GROUPS):
        m1, _, m2, _ = _top2(jnp.where(lane_grp == gidx, aff, -1.0), lane)
        score = m1 + m2
        if gidx == 0:
            best, grp = score, jnp.zeros_like(lane[:, :1])
        else:
            better = score > best
            grp = jnp.where(better, gidx, grp)
            best = jnp.where(better, score, best)
    m1, i1, m2, i2 = _top2(jnp.where(lane_grp == grp, aff, -1.0), lane)
    den = m1 + m2

    hot1 = (lane == i1).astype(f32)
    hot2 = (lane == i2).astype(f32)
    both = hot1 + hot2
    row = lax.broadcasted_iota(jnp.int32, (tm, tm), 0)
    col = lax.broadcasted_iota(jnp.int32, (tm, tm), 1)
    before = _dot((col < row).astype(bf16), both.astype(bf16)) + carry[...]
    rank1 = jnp.sum(hot1 * before, axis=-1, keepdims=True)
    rank2 = jnp.sum(hot2 * before, axis=-1, keepdims=True)
    carry[...] += jnp.sum(both, axis=0, keepdims=True)
    cnt_ref[...] = carry[...]

    cols = (i1.astype(f32), i2.astype(f32), m1 / den, m2 / den, rank1, rank2)
    info = jnp.zeros(aff.shape, f32)
    for c, v in enumerate(cols):
        info = jnp.where(lane == c, v, info)
    info_ref[...] = info


def moe_router(x, router_w, router_b, *, tm):
    n, d = x.shape
    tm = min(tm, n)
    assert n % tm == 0
    return pl.pallas_call(
        _router_kernel,
        grid=(n // tm,),
        in_specs=[
            pl.BlockSpec((tm, d), lambda i: (i, 0)),
            pl.BlockSpec((d, N_EXPERTS), lambda i: (0, 0)),
            pl.BlockSpec((1, N_EXPERTS), lambda i: (0, 0)),
        ],
        out_specs=[pl.BlockSpec((tm, N_EXPERTS), lambda i: (i, 0)), pl.BlockSpec((1, N_EXPERTS), lambda i: (0, 0))],
        out_shape=[jax.ShapeDtypeStruct((n, N_EXPERTS), f32), jax.ShapeDtypeStruct((1, N_EXPERTS), f32)],
        scratch_shapes=[pltpu.VMEM((1, N_EXPERTS), f32)],
        compiler_params=_params("arbitrary"),
        name="moe_router",
    )(x, router_w, router_b)


def _row_gather(src_hbm, dst, sem, index_of, n_rows, straight_line=False):
    def body(r, carry):
        pltpu.make_async_copy(src_hbm.at[pl.ds(index_of(r), 1), :], dst.at[pl.ds(r, 1), :], sem).start()
        return carry

    if straight_line:
        for r in range(n_rows):
            body(r, 0)
    else:
        lax.fori_loop(0, n_rows, body, 0, unroll=8)


def _row_gather_wait(src_hbm, dst, sem):
    pltpu.make_async_copy(src_hbm.at[pl.ds(0, dst.shape[0]), :], dst, sem).wait()


def _moe_kernel(te_ref, nt_ref, src_ref, x_hbm, wg_ref, wu_ref, wd_ref, o_ref, xbuf, sem):
    t = pl.program_id(0)
    n_live = nt_ref[0]
    tm = xbuf.shape[1]

    @pl.when(t == 0)
    def _():
        _row_gather(x_hbm, xbuf.at[0], sem.at[0], lambda r: src_ref[r], tm)

    @pl.when(t < n_live)
    def _():
        slot = t % 2
        _row_gather_wait(x_hbm, xbuf.at[slot], sem.at[slot])
        nxt = jnp.minimum(t + 1, n_live - 1)
        _row_gather(x_hbm, xbuf.at[1 - slot], sem.at[1 - slot], lambda r: src_ref[nxt * tm + r], tm,
                    straight_line=True)
        x = xbuf[slot].astype(bf16)
        h = jax.nn.silu(_dot(x, wg_ref[0])) * _dot(x, wu_ref[0])
        o_ref[...] = _dot(h.astype(bf16), wd_ref[0])

    @pl.when(t == n_live)
    def _():
        _row_gather_wait(x_hbm, xbuf.at[t % 2], sem.at[t % 2])

    @pl.when(t >= n_live)
    def _():
        o_ref[...] = jnp.zeros_like(o_ref)


def moe_experts(x, src, tile_expert, n_live, w_gate, w_up, w_down, e_base, *, tm):
    p = src.shape[0]
    d = x.shape[1]
    fdim = w_gate.shape[2]
    grid_spec = pltpu.PrefetchScalarGridSpec(
        num_scalar_prefetch=3,
        grid=(p // tm,),
        in_specs=[
            pl.BlockSpec(memory_space=pl.ANY),
            pl.BlockSpec((1, d, fdim), lambda i, te, nt, sr: (e_base + te[i], 0, 0)),
            pl.BlockSpec((1, d, fdim), lambda i, te, nt, sr: (e_base + te[i], 0, 0)),
            pl.BlockSpec((1, fdim, d), lambda i, te, nt, sr: (e_base + te[i], 0, 0)),
        ],
        out_specs=pl.BlockSpec((tm, d), lambda i, te, nt, sr: (i, 0)),
        scratch_shapes=[pltpu.VMEM((2, tm, d), f32), pltpu.SemaphoreType.DMA((2,))],
    )
    return pl.pallas_call(
        _moe_kernel,
        grid_spec=grid_spec,
        out_shape=jax.ShapeDtypeStruct((p, d), f32),
        compiler_params=_params("arbitrary"),
        name="moe_experts",
    )(tile_expert, n_live, src, x, w_gate, w_up, w_down)


def moe_layer(x, router_w, router_b, w_gate, w_up, w_down, e_base, *, tm):
    n = x.shape[0]
    info, counts = moe_router(x, router_w, router_b, tm=512)
    e1, e2 = info[:, 0].astype(jnp.int32), info[:, 1].astype(jnp.int32)
    r1, r2 = info[:, 4].astype(jnp.int32), info[:, 5].astype(jnp.int32)
    cnt = counts[0].astype(jnp.int32)
    padded = (cnt + tm - 1) // tm * tm
    ends = jnp.cumsum(padded)
    starts = ends - padded
    d1, d2 = starts[e1] + r1, starts[e2] + r2
    n_tiles = -(-2 * n // tm) + N_EXPERTS + 1
    p = n_tiles * tm
    tok = jnp.arange(n, dtype=jnp.int32)
    src = jnp.zeros((p,), jnp.int32).at[jnp.concatenate([d1, d2])].set(jnp.concatenate([tok, tok]))
    n_live = ends[-1:] // tm
    tile_ix = jnp.arange(n_tiles, dtype=jnp.int32)
    tile_start = jnp.minimum(tile_ix, n_live[0] - 1) * tm
    tile_expert = jnp.sum((ends[None, :] <= tile_start[:, None]).astype(jnp.int32), axis=1)
    ys = moe_experts(x, src, tile_expert, n_live.astype(jnp.int32), w_gate, w_up, w_down, e_base, tm=tm)
    return ys, d1, d2, info


def _ln_ple_kernel(d1_ref, d2_ref, x_ref, info_ref, ys_hbm, g_ref, b_ref, p_ref, wg_ref, wp_ref, o_ref, ybuf, sem):
    i = pl.program_id(0)
    tm = x_ref.shape[0]

    last = pl.num_programs(0) - 1

    def start(tile, slot, straight_line):
        _row_gather(ys_hbm, ybuf.at[slot, 0], sem.at[slot], lambda r: d1_ref[tile * tm + r], tm, straight_line)
        _row_gather(ys_hbm, ybuf.at[slot, 1], sem.at[slot], lambda r: d2_ref[tile * tm + r], tm, straight_line)

    def wait(slot):
        _row_gather_wait(ys_hbm, ybuf.at[slot, 0], sem.at[slot])
        _row_gather_wait(ys_hbm, ybuf.at[slot, 1], sem.at[slot])

    @pl.when(i == 0)
    def _():
        start(0, 0, False)

    slot = i % 2
    wait(slot)
    start(jnp.minimum(i + 1, last), 1 - slot, True)
    info = info_ref[...]
    y = info[:, 2:3] * ybuf[slot, 0] + info[:, 3:4] * ybuf[slot, 1]
    x2 = _layer_norm(DN_ALPHA * x_ref[...] + y, g_ref[...], b_ref[...])
    gate = jax.nn.sigmoid(_dot(x2.astype(bf16), wg_ref[...]))
    o_ref[...] = x2 + gate * _dot(p_ref[...].astype(bf16), wp_ref[...])

    @pl.when(i == last)
    def _():
        wait(1 - slot)


def ln_ple(x, ys, d1, d2, info, g, b, p, w_gate, w_proj, *, tm):
    n, d = x.shape
    tm = min(tm, n)
    assert n % tm == 0
    pd = p.shape[1]
    row = lambda i, a, c: (i, 0)
    fixed = lambda i, a, c: (0, 0)
    grid_spec = pltpu.PrefetchScalarGridSpec(
        num_scalar_prefetch=2,
        grid=(n // tm,),
        in_specs=[
            pl.BlockSpec((tm, d), row), pl.BlockSpec((tm, info.shape[1]), row), pl.BlockSpec(memory_space=pl.ANY),
            pl.BlockSpec((1, d), fixed), pl.BlockSpec((1, d), fixed),
            pl.BlockSpec((tm, pd), row), pl.BlockSpec((d, d), fixed), pl.BlockSpec((pd, d), fixed),
        ],
        out_specs=pl.BlockSpec((tm, d), row),
        scratch_shapes=[pltpu.VMEM((2, 2, tm, d), f32), pltpu.SemaphoreType.DMA((2,))],
    )
    return pl.pallas_call(
        _ln_ple_kernel,
        grid_spec=grid_spec,
        out_shape=jax.ShapeDtypeStruct((n, d), f32),
        compiler_params=_params("arbitrary"),
        name="ln_ple",
    )(d1, d2, x, info, ys, g, b, p, w_gate, w_proj)


def _retention_decode_kernel(q_ref, k_ref, v_ref, g_ref, gn_ref, dec_ref, s_ref, o_ref, so_ref):
    q = q_ref[0]
    k = k_ref[0] * (RET_DK ** -0.5)
    v = v_ref[0]
    gamma = dec_ref[0]
    s_prev = s_ref[0, 0]
    att = jnp.sum(q.astype(f32) * k.astype(f32), axis=-1, keepdims=True)
    rows = 8
    q8 = jnp.broadcast_to(q, (rows, RET_DK))
    first = (lax.broadcasted_iota(jnp.int32, (rows, 1), 0) == 0).astype(f32)
    k8 = (jnp.broadcast_to(k.astype(f32), (rows, RET_DK)) * first).astype(bf16)
    v8 = jnp.broadcast_to(v, (rows, RET_DV))
    o = att.astype(bf16).astype(f32) * v.astype(f32) + _dot(q8, s_prev.astype(bf16))[:1] * gamma
    so_ref[0, 0] = s_prev * gamma + _dot_tn(k8, v8)
    mu = jnp.mean(o, axis=-1, keepdims=True)
    d = o - mu
    var = jnp.mean(d * d, axis=-1, keepdims=True)
    o = d * lax.rsqrt(var + LN_EPS) * gn_ref[...]
    o_ref[0] = (jax.nn.silu(g_ref[0].astype(f32)) * o).astype(o_ref.dtype)


def retention_decode(qk, vg, gn_g, states, base):
    b = qk.shape[0]
    h = RET_HEADS
    _, qdec, _ = _retention_decay_tables(1)
    qk3, vg3 = qk[:, None, :], vg[:, None, :]
    o, s = pl.pallas_call(
        _retention_decode_kernel,
        grid=(b, h),
        in_specs=[
            pl.BlockSpec((1, 1, RET_DK), lambda i, hh: (i, 0, hh)),
            pl.BlockSpec((1, 1, RET_DK), lambda i, hh: (i, 0, h + hh)),
            pl.BlockSpec((1, 1, RET_DV), lambda i, hh: (i, 0, hh)),
            pl.BlockSpec((1, 1, RET_DV), lambda i, hh: (i, 0, h + hh)),
            pl.BlockSpec((1, RET_DV), lambda i, hh: (0, hh)),
            pl.BlockSpec((1, 1, 1), lambda i, hh: (hh, 0, 0)),
            pl.BlockSpec((1, 1, RET_DK, RET_DV), lambda i, hh: (base + i, hh, 0, 0)),
        ],
        out_specs=[
            pl.BlockSpec((1, 1, RET_DV), lambda i, hh: (i, 0, hh)),
            pl.BlockSpec((1, 1, RET_DK, RET_DV), lambda i, hh: (i, hh, 0, 0)),
        ],
        out_shape=[jax.ShapeDtypeStruct((b, 1, h * RET_DV), bf16), jax.ShapeDtypeStruct((b, h, RET_DK, RET_DV), f32)],
        compiler_params=_params("parallel", "parallel"),
        name="retention_decode",
    )(qk3, qk3, vg3, vg3, gn_g, qdec, states)
    return o[:, 0, :], s


HIST_PAGES_PER_STEP = 32


def _compress_hist_kernel(pt_ref, *refs, n_pages):
    page_refs = refs[:n_pages]
    pe_ref, w1_ref, w2_ref, o_ref, a_buf, o_buf = refs[n_pages:]
    g = NSA_KV
    upp = page_refs[0].shape[1] // CMP_STRIDE
    rows = n_pages * upp * g
    pad = a_buf.shape[0] - rows

    def load(l):
        return jnp.concatenate([pr[0, pl.ds(l, upp, stride=CMP_STRIDE), :, :].reshape(upp * g, NSA_HD)
                                for pr in page_refs], axis=0)

    a, b = _compress_units(load, pe_ref[0], w1_ref)

    @pl.when(pl.program_id(2) == 0)
    def _():
        a_buf[0:pad, :] = jnp.zeros((pad, a_buf.shape[1]), f32)

    @pl.when(pl.program_id(2) > 0)
    def _():
        a_buf[0:pad, :] = a_buf[rows:rows + pad, :]

    a_buf[pad:pad + rows, :] = a
    a_prev = a_buf[pad - g:pad - g + rows, :]
    o_buf[...] = _dot(jax.nn.gelu(a_prev + b).astype(bf16), w2_ref[0])
    for gg in range(g):
        o_ref[0, 0, gg] = o_buf[pl.ds(gg, rows // g, stride=g), :].astype(o_ref.dtype)


def compress_history(pool, page_table, pe, w1, w2):
    b, ppb = page_table.shape
    page = pool.shape[1]
    g = NSA_KV
    n_pages = min(HIST_PAGES_PER_STEP, ppb)
    assert ppb % n_pages == 0 and page % CMP_STRIDE == 0
    upp = page // CMP_STRIDE
    seg = n_pages * upp
    nu = ppb * upp
    sublanes = 8
    page_spec = lambda k: pl.BlockSpec(
        (1, page, None, g, NSA_HD), lambda i, s, ch, pt: (pt[i, ch * n_pages + k], 0, s, 0, 0))
    grid_spec = pltpu.PrefetchScalarGridSpec(
        num_scalar_prefetch=1,
        grid=(b, 2, ppb // n_pages),
        in_specs=[page_spec(k) for k in range(n_pages)] + [
            pl.BlockSpec((1, CMP_BLOCK, NSA_HD), lambda i, s, ch, pt: (s, 0, 0)),
            pl.BlockSpec((1, CMP_BLOCK * NSA_HD, CMP_HIDDEN), lambda i, s, ch, pt: (s, 0, 0)),
            pl.BlockSpec((1, CMP_HIDDEN, NSA_HD), lambda i, s, ch, pt: (s, 0, 0)),
        ],
        out_specs=pl.BlockSpec((1, 1, g, seg, NSA_HD), lambda i, s, ch, pt: (i, s, 0, ch, 0)),
        scratch_shapes=[pltpu.VMEM((seg * g + sublanes, CMP_HIDDEN), f32), pltpu.VMEM((seg * g, NSA_HD), f32)],
    )
    return pl.pallas_call(
        functools.partial(_compress_hist_kernel, n_pages=n_pages),
        grid_spec=grid_spec,
        out_shape=jax.ShapeDtypeStruct((b, 2, g, nu, NSA_HD), bf16),
        compiler_params=_params("parallel", "parallel", "arbitrary"),
        name="compress_history",
    )(page_table, *([pool] * n_pages), pe, w1, w2)


def _nsa_select_decode_kernel(q_ref, kc_ref, vc_ref, msel_ref, oc_ref, top_ref, *, t, ns, n_top):
    g, r_heads, hd = NSA_KV, NSA_HPG, NSA_HD
    q = q_ref[0]
    nu = kc_ref.shape[3]
    u = lax.broadcasted_iota(jnp.int32, (1, nu), 1)
    visible = (u >= 1) & (u * CMP_STRIDE + (CMP_STRIDE - 1) <= t)
    imps = []
    for gg in range(g):
        heads = [q[:, (gg * r_heads + r) * hd:(gg * r_heads + r + 1) * hd] for r in range(r_heads)]
        qb = jnp.concatenate(heads, axis=0)
        p = _masked_softmax(_dot_nt(qb, kc_ref[0, 0, gg]) * NSA_SCALE, visible)
        oc_ref[0, gg] = _dot(p.astype(bf16), vc_ref[0, 0, gg])
        imps.append(jnp.sum(p, axis=0, keepdims=True))
    imp = jnp.concatenate(imps, axis=0)
    sel = jnp.dot(imp, msel_ref[...], precision=lax.Precision.HIGHEST, preferred_element_type=f32)
    jidx = lax.broadcasted_iota(jnp.int32, sel.shape, 1)
    cur = t // SEL_BLOCK
    forced = (jidx == 0) | (jidx == cur) | (jidx == cur - 1)
    sel = jnp.where(jidx * SEL_BLOCK <= t, jnp.where(forced, FORCE_SCORE, sel), NEG)
    sel = jnp.where(jidx < ns, sel, -jnp.inf)
    width = sel.shape[-1]
    lane = lax.broadcasted_iota(jnp.int32, (g, top_ref.shape[-1]), 1)
    top = jnp.zeros(lane.shape, jnp.int32)
    for it in range(n_top):
        mx = jnp.max(sel, axis=-1, keepdims=True)
        idx = jnp.min(jnp.where(sel == mx, jidx, width), axis=-1, keepdims=True)
        top = jnp.where(lane == it, idx, top)
        sel = jnp.where(jidx == idx, -jnp.inf, sel)
    top_ref[0] = top


def nsa_select_decode(q, cmp, t):
    b = q.shape[0]
    g = NSA_KV
    nu = cmp.shape[3]
    ns = (t + 1 + SEL_BLOCK - 1) // SEL_BLOCK
    ns_pad = -(-ns // LANES) * LANES
    per_sel = SEL_BLOCK // CMP_STRIDE
    u = jnp.arange(nu)[:, None]
    j = jnp.arange(ns_pad)[None, :]
    rb = CMP_BLOCK // CMP_STRIDE
    msel = sum(((u - 1 + r) // per_sel == j).astype(f32) for r in range(rb)) / rb
    msel = jnp.where(u >= 1, msel, 0.0)
    gw = NSA_HPG * NSA_HD
    n_top = min(N_SEL, ns)
    cmp_spec = lambda s: pl.BlockSpec((1, 1, g, nu, NSA_HD), lambda i: (i, s, 0, 0, 0))
    return pl.pallas_call(
        functools.partial(_nsa_select_decode_kernel, t=t, ns=ns, n_top=n_top),
        grid=(b,),
        in_specs=[
            pl.BlockSpec((1, 1, g * gw), lambda i: (i, 0, 0)),
            cmp_spec(0), cmp_spec(1),
            pl.BlockSpec((nu, ns_pad), lambda i: (0, 0)),
        ],
        out_specs=[
            pl.BlockSpec((1, g, NSA_HPG, NSA_HD), lambda i: (i, 0, 0, 0)),
            pl.BlockSpec((1, g, LANES), lambda i: (i, 0, 0)),
        ],
        out_shape=[jax.ShapeDtypeStruct((b, g, NSA_HPG, NSA_HD), f32), jax.ShapeDtypeStruct((b, g, LANES), jnp.int32)],
        compiler_params=_params("parallel"),
        name="nsa_select_decode",
    )(q[:, None, :], cmp, cmp, msel)


def _nsa_attn_decode_kernel(pt_ref, top_ref, *refs, n_top, n_hist_blocks):
    k_refs, v_refs = refs[:n_top], refs[n_top:2 * n_top]
    (q_ref, oc_ref, knew_ref, vnew_ref, kw_ref, vw_ref, kwnew_ref, vwnew_ref, gl_ref, o_ref) = refs[2 * n_top:]
    i, gg = pl.program_id(0), pl.program_id(1)
    r_heads, hd = NSA_HPG, NSA_HD
    q = q_ref[0]
    qb = jnp.concatenate([q[:, r * hd:(r + 1) * hd] for r in range(r_heads)], axis=0)

    def attend(keys, vals, mask, k_new, v_new):
        s = jnp.where(mask, _dot_nt(qb, keys) * NSA_SCALE, NEG)
        s_new = jnp.sum(qb.astype(f32) * k_new.astype(f32), axis=-1, keepdims=True) * NSA_SCALE
        m = jnp.maximum(jnp.max(s, axis=-1, keepdims=True), s_new)
        e = jnp.where(mask, jnp.exp(s - m), 0.0)
        e_new = jnp.exp(s_new - m)
        num = _dot(e.astype(bf16), vals) + e_new.astype(bf16).astype(f32) * v_new.astype(f32)
        return num / (jnp.sum(e, axis=-1, keepdims=True) + e_new)

    g = NSA_KV

    def rows_of(ref):
        v = ref[0]
        return v.reshape(v.shape[0] * g, hd).astype(bf16)

    def own_group(n_rows):
        return lax.broadcasted_iota(jnp.int32, (1, n_rows * g), 1) % g == gg

    sb = k_refs[0].shape[1]
    keys = jnp.concatenate([rows_of(r) for r in k_refs], axis=0)
    vals = jnp.concatenate([rows_of(r) for r in v_refs], axis=0)
    blk_of_lane = lax.broadcasted_iota(jnp.int32, (1, n_top * sb * g), 1) // (sb * g)
    sel_of_lane = jnp.zeros((1, n_top * sb * g), jnp.int32)
    for n in range(n_top):
        sel_of_lane = jnp.where(blk_of_lane == n, top_ref[i, gg, n], sel_of_lane)
    o_s = attend(keys, vals, (sel_of_lane < n_hist_blocks) & own_group(n_top * sb), knew_ref[0], vnew_ref[0])

    wlen = kw_ref.shape[1]
    wmask = (lax.broadcasted_iota(jnp.int32, (1, wlen * g), 1) >= g) & own_group(wlen)
    o_w = attend(rows_of(kw_ref), rows_of(vw_ref), wmask, kwnew_ref[0], vwnew_ref[0])

    gates = jax.nn.sigmoid(gl_ref[0])
    o_c = oc_ref[0, 0]
    outs = []
    for r in range(r_heads):
        outs.append(gates[:, 3 * r:3 * r + 1] * o_c[r:r + 1] + gates[:, 3 * r + 1:3 * r + 2] * o_s[r:r + 1]
                    + gates[:, 3 * r + 2:3 * r + 3] * o_w[r:r + 1])
    o_ref[0] = jnp.concatenate(outs, axis=1).astype(o_ref.dtype)


def nsa_attn_decode(q, o_c, top, kvb, gl, pool, page_table, win, win_base, t):
    b = q.shape[0]
    g = NSA_KV
    n_top = top.shape[-1]
    page = pool.shape[1]
    assert t % SEL_BLOCK == 0 and page % SEL_BLOCK == 0 and win.shape[1] == WINDOW
    n_hist_blocks = t // SEL_BLOCK
    bpp = page // SEL_BLOCK
    gw = NSA_HPG * NSA_HD

    def blk_spec(n, slot):
        def imap(i, gg, pt, tp):
            j = jnp.minimum(tp[i, gg, n], n_hist_blocks - 1)
            return (pt[i, j // bpp], j % bpp, slot, 0, 0)
        return pl.BlockSpec((1, SEL_BLOCK, None, g, NSA_HD), imap)

    new_spec = lambda slot: pl.BlockSpec((1, 1, NSA_HD), lambda i, gg, pt, tp: (i, 0, slot * g + gg))
    win_spec = lambda slot: pl.BlockSpec((1, WINDOW, None, g, NSA_HD),
                                         lambda i, gg, pt, tp: (win_base + i, 0, slot, 0, 0))
    kvb3 = kvb[:, None, :]
    grid_spec = pltpu.PrefetchScalarGridSpec(
        num_scalar_prefetch=2,
        grid=(b, g),
        in_specs=[blk_spec(n, 2) for n in range(n_top)] + [blk_spec(n, 3) for n in range(n_top)] + [
            pl.BlockSpec((1, 1, gw), lambda i, gg, pt, tp: (i, 0, gg)),
            pl.BlockSpec((1, 1, NSA_HPG, NSA_HD), lambda i, gg, pt, tp: (i, gg, 0, 0)),
            new_spec(2), new_spec(3), win_spec(0), win_spec(1), new_spec(4), new_spec(5),
            pl.BlockSpec((1, 1, LANES), lambda i, gg, pt, tp: (i, 0, gg)),
        ],
        out_specs=pl.BlockSpec((1, 1, gw), lambda i, gg, pt, tp: (i, 0, gg)),
    )
    o = pl.pallas_call(
        functools.partial(_nsa_attn_decode_kernel, n_top=n_top, n_hist_blocks=n_hist_blocks),
        grid_spec=grid_spec,
        out_shape=jax.ShapeDtypeStruct((b, 1, NSA_HEADS * NSA_HD), bf16),
        compiler_params=_params("parallel", "parallel"),
        name="nsa_attn_decode",
    )(page_table, top, *([pool] * (2 * n_top)), q[:, None, :], o_c, kvb3, kvb3, win, win, kvb3, kvb3,
      gl[:, None, :])
    return o[:, 0, :]


def _retention_mixer(x, pos, batch, w, states, base):
    cos, sin = _rope_tables(pos, RET_DK)
    tn = PROJ_TILE
    n_qk = 2 * RET_HEADS * RET_DK
    flags = jnp.ones((n_qk // tn,), jnp.int32)
    qk, = mm_rope(x, w["in"], (0, n_qk), cos, sin, flags, [bf16], hd=RET_DK, tm=PROJ_TILE, tn=tn)
    vg = mm(x, w["in"], (n_qk, 2 * RET_HEADS * RET_DV), bf16, tm=PROJ_TILE, tn=PROJ_TILE)
    if states is None:
        return retention_prompt(qk, vg, w["gn"], batch)
    return retention_decode(qk, vg, w["gn"], states, base)


def _nsa_projections(x, pos, w):
    cos, sin = _rope_tables(pos, NSA_HD)
    tn = NSA_KV * NSA_HD
    n_q = NSA_HEADS * NSA_HD
    n_kv = 6 * NSA_KV * NSA_HD
    q, = mm_rope(x, w["in"], (0, n_q), cos, sin, jnp.ones((n_q // tn,), jnp.int32), [bf16],
                 hd=NSA_HD, tm=PROJ_TILE, tn=tn)
    kvf, kvb = mm_rope(x, w["in"], (n_q, n_kv), cos, sin, jnp.array([1, 0] * 3, jnp.int32), [f32, bf16],
                       hd=NSA_HD, tm=PROJ_TILE, tn=tn)
    gl = mm(x, w["gl"], (0, w["gl"].shape[1]), f32, tm=PROJ_TILE, tn=tn)
    return q, kvf, kvb, gl


def _layer_tail(x, h, p, w, tm_moe):
    x1 = mm_ln(h, w["out"], x, w["ln_g"][0:1], w["ln_b"][0:1], tm=512, tk=2048)
    ys, d1, d2, info = moe_layer(x1, w["router_w"], w["router_b"], w["moe_gate"], w["moe_up"], w["moe_down"],
                                 w["moe_base"], tm=tm_moe)
    return ln_ple(x1, ys, d1, d2, info, w["ln_g"][1:2], w["ln_b"][1:2], p, w["ple_gate"], w["ple_proj"], tm=256)


def kernel(x_prompt, x_sample, state_ret, cache_nsa_kv, state_nsa_win, page_table, p_prompt, p_sample, ret_w_in, ret_w_out, ret_gn_g, nsa_w_in, nsa_w_out, nsa_cmp_pos, nsa_cmp_w1, nsa_cmp_w2, ln_g, ln_b, router_w, router_b, moe_w_gate, moe_w_up, moe_w_down, ple_w_gate, ple_w_proj):
    bp, tp, d = x_prompt.shape
    bs, ts, _ = x_sample.shape
    assert ts == 1
    n_pool, page = cache_nsa_kv.shape[1], cache_nsa_kv.shape[2]
    past = page_table.shape[1] * page
    g, hd = NSA_KV, NSA_HD
    kv_cols = N_KV_SLOTS * g * hd

    xp = x_prompt.reshape(bp * tp, d)
    xs = x_sample.reshape(bs * ts, d)
    pos_p = jnp.arange(tp, dtype=jnp.int32)
    pos_s = jnp.full((bs,), past, jnp.int32)
    states = state_ret.reshape((-1,) + state_ret.shape[2:])
    pool = cache_nsa_kv.reshape((-1,) + cache_nsa_kv.shape[2:])
    wins = state_nsa_win.reshape((-1,) + state_nsa_win.shape[2:])

    moe_gate = moe_w_gate.astype(bf16).reshape((-1,) + moe_w_gate.shape[2:])
    moe_up = moe_w_up.astype(bf16).reshape((-1,) + moe_w_up.shape[2:])
    moe_down = moe_w_down.astype(bf16).reshape((-1,) + moe_w_down.shape[2:])

    ret_p, ret_s, kv_p, kv_s, win_p, win_s = [], [], [], [], [], []
    for i in range(DEPTH):
        j = i // 2
        w = {
            "ln_g": ln_g[i], "ln_b": ln_b[i],
            "router_w": router_w, "router_b": router_b[None, :],
            "moe_gate": moe_gate, "moe_up": moe_up, "moe_down": moe_down, "moe_base": i * N_EXPERTS,
            "ple_gate": ple_w_gate[i].astype(bf16), "ple_proj": ple_w_proj[i].astype(bf16),
        }
        if i % 2 == 0:
            w.update({"in": ret_w_in[j].astype(bf16), "out": ret_w_out[j].astype(bf16), "gn": ret_gn_g[j][None, :]})
            hp, sp = _retention_mixer(xp, pos_p, bp, w, None, 0)
            hs, ss = _retention_mixer(xs, pos_s, bs, w, states, j * bs)
            ret_p.append(sp)
            ret_s.append(ss)
        else:
            nq = NSA_HEADS * hd
            w_in = nsa_w_in[j]
            gl = w_in[:, nq + 6 * g * hd:].reshape(d, g, NSA_HPG * 3)
            gl = jnp.pad(gl, ((0, 0), (0, 0), (0, LANES - NSA_HPG * 3))).reshape(d, g * LANES)
            w.update({"in": w_in.astype(bf16), "gl": gl.astype(bf16), "out": nsa_w_out[j].astype(bf16)})
            pe = nsa_cmp_pos[j]
            w1 = nsa_cmp_w1[j].reshape(2, CMP_BLOCK * hd, CMP_HIDDEN).astype(bf16)
            w2 = nsa_cmp_w2[j].astype(bf16)
            q, kvf, kvb, glp = _nsa_projections(xp, pos_p, w)
            cmp = compress_prompt(kvf, pe, w1, w2, bp)
            hp = nsa_attn_prompt(q, cmp, kvb, glp, bp)
            kv_p.append(kvf[:, :kv_cols].reshape(bp, tp, N_KV_SLOTS, g, hd))
            keep = min(WINDOW, tp)
            win_p.append(kvf.reshape(bp, tp, -1)[:, tp - keep:, kv_cols:].reshape(bp, keep, 2, g, hd))
            q, kvf, kvb, gls = _nsa_projections(xs, pos_s, w)
            pt = page_table + j * n_pool
            cmp = compress_history(pool, pt, pe, w1, w2)
            o_c, top = nsa_select_decode(q, cmp, past)
            top = top[:, :, :min(N_SEL, past // SEL_BLOCK + 1)]
            hs = nsa_attn_decode(q, o_c, top, kvb, gls, pool, pt, wins, j * bs, past)
            kv_s.append(kvf[:, :kv_cols].reshape(bs, ts, N_KV_SLOTS, g, hd))
            new_win = kvf[:, kv_cols:].reshape(bs, ts, 2, g, hd)
            win_s.append(jnp.concatenate([state_nsa_win[j][:, ts:], new_win], axis=1))
        xp = _layer_tail(xp, hp, p_prompt[i].reshape(bp * tp, -1), w, MOE_TILE)
        xs = _layer_tail(xs, hs, p_sample[i].reshape(bs * ts, -1), w, 16)

    return (xp.reshape(bp, tp, d), xs.reshape(bs, ts, d), jnp.stack(ret_p), jnp.stack(ret_s),
            jnp.stack(kv_p), jnp.stack(kv_s), jnp.stack(win_p), jnp.stack(win_s))
```

```python
import functools
import math

import jax
import jax.numpy as jnp
from jax import lax
from jax.experimental import pallas as pl
from jax.experimental.pallas import tpu as pltpu

f32 = jnp.float32
bf16 = jnp.bfloat16

D_MODEL = 2048
DEPTH = 4
RET_HEADS = 8
RET_DK = D_MODEL // RET_HEADS
RET_DV = 2 * RET_DK
RET_CHUNK = 128
NSA_HEADS = 16
NSA_HD = D_MODEL // NSA_HEADS
NSA_KV = 4
NSA_HPG = NSA_HEADS // NSA_KV
CMP_BLOCK = 32
CMP_STRIDE = 16
CMP_HIDDEN = 4 * NSA_HD
SEL_BLOCK = 64
N_SEL = 16
WINDOW = 512
Q_BLOCK = 128
N_KV_SLOTS = 4
N_EXPERTS = 16
N_GROUPS = 4
EXPERTS_PER_GROUP = N_EXPERTS // N_GROUPS
D_EXPERT = 1408
PLE_DIM = 256
ROPE_THETA = 10000.0
LN_EPS = 1e-5
DN_ALPHA = (2 * DEPTH) ** 0.25
NEG = -1e30
FORCE_SCORE = float(NSA_HPG + 1)
NSA_SCALE = NSA_HD ** -0.5

V7X_VMEM_BYTES = 64 * 1024 * 1024
VMEM_LIMIT = V7X_VMEM_BYTES - 8 * 1024 * 1024
LANES = 128
MOE_TILE = 256
PROJ_TILE = 1024
RET_HEADS_PER_STEP = 8


def _params(*sem):
    return pltpu.CompilerParams(dimension_semantics=sem, vmem_limit_bytes=VMEM_LIMIT)


def _layer_norm(v, g, b):
    mu = jnp.mean(v, axis=-1, keepdims=True)
    d = v - mu
    var = jnp.mean(d * d, axis=-1, keepdims=True)
    return d * lax.rsqrt(var + LN_EPS) * g + b


def _dot(a, b):
    return jnp.dot(a, b, preferred_element_type=f32)


def _dot_nt(a, b):
    return lax.dot_general(a, b, (((1,), (1,)), ((), ())), preferred_element_type=f32)


def _dot_tn(a, b):
    return lax.dot_general(a, b, (((0,), (0,)), ((), ())), preferred_element_type=f32)


def _rotate_half(v, hd):
    pieces = []
    for c in range(0, v.shape[1], hd):
        if hd == 2 * LANES:
            pieces += [v[:, c + LANES:c + hd], v[:, c:c + LANES]]
        else:
            pieces.append(pltpu.roll(v[:, c:c + hd], hd // 2, 1))
    return jnp.concatenate(pieces, axis=1)


def _mm_rope_kernel(flags_ref, x_ref, w_ref, cos_ref, sin_ref, *o_refs, hd):
    acc = _dot(x_ref[...].astype(bf16), w_ref[...])
    roped = flags_ref[pl.program_id(1)] == 1

    @pl.when(roped)
    def _():
        reps = acc.shape[1] // hd
        cos = jnp.concatenate([cos_ref[...]] * reps, axis=1)
        sin = jnp.concatenate([sin_ref[...]] * reps, axis=1)
        v = acc * cos + _rotate_half(acc, hd) * sin
        for o in o_refs:
            o[...] = v.astype(o.dtype)

    @pl.when(jnp.logical_not(roped))
    def _():
        for o in o_refs:
            o[...] = acc.astype(o.dtype)


def mm_rope(x, w, cols, cos, sin, flags, out_dtypes, *, hd, tm, tn):
    n, k = x.shape
    c0, m = cols
    t = cos.shape[0]
    tm = min(tm, n)
    assert n % tm == 0 and m % tn == 0 and c0 % tn == 0 and t % tm == 0 and tn % hd == 0
    tper = t // tm
    j0 = c0 // tn
    grid_spec = pltpu.PrefetchScalarGridSpec(
        num_scalar_prefetch=1,
        grid=(n // tm, m // tn),
        in_specs=[
            pl.BlockSpec((tm, k), lambda i, j, f: (i, 0)),
            pl.BlockSpec((k, tn), lambda i, j, f: (0, j0 + j)),
            pl.BlockSpec((tm, hd), lambda i, j, f: (i % tper, 0)),
            pl.BlockSpec((tm, hd), lambda i, j, f: (i % tper, 0)),
        ],
        out_specs=[pl.BlockSpec((tm, tn), lambda i, j, f: (i, j)) for _ in out_dtypes],
    )
    return pl.pallas_call(
        functools.partial(_mm_rope_kernel, hd=hd),
        grid_spec=grid_spec,
        out_shape=[jax.ShapeDtypeStruct((n, m), d) for d in out_dtypes],
        compiler_params=_params("parallel", "arbitrary"),
        name="mm_rope",
    )(flags, x, w, cos, sin)


def _mm_kernel(x_ref, w_ref, o_ref):
    o_ref[...] = _dot(x_ref[...].astype(bf16), w_ref[...]).astype(o_ref.dtype)


def mm(x, w, cols, out_dtype, *, tm, tn):
    n, k = x.shape
    c0, m = cols
    tm = min(tm, n)
    tn = min(tn, m)
    assert n % tm == 0 and m % tn == 0 and c0 % tn == 0
    j0 = c0 // tn
    return pl.pallas_call(
        _mm_kernel,
        grid=(n // tm, m // tn),
        in_specs=[pl.BlockSpec((tm, k), lambda i, j: (i, 0)), pl.BlockSpec((k, tn), lambda i, j: (0, j0 + j))],
        out_specs=pl.BlockSpec((tm, tn), lambda i, j: (i, j)),
        out_shape=jax.ShapeDtypeStruct((n, m), out_dtype),
        compiler_params=_params("parallel", "arbitrary"),
        name="mm",
    )(x, w)


def _mm_ln_kernel(a_ref, w_ref, x_ref, g_ref, b_ref, o_ref, acc_ref):
    kk = pl.program_id(1)

    @pl.when(kk == 0)
    def _():
        acc_ref[...] = jnp.zeros_like(acc_ref)

    acc_ref[...] += _dot(a_ref[...].astype(bf16), w_ref[...])

    @pl.when(kk == pl.num_programs(1) - 1)
    def _():
        o_ref[...] = _layer_norm(DN_ALPHA * x_ref[...] + acc_ref[...], g_ref[...], b_ref[...])


def _mm_ln_single_kernel(a_ref, w_ref, x_ref, g_ref, b_ref, o_ref):
    acc = _dot(a_ref[...].astype(bf16), w_ref[...])
    o_ref[...] = _layer_norm(DN_ALPHA * x_ref[...] + acc, g_ref[...], b_ref[...])


def mm_ln(a, w, x, g, b, *, tm, tk):
    n, k = a.shape
    d = w.shape[1]
    tm = min(tm, n)
    tk = min(tk, k)
    assert n % tm == 0 and k % tk == 0
    if tk == k:
        return pl.pallas_call(
            _mm_ln_single_kernel,
            grid=(n // tm,),
            in_specs=[
                pl.BlockSpec((tm, k), lambda i: (i, 0)),
                pl.BlockSpec((k, d), lambda i: (0, 0)),
                pl.BlockSpec((tm, d), lambda i: (i, 0)),
                pl.BlockSpec((1, d), lambda i: (0, 0)),
                pl.BlockSpec((1, d), lambda i: (0, 0)),
            ],
            out_specs=pl.BlockSpec((tm, d), lambda i: (i, 0)),
            out_shape=jax.ShapeDtypeStruct((n, d), f32),
            compiler_params=_params("parallel"),
            name="mm_ln",
        )(a, w, x, g, b)
    return pl.pallas_call(
        _mm_ln_kernel,
        grid=(n // tm, k // tk),
        in_specs=[
            pl.BlockSpec((tm, tk), lambda i, kk: (i, kk)),
            pl.BlockSpec((tk, d), lambda i, kk: (kk, 0)),
            pl.BlockSpec((tm, d), lambda i, kk: (i, 0)),
            pl.BlockSpec((1, d), lambda i, kk: (0, 0)),
            pl.BlockSpec((1, d), lambda i, kk: (0, 0)),
        ],
        out_specs=pl.BlockSpec((tm, d), lambda i, kk: (i, 0)),
        out_shape=jax.ShapeDtypeStruct((n, d), f32),
        scratch_shapes=[pltpu.VMEM((tm, d), f32)],
        compiler_params=_params("parallel", "arbitrary"),
        name="mm_ln",
    )(a, w, x, g, b)


def _retention_kernel(q_ref, k_ref, v_ref, g_ref, gn_ref, intra_ref, qdec_ref, kdec_ref, o_ref, s_ref, state):
    c = pl.program_id(2)

    @pl.when(c == 0)
    def _():
        state[...] = jnp.zeros_like(state)

    for hh in range(state.shape[0]):
        kcols = slice(hh * RET_DK, (hh + 1) * RET_DK)
        vcols = slice(hh * RET_DV, (hh + 1) * RET_DV)
        q = q_ref[:, kcols]
        k = k_ref[:, kcols] * (RET_DK ** -0.5)
        v = v_ref[:, vcols]
        qdec = qdec_ref[hh]
        kdec = kdec_ref[hh]
        cdec = qdec[-1:, :]
        s_prev = state[hh]
        att = _dot_nt(q, k) * intra_ref[hh]
        o = _dot(att.astype(bf16), v) + _dot(q, s_prev.astype(bf16)) * qdec
        kd = (k.astype(f32) * kdec).astype(bf16)
        s_new = s_prev * cdec + _dot_tn(kd, v)
        state[hh] = s_new

        mu = jnp.mean(o, axis=-1, keepdims=True)
        d = o - mu
        var = jnp.mean(d * d, axis=-1, keepdims=True)
        o = d * lax.rsqrt(var + LN_EPS) * gn_ref[:, vcols]
        o_ref[:, vcols] = (jax.nn.silu(g_ref[:, vcols].astype(f32)) * o).astype(o_ref.dtype)

    @pl.when(c == pl.num_programs(2) - 1)
    def _():
        s_ref[0] = state[...]


def _retention_decay_tables(chunk):
    h = jnp.arange(RET_HEADS, dtype=f32)
    log_gamma = jnp.log1p(-jnp.exp2(-5.0 - h))
    idx = jnp.arange(chunk, dtype=f32)
    rel = idx[:, None] - idx[None, :]
    intra = jnp.where(rel >= 0, jnp.exp(jnp.maximum(rel, 0.0)[None] * log_gamma[:, None, None]), 0.0)
    qdec = jnp.exp((idx[None, :] + 1.0) * log_gamma[:, None])[..., None]
    kdec = jnp.exp((chunk - 1.0 - idx[None, :]) * log_gamma[:, None])[..., None]
    return intra, qdec, kdec


def retention_prompt(qk, vg, gn_g, batch):
    n = qk.shape[0]
    t = n // batch
    chunk = math.gcd(t, RET_CHUNK)
    nch = t // chunk
    h = RET_HEADS
    hps = RET_HEADS_PER_STEP
    hb = h // hps
    intra, qdec, kdec = _retention_decay_tables(chunk)
    row = lambda b, hh, c: b * nch + c
    return pl.pallas_call(
        _retention_kernel,
        grid=(batch, hb, nch),
        in_specs=[
            pl.BlockSpec((chunk, hps * RET_DK), lambda b, hh, c: (row(b, hh, c), hh)),
            pl.BlockSpec((chunk, hps * RET_DK), lambda b, hh, c: (row(b, hh, c), hb + hh)),
            pl.BlockSpec((chunk, hps * RET_DV), lambda b, hh, c: (row(b, hh, c), hh)),
            pl.BlockSpec((chunk, hps * RET_DV), lambda b, hh, c: (row(b, hh, c), hb + hh)),
            pl.BlockSpec((1, hps * RET_DV), lambda b, hh, c: (0, hh)),
            pl.BlockSpec((hps, chunk, chunk), lambda b, hh, c: (hh, 0, 0)),
            pl.BlockSpec((hps, chunk, 1), lambda b, hh, c: (hh, 0, 0)),
            pl.BlockSpec((hps, chunk, 1), lambda b, hh, c: (hh, 0, 0)),
        ],
        out_specs=[
            pl.BlockSpec((chunk, hps * RET_DV), lambda b, hh, c: (row(b, hh, c), hh)),
            pl.BlockSpec((1, hps, RET_DK, RET_DV), lambda b, hh, c: (b, hh, 0, 0)),
        ],
        out_shape=[
            jax.ShapeDtypeStruct((n, h * RET_DV), bf16),
            jax.ShapeDtypeStruct((batch, h, RET_DK, RET_DV), f32),
        ],
        scratch_shapes=[pltpu.VMEM((hps, RET_DK, RET_DV), f32)],
        compiler_params=_params("parallel", "parallel", "arbitrary"),
        name="retention_prompt",
    )(qk, qk, vg, vg, gn_g, intra, qdec, kdec)


def _rope_tables(pos, hd):
    half = hd // 2
    inv = ROPE_THETA ** (-jnp.arange(half, dtype=f32) / half)
    ang = pos.astype(f32)[:, None] * inv[None, :]
    cos, sin = jnp.cos(ang), jnp.sin(ang)
    return jnp.concatenate([cos, cos], axis=1), jnp.concatenate([-sin, sin], axis=1)


def _compress_units(load_unit_row, pe, w1_ref):
    xa, xb = [], []
    for l in range(CMP_STRIDE):
        xl = load_unit_row(l)
        xa.append((xl + pe[l:l + 1]).astype(bf16))
        xb.append((xl + pe[CMP_STRIDE + l:CMP_STRIDE + l + 1]).astype(bf16))
    half = CMP_STRIDE * NSA_HD
    a = _dot(jnp.concatenate(xa, axis=1), w1_ref[0, :half])
    b = _dot(jnp.concatenate(xb, axis=1), w1_ref[0, half:])
    return a, b


def _compress_kernel(kv_ref, pe_ref, w1_ref, w2_ref, o_ref):
    nu = kv_ref.shape[0] // CMP_STRIDE
    a, b = _compress_units(lambda l: kv_ref[pl.ds(l, nu, stride=CMP_STRIDE), :], pe_ref[0], w1_ref)
    hid = a + pltpu.roll(b, nu - 1, 0)
    o_ref[0, 0, 0] = _dot(jax.nn.gelu(hid).astype(bf16), w2_ref[0]).astype(o_ref.dtype)


def compress_prompt(kvf, pe, w1, w2, batch):
    n = kvf.shape[0]
    t = n // batch
    nu = t // CMP_STRIDE
    g = NSA_KV
    return pl.pallas_call(
        _compress_kernel,
        grid=(batch, 2, g),
        in_specs=[
            pl.BlockSpec((t, NSA_HD), lambda b, s, gg: (b, s * g + gg)),
            pl.BlockSpec((1, CMP_BLOCK, NSA_HD), lambda b, s, gg: (s, 0, 0)),
            pl.BlockSpec((1, CMP_BLOCK * NSA_HD, CMP_HIDDEN), lambda b, s, gg: (s, 0, 0)),
            pl.BlockSpec((1, CMP_HIDDEN, NSA_HD), lambda b, s, gg: (s, 0, 0)),
        ],
        out_specs=pl.BlockSpec((1, 1, 1, nu, NSA_HD), lambda b, s, gg: (b, s, gg, 0, 0)),
        out_shape=jax.ShapeDtypeStruct((batch, 2, g, nu, NSA_HD), bf16),
        compiler_params=_params("parallel", "parallel", "parallel"),
        name="compress_prompt",
    )(kvf, pe, w1, w2)


def _select_blocks(sel, n_top, axis):
    ns = sel.shape[axis]
    jidx = lax.broadcasted_iota(jnp.int32, sel.shape, axis)
    chosen = jnp.zeros(sel.shape, f32)
    for _ in range(n_top):
        mx = jnp.max(sel, axis=axis, keepdims=True)
        idx = jnp.min(jnp.where(sel == mx, jidx, ns), axis=axis, keepdims=True)
        hit = jidx == idx
        chosen = jnp.where(hit, 1.0, chosen)
        sel = jnp.where(hit, -jnp.inf, sel)
    return chosen


def _masked_softmax(s, mask):
    s = jnp.where(mask, s, NEG)
    e = jnp.exp(s - jnp.max(s, axis=-1, keepdims=True))
    return jnp.where(mask, e / jnp.sum(e, axis=-1, keepdims=True), 0.0)


def _nsa_attn_kernel(q_ref, kc_ref, vc_ref, ks_ref, vs_ref, kw_ref, vw_ref, gl_ref, msel_ref, exp_ref,
                     o_ref, vs_t, vw_t, vc_t, *, n_top):
    qi = pl.program_id(2)
    blk = q_ref.shape[0]
    r_heads, hd = NSA_HPG, NSA_HD
    t_len = ks_ref.shape[0]
    nc = kc_ref.shape[3]
    ck = 8 * blk

    def transposed(ref_rows):
        return ref_rows.astype(f32).T.astype(bf16)

    @pl.when(qi == 0)
    def _():
        def tr(c, carry):
            off = pl.multiple_of(c * blk, blk)
            vs_t[0:hd, pl.ds(off, blk)] = transposed(vs_ref[pl.ds(off, blk), :])
            vw_t[0:hd, pl.ds(off, blk)] = transposed(vw_ref[pl.ds(off, blk), :])
            return carry

        lax.fori_loop(0, t_len // blk, tr, 0)
        vs_t[hd:, :] = jnp.ones((vs_t.shape[0] - hd, t_len), bf16)
        vw_t[hd:, :] = jnp.ones((vw_t.shape[0] - hd, t_len), bf16)
        for c in range(nc // blk):
            vc_t[:, c * blk:(c + 1) * blk] = transposed(vc_ref[0, 0, 0, c * blk:(c + 1) * blk, :])

    def head(x, r):
        return x[:, r * blk:(r + 1) * blk]

    def per_head(fn):
        return jnp.concatenate([fn(r) for r in range(r_heads)], axis=1)

    q = q_ref[...].astype(f32) * (NSA_SCALE * math.log2(math.e))
    q_t = per_head(lambda r: q[:, r * hd:(r + 1) * hd].T).astype(bf16)
    t_q = qi * blk + lax.broadcasted_iota(jnp.int32, (1, blk), 1)

    z = _dot(kc_ref[0, 0, 0], q_t)
    c_end = lax.broadcasted_iota(jnp.int32, (nc, 1), 0) * CMP_STRIDE + (CMP_BLOCK - 1)
    mc = c_end <= t_q
    z = per_head(lambda r: jnp.where(mc, head(z, r), NEG))
    e = jnp.exp2(z - jnp.max(z, axis=0, keepdims=True))
    p = e * (1.0 / jnp.sum(e, axis=0, keepdims=True))
    p = per_head(lambda r: jnp.where(mc, head(p, r), 0.0))
    oc_t = _dot(vc_t[...], p.astype(bf16))

    span = WINDOW + blk
    w_off = pl.multiple_of(jnp.clip(qi * blk - WINDOW, 0, t_len - span), blk)
    kpos = w_off + lax.broadcasted_iota(jnp.int32, (span, 1), 0)
    in_win = (kpos <= t_q) & (kpos > t_q - WINDOW)
    z = _dot(kw_ref[pl.ds(w_off, span), :], q_t)
    z = per_head(lambda r: jnp.where(in_win, head(z, r), NEG))
    e = jnp.exp2(z - jnp.max(z, axis=0, keepdims=True))
    pv = _dot(vw_t[:, pl.ds(w_off, span)], e.astype(bf16))
    ow_t = pv[0:hd] * (1.0 / pv[hd:hd + 1])

    imp = head(p, 0)
    for r in range(1, r_heads):
        imp = imp + head(p, r)
    sel = jnp.dot(msel_ref[...], imp, precision=lax.Precision.HIGHEST, preferred_element_type=f32)
    jidx = lax.broadcasted_iota(jnp.int32, sel.shape, 0)
    cur = lax.shift_right_logical(t_q, int(math.log2(SEL_BLOCK)))
    forced = (jidx == 0) | (jidx == cur) | (jidx == cur - 1)
    sel = jnp.where(jidx * SEL_BLOCK <= t_q, jnp.where(forced, FORCE_SCORE, sel), NEG)
    chosen = _select_blocks(sel, n_top, 0).astype(bf16)

    def flash(k_ref, vt_ref, n_chunks, chunk_of, mask_fn):
        def body(i, carry):
            m, acc = carry
            off = pl.multiple_of(chunk_of(i) * ck, ck)
            z = _dot(k_ref[pl.ds(off, ck), :], q_t)
            msk = mask_fn(off, off + lax.broadcasted_iota(jnp.int32, (ck, 1), 0))
            z = per_head(lambda r: jnp.where(msk, head(z, r), NEG))
            m_new = jnp.maximum(m, jnp.max(z, axis=0, keepdims=True))
            alpha = jnp.exp2(m - m_new)
            e = jnp.exp2(z - m_new)
            return m_new, acc * alpha + _dot(vt_ref[:, pl.ds(off, ck)], e.astype(bf16))

        width = r_heads * blk
        init = (jnp.full((1, width), NEG, f32), jnp.zeros((vt_ref.shape[0], width), f32))
        _, acc = lax.fori_loop(0, n_chunks, body, init)
        return acc[0:hd] * (1.0 / acc[hd:hd + 1])

    os_t = flash(ks_ref, vs_t, qi // (ck // blk) + 1, lambda i: i,
                 lambda off, kpos: (_dot(exp_ref[pl.ds(off, ck), :], chosen) > 0.5) & (kpos <= t_q))

    g_t = jax.nn.sigmoid(gl_ref[...]).T
    outs = []
    for r in range(r_heads):
        o_r = (g_t[3 * r:3 * r + 1] * head(oc_t, r) + g_t[3 * r + 1:3 * r + 2] * head(os_t, r)
               + g_t[3 * r + 2:3 * r + 3] * head(ow_t, r))
        outs.append(o_r.T)
    o_ref[...] = jnp.concatenate(outs, axis=1).astype(o_ref.dtype)


def _selection_constants(nu, ns, nkeys):
    c = jnp.arange(nu)[None, :]
    j = jnp.arange(ns)[:, None]
    per_sel = SEL_BLOCK // CMP_STRIDE
    rb = CMP_BLOCK // CMP_STRIDE
    msel = sum(((c + r) // per_sel == j).astype(f32) for r in range(rb)) / rb
    msel = jnp.where(c < nu - rb + 1, msel, 0.0)
    expand = (jnp.arange(nkeys)[:, None] // SEL_BLOCK == jnp.arange(ns)[None, :]).astype(bf16)
    return msel, expand


def nsa_attn_prompt(q, cmp, kvb, gl, batch):
    n = q.shape[0]
    t = n // batch
    blk = math.gcd(t, Q_BLOCK)
    nq = t // blk
    g = NSA_KV
    nu = t // CMP_STRIDE
    ns = t // SEL_BLOCK
    msel, expand = _selection_constants(nu, ns, t)
    gw = NSA_HPG * NSA_HD
    ones_rows = 16
    kv_spec = lambda slot: pl.BlockSpec((t, NSA_HD), lambda b, gg, qi: (b, slot * g + gg))
    cmp_spec = lambda s: pl.BlockSpec((1, 1, 1, nu, NSA_HD), lambda b, gg, qi: (b, s, gg, 0, 0))
    return pl.pallas_call(
        functools.partial(_nsa_attn_kernel, n_top=min(N_SEL, ns)),
        grid=(batch, g, nq),
        in_specs=[
            pl.BlockSpec((blk, gw), lambda b, gg, qi: (b * nq + qi, gg)),
            cmp_spec(0), cmp_spec(1),
            kv_spec(2), kv_spec(3), kv_spec(4), kv_spec(5),
            pl.BlockSpec((blk, LANES), lambda b, gg, qi: (b * nq + qi, gg)),
            pl.BlockSpec((ns, nu), lambda b, gg, qi: (0, 0)),
            pl.BlockSpec((t, ns), lambda b, gg, qi: (0, 0)),
        ],
        out_specs=pl.BlockSpec((blk, gw), lambda b, gg, qi: (b * nq + qi, gg)),
        out_shape=jax.ShapeDtypeStruct((n, NSA_HEADS * NSA_HD), bf16),
        scratch_shapes=[pltpu.VMEM((NSA_HD + ones_rows, t), bf16), pltpu.VMEM((NSA_HD + ones_rows, t), bf16),
                        pltpu.VMEM((NSA_HD, nu), bf16)],
        compiler_params=_params("parallel", "parallel", "arbitrary"),
        name="nsa_attn_prompt",
    )(q, cmp, cmp, kvb, kvb, kvb, kvb, gl, msel, expand)


def _top2(vals, lane):
    width = vals.shape[-1]
    m1 = jnp.max(vals, axis=-1, keepdims=True)
    i1 = jnp.min(jnp.where(vals == m1, lane, width), axis=-1, keepdims=True)
    rest = jnp.where(lane == i1, -2.0, vals)
    m2 = jnp.max(rest, axis=-1, keepdims=True)
    i2 = jnp.min(jnp.where(rest == m2, lane, width), axis=-1, keepdims=True)
    return m1, i1, m2, i2


def _router_kernel(x_ref, w_ref, b_ref, info_ref, cnt_ref, carry):
    @pl.when(pl.program_id(0) == 0)
    def _():
        carry[...] = jnp.zeros_like(carry)

    logits = jnp.dot(x_ref[...], w_ref[...], precision=lax.Precision.HIGHEST, preferred_element_type=f32)
    logits = logits + b_ref[...]
    e = jnp.exp(logits - jnp.max(logits, axis=-1, keepdims=True))
    aff = e / jnp.sum(e, axis=-1, keepdims=True)
    tm = aff.shape[0]
    lane = lax.broadcasted_iota(jnp.int32, aff.shape, 1)
    lane_grp = lax.shift_right_logical(lane, int(math.log2(EXPERTS_PER_GROUP)))

    best, grp = None, None
    for gidx in range(N_GROUPS):
        m1, _, m2, _ = _top2(jnp.where(lane_grp == gidx, aff, -1.0), lane)
        score = m1 + m2
        if gidx == 0:
            best, grp = score, jnp.zeros_like(lane[:, :1])
        else:
            better = score > best
            grp = jnp.where(better, gidx, grp)
            best = jnp.where(better, score, best)
    m1, i1, m2, i2 = _top2(jnp.where(lane_grp == grp, aff, -1.0), lane)
    den = m1 + m2

    hot1 = (lane == i1).astype(f32)
    hot2 = (lane == i2).astype(f32)
    both = hot1 + hot2
    row = lax.broadcasted_iota(jnp.int32, (tm, tm), 0)
    col = lax.broadcasted_iota(jnp.int32, (tm, tm), 1)
    before = _dot((col < row).astype(bf16), both.astype(bf16)) + carry[...]
    rank1 = jnp.sum(hot1 * before, axis=-1, keepdims=True)
    rank2 = jnp.sum(hot2 * before, axis=-1, keepdims=True)
    carry[...] += jnp.sum(both, axis=0, keepdims=True)
    cnt_ref[...] = carry[...]

    cols = (i1.astype(f32), i2.astype(f32), m1 / den, m2 / den, rank1, rank2)
    info = jnp.zeros(aff.shape, f32)
    for c, v in enumerate(cols):
        info = jnp.where(lane == c, v, info)
    info_ref[...] = info


def moe_router(x, router_w, router_b, *, tm):
    n, d = x.shape
    tm = min(tm, n)
    assert n % tm == 0
    return pl.pallas_call(
        _router_kernel,
        grid=(n // tm,),
        in_specs=[
            pl.BlockSpec((tm, d), lambda i: (i, 0)),
            pl.BlockSpec((d, N_EXPERTS), lambda i: (0, 0)),
            pl.BlockSpec((1, N_EXPERTS), lambda i: (0, 0)),
        ],
        out_specs=[pl.BlockSpec((tm, N_EXPERTS), lambda i: (i, 0)), pl.BlockSpec((1, N_EXPERTS), lambda i: (0, 0))],
        out_shape=[jax.ShapeDtypeStruct((n, N_EXPERTS), f32), jax.ShapeDtypeStruct((1, N_EXPERTS), f32)],
        scratch_shapes=[pltpu.VMEM((1, N_EXPERTS), f32)],
        compiler_params=_params("arbitrary"),
        name="moe_router",
    )(x, router_w, router_b)


def _row_gather(src_hbm, dst, sem, index_of, n_rows, straight_line=False):
    def body(r, carry):
        pltpu.make_async_copy(src_hbm.at[pl.ds(index_of(r), 1), :], dst.at[pl.ds(r, 1), :], sem).start()
        return carry

    if straight_line:
        for r in range(n_rows):
            body(r, 0)
    else:
        lax.fori_loop(0, n_rows, body, 0, unroll=8)


def _row_gather_wait(src_hbm, dst, sem):
    pltpu.make_async_copy(src_hbm.at[pl.ds(0, dst.shape[0]), :], dst, sem).wait()


def _moe_kernel(te_ref, nt_ref, src_ref, x_hbm, wg_ref, wu_ref, wd_ref, o_ref, xbuf, sem):
    t = pl.program_id(0)
    n_live = nt_ref[0]
    tm = xbuf.shape[1]

    @pl.when(t == 0)
    def _():
        _row_gather(x_hbm, xbuf.at[0], sem.at[0], lambda r: src_ref[r], tm)

    @pl.when(t < n_live)
    def _():
        slot = t % 2
        _row_gather_wait(x_hbm, xbuf.at[slot], sem.at[slot])
        nxt = jnp.minimum(t + 1, n_live - 1)
        _row_gather(x_hbm, xbuf.at[1 - slot], sem.at[1 - slot], lambda r: src_ref[nxt * tm + r], tm,
                    straight_line=True)
        x = xbuf[slot].astype(bf16)
        h = jax.nn.silu(_dot(x, wg_ref[0])) * _dot(x, wu_ref[0])
        o_ref[...] = _dot(h.astype(bf16), wd_ref[0])

    @pl.when(t == n_live)
    def _():
        _row_gather_wait(x_hbm, xbuf.at[t % 2], sem.at[t % 2])

    @pl.when(t >= n_live)
    def _():
        o_ref[...] = jnp.zeros_like(o_ref)


def moe_experts(x, src, tile_expert, n_live, w_gate, w_up, w_down, e_base, *, tm):
    p = src.shape[0]
    d = x.shape[1]
    fdim = w_gate.shape[2]
    grid_spec = pltpu.PrefetchScalarGridSpec(
        num_scalar_prefetch=3,
        grid=(p // tm,),
        in_specs=[
            pl.BlockSpec(memory_space=pl.ANY),
            pl.BlockSpec((1, d, fdim), lambda i, te, nt, sr: (e_base + te[i], 0, 0)),
            pl.BlockSpec((1, d, fdim), lambda i, te, nt, sr: (e_base + te[i], 0, 0)),
            pl.BlockSpec((1, fdim, d), lambda i, te, nt, sr: (e_base + te[i], 0, 0)),
        ],
        out_specs=pl.BlockSpec((tm, d), lambda i, te, nt, sr: (i, 0)),
        scratch_shapes=[pltpu.VMEM((2, tm, d), f32), pltpu.SemaphoreType.DMA((2,))],
    )
    return pl.pallas_call(
        _moe_kernel,
        grid_spec=grid_spec,
        out_shape=jax.ShapeDtypeStruct((p, d), f32),
        compiler_params=_params("arbitrary"),
        name="moe_experts",
    )(tile_expert, n_live, src, x, w_gate, w_up, w_down)


def moe_layer(x, router_w, router_b, w_gate, w_up, w_down, e_base, *, tm):
    n = x.shape[0]
    info, counts = moe_router(x, router_w, router_b, tm=512)
    e1, e2 = info[:, 0].astype(jnp.int32), info[:, 1].astype(jnp.int32)
    r1, r2 = info[:, 4].astype(jnp.int32), info[:, 5].astype(jnp.int32)
    cnt = counts[0].astype(jnp.int32)
    padded = (cnt + tm - 1) // tm * tm
    ends = jnp.cumsum(padded)
    starts = ends - padded
    d1, d2 = starts[e1] + r1, starts[e2] + r2
    n_tiles = -(-2 * n // tm) + N_EXPERTS + 1
    p = n_tiles * tm
    tok = jnp.arange(n, dtype=jnp.int32)
    src = jnp.zeros((p,), jnp.int32).at[jnp.concatenate([d1, d2])].set(jnp.concatenate([tok, tok]))
    n_live = ends[-1:] // tm
    tile_ix = jnp.arange(n_tiles, dtype=jnp.int32)
    tile_start = jnp.minimum(tile_ix, n_live[0] - 1) * tm
    tile_expert = jnp.sum((ends[None, :] <= tile_start[:, None]).astype(jnp.int32), axis=1)
    ys = moe_experts(x, src, tile_expert, n_live.astype(jnp.int32), w_gate, w_up, w_down, e_base, tm=tm)
    return ys, d1, d2, info


def _ln_ple_kernel(d1_ref, d2_ref, x_ref, info_ref, ys_hbm, g_ref, b_ref, p_ref, wg_ref, wp_ref, o_ref, ybuf, sem):
    i = pl.program_id(0)
    tm = x_ref.shape[0]

    last = pl.num_programs(0) - 1

    def start(tile, slot, straight_line):
        _row_gather(ys_hbm, ybuf.at[slot, 0], sem.at[slot], lambda r: d1_ref[tile * tm + r], tm, straight_line)
        _row_gather(ys_hbm, ybuf.at[slot, 1], sem.at[slot], lambda r: d2_ref[tile * tm + r], tm, straight_line)

    def wait(slot):
        _row_gather_wait(ys_hbm, ybuf.at[slot, 0], sem.at[slot])
        _row_gather_wait(ys_hbm, ybuf.at[slot, 1], sem.at[slot])

    @pl.when(i == 0)
    def _():
        start(0, 0, False)

    slot = i % 2
    wait(slot)
    start(jnp.minimum(i + 1, last), 1 - slot, True)
    info = info_ref[...]
    y = info[:, 2:3] * ybuf[slot, 0] + info[:, 3:4] * ybuf[slot, 1]
    x2 = _layer_norm(DN_ALPHA * x_ref[...] + y, g_ref[...], b_ref[...])
    gate = jax.nn.sigmoid(_dot(x2.astype(bf16), wg_ref[...]))
    o_ref[...] = x2 + gate * _dot(p_ref[...].astype(bf16), wp_ref[...])

    @pl.when(i == last)
    def _():
        wait(1 - slot)


def ln_ple(x, ys, d1, d2, info, g, b, p, w_gate, w_proj, *, tm):
    n, d = x.shape
    tm = min(tm, n)
    assert n % tm == 0
    pd = p.shape[1]
    row = lambda i, a, c: (i, 0)
    fixed = lambda i, a, c: (0, 0)
    grid_spec = pltpu.PrefetchScalarGridSpec(
        num_scalar_prefetch=2,
        grid=(n // tm,),
        in_specs=[
            pl.BlockSpec((tm, d), row), pl.BlockSpec((tm, info.shape[1]), row), pl.BlockSpec(memory_space=pl.ANY),
            pl.BlockSpec((1, d), fixed), pl.BlockSpec((1, d), fixed),
            pl.BlockSpec((tm, pd), row), pl.BlockSpec((d, d), fixed), pl.BlockSpec((pd, d), fixed),
        ],
        out_specs=pl.BlockSpec((tm, d), row),
        scratch_shapes=[pltpu.VMEM((2, 2, tm, d), f32), pltpu.SemaphoreType.DMA((2,))],
    )
    return pl.pallas_call(
        _ln_ple_kernel,
        grid_spec=grid_spec,
        out_shape=jax.ShapeDtypeStruct((n, d), f32),
        compiler_params=_params("arbitrary"),
        name="ln_ple",
    )(d1, d2, x, info, ys, g, b, p, w_gate, w_proj)


def _retention_decode_kernel(q_ref, k_ref, v_ref, g_ref, gn_ref, dec_ref, s_ref, o_ref, so_ref):
    q = q_ref[0]
    k = k_ref[0] * (RET_DK ** -0.5)
    v = v_ref[0]
    gamma = dec_ref[0]
    s_prev = s_ref[0, 0]
    att = jnp.sum(q.astype(f32) * k.astype(f32), axis=-1, keepdims=True)
    rows = 8
    q8 = jnp.broadcast_to(q, (rows, RET_DK))
    first = (lax.broadcasted_iota(jnp.int32, (rows, 1), 0) == 0).astype(f32)
    k8 = (jnp.broadcast_to(k.astype(f32), (rows, RET_DK)) * first).astype(bf16)
    v8 = jnp.broadcast_to(v, (rows, RET_DV))
    o = att.astype(bf16).astype(f32) * v.astype(f32) + _dot(q8, s_prev.astype(bf16))[:1] * gamma
    so_ref[0, 0] = s_prev * gamma + _dot_tn(k8, v8)
    mu = jnp.mean(o, axis=-1, keepdims=True)
    d = o - mu
    var = jnp.mean(d * d, axis=-1, keepdims=True)
    o = d * lax.rsqrt(var + LN_EPS) * gn_ref[...]
    o_ref[0] = (jax.nn.silu(g_ref[0].astype(f32)) * o).astype(o_ref.dtype)


def retention_decode(qk, vg, gn_g, states, base):
    b = qk.shape[0]
    h = RET_HEADS
    _, qdec, _ = _retention_decay_tables(1)
    qk3, vg3 = qk[:, None, :], vg[:, None, :]
    o, s = pl.pallas_call(
        _retention_decode_kernel,
        grid=(b, h),
        in_specs=[
            pl.BlockSpec((1, 1, RET_DK), lambda i, hh: (i, 0, hh)),
            pl.BlockSpec((1, 1, RET_DK), lambda i, hh: (i, 0, h + hh)),
            pl.BlockSpec((1, 1, RET_DV), lambda i, hh: (i, 0, hh)),
            pl.BlockSpec((1, 1, RET_DV), lambda i, hh: (i, 0, h + hh)),
            pl.BlockSpec((1, RET_DV), lambda i, hh: (0, hh)),
            pl.BlockSpec((1, 1, 1), lambda i, hh: (hh, 0, 0)),
            pl.BlockSpec((1, 1, RET_DK, RET_DV), lambda i, hh: (base + i, hh, 0, 0)),
        ],
        out_specs=[
            pl.BlockSpec((1, 1, RET_DV), lambda i, hh: (i, 0, hh)),
            pl.BlockSpec((1, 1, RET_DK, RET_DV), lambda i, hh: (i, hh, 0, 0)),
        ],
        out_shape=[jax.ShapeDtypeStruct((b, 1, h * RET_DV), bf16), jax.ShapeDtypeStruct((b, h, RET_DK, RET_DV), f32)],
        compiler_params=_params("parallel", "parallel"),
        name="retention_decode",
    )(qk3, qk3, vg3, vg3, gn_g, qdec, states)
    return o[:, 0, :], s


HIST_PAGES_PER_STEP = 32


def _compress_hist_kernel(pt_ref, *refs, n_pages):
    page_refs = refs[:n_pages]
    pe_ref, w1_ref, w2_ref, o_ref, a_buf, o_buf = refs[n_pages:]
    g = NSA_KV
    upp = page_refs[0].shape[1] // CMP_STRIDE
    rows = n_pages * upp * g
    pad = a_buf.shape[0] - rows

    def load(l):
        return jnp.concatenate([pr[0, pl.ds(l, upp, stride=CMP_STRIDE), :, :].reshape(upp * g, NSA_HD)
                                for pr in page_refs], axis=0)

    a, b = _compress_units(load, pe_ref[0], w1_ref)

    @pl.when(pl.program_id(2) == 0)
    def _():
        a_buf[0:pad, :] = jnp.zeros((pad, a_buf.shape[1]), f32)

    @pl.when(pl.program_id(2) > 0)
    def _():
        a_buf[0:pad, :] = a_buf[rows:rows + pad, :]

    a_buf[pad:pad + rows, :] = a
    a_prev = a_buf[pad - g:pad - g + rows, :]
    o_buf[...] = _dot(jax.nn.gelu(a_prev + b).astype(bf16), w2_ref[0])
    for gg in range(g):
        o_ref[0, 0, gg] = o_buf[pl.ds(gg, rows // g, stride=g), :].astype(o_ref.dtype)


def compress_history(pool, page_table, pe, w1, w2):
    b, ppb = page_table.shape
    page = pool.shape[1]
    g = NSA_KV
    n_pages = min(HIST_PAGES_PER_STEP, ppb)
    assert ppb % n_pages == 0 and page % CMP_STRIDE == 0
    upp = page // CMP_STRIDE
    seg = n_pages * upp
    nu = ppb * upp
    sublanes = 8
    page_spec = lambda k: pl.BlockSpec(
        (1, page, None, g, NSA_HD), lambda i, s, ch, pt: (pt[i, ch * n_pages + k], 0, s, 0, 0))
    grid_spec = pltpu.PrefetchScalarGridSpec(
        num_scalar_prefetch=1,
        grid=(b, 2, ppb // n_pages),
        in_specs=[page_spec(k) for k in range(n_pages)] + [
            pl.BlockSpec((1, CMP_BLOCK, NSA_HD), lambda i, s, ch, pt: (s, 0, 0)),
            pl.BlockSpec((1, CMP_BLOCK * NSA_HD, CMP_HIDDEN), lambda i, s, ch, pt: (s, 0, 0)),
            pl.BlockSpec((1, CMP_HIDDEN, NSA_HD), lambda i, s, ch, pt: (s, 0, 0)),
        ],
        out_specs=pl.BlockSpec((1, 1, g, seg, NSA_HD), lambda i, s, ch, pt: (i, s, 0, ch, 0)),
        scratch_shapes=[pltpu.VMEM((seg * g + sublanes, CMP_HIDDEN), f32), pltpu.VMEM((seg * g, NSA_HD), f32)],
    )
    return pl.pallas_call(
        functools.partial(_compress_hist_kernel, n_pages=n_pages),
        grid_spec=grid_spec,
        out_shape=jax.ShapeDtypeStruct((b, 2, g, nu, NSA_HD), bf16),
        compiler_params=_params("parallel", "parallel", "arbitrary"),
        name="compress_history",
    )(page_table, *([pool] * n_pages), pe, w1, w2)


def _nsa_select_decode_kernel(q_ref, kc_ref, vc_ref, msel_ref, oc_ref, top_ref, *, t, ns, n_top):
    g, r_heads, hd = NSA_KV, NSA_HPG, NSA_HD
    q = q_ref[0]
    nu = kc_ref.shape[3]
    u = lax.broadcasted_iota(jnp.int32, (1, nu), 1)
    visible = (u >= 1) & (u * CMP_STRIDE + (CMP_STRIDE - 1) <= t)
    imps = []
    for gg in range(g):
        heads = [q[:, (gg * r_heads + r) * hd:(gg * r_heads + r + 1) * hd] for r in range(r_heads)]
        qb = jnp.concatenate(heads, axis=0)
        p = _masked_softmax(_dot_nt(qb, kc_ref[0, 0, gg]) * NSA_SCALE, visible)
        oc_ref[0, gg] = _dot(p.astype(bf16), vc_ref[0, 0, gg])
        imps.append(jnp.sum(p, axis=0, keepdims=True))
    imp = jnp.concatenate(imps, axis=0)
    sel = jnp.dot(imp, msel_ref[...], precision=lax.Precision.HIGHEST, preferred_element_type=f32)
    jidx = lax.broadcasted_iota(jnp.int32, sel.shape, 1)
    cur = t // SEL_BLOCK
    forced = (jidx == 0) | (jidx == cur) | (jidx == cur - 1)
    sel = jnp.where(jidx * SEL_BLOCK <= t, jnp.where(forced, FORCE_SCORE, sel), NEG)
    sel = jnp.where(jidx < ns, sel, -jnp.inf)
    width = sel.shape[-1]
    lane = lax.broadcasted_iota(jnp.int32, (g, top_ref.shape[-1]), 1)
    top = jnp.zeros(lane.shape, jnp.int32)
    for it in range(n_top):
        mx = jnp.max(sel, axis=-1, keepdims=True)
        idx = jnp.min(jnp.where(sel == mx, jidx, width), axis=-1, keepdims=True)
        top = jnp.where(lane == it, idx, top)
        sel = jnp.where(jidx == idx, -jnp.inf, sel)
    top_ref[0] = top


def nsa_select_decode(q, cmp, t):
    b = q.shape[0]
    g = NSA_KV
    nu = cmp.shape[3]
    ns = (t + 1 + SEL_BLOCK - 1) // SEL_BLOCK
    ns_pad = -(-ns // LANES) * LANES
    per_sel = SEL_BLOCK // CMP_STRIDE
    u = jnp.arange(nu)[:, None]
    j = jnp.arange(ns_pad)[None, :]
    rb = CMP_BLOCK // CMP_STRIDE
    msel = sum(((u - 1 + r) // per_sel == j).astype(f32) for r in range(rb)) / rb
    msel = jnp.where(u >= 1, msel, 0.0)
    gw = NSA_HPG * NSA_HD
    n_top = min(N_SEL, ns)
    cmp_spec = lambda s: pl.BlockSpec((1, 1, g, nu, NSA_HD), lambda i: (i, s, 0, 0, 0))
    return pl.pallas_call(
        functools.partial(_nsa_select_decode_kernel, t=t, ns=ns, n_top=n_top),
        grid=(b,),
        in_specs=[
            pl.BlockSpec((1, 1, g * gw), lambda i: (i, 0, 0)),
            cmp_spec(0), cmp_spec(1),
            pl.BlockSpec((nu, ns_pad), lambda i: (0, 0)),
        ],
        out_specs=[
            pl.BlockSpec((1, g, NSA_HPG, NSA_HD), lambda i: (i, 0, 0, 0)),
            pl.BlockSpec((1, g, LANES), lambda i: (i, 0, 0)),
        ],
        out_shape=[jax.ShapeDtypeStruct((b, g, NSA_HPG, NSA_HD), f32), jax.ShapeDtypeStruct((b, g, LANES), jnp.int32)],
        compiler_params=_params("parallel"),
        name="nsa_select_decode",
    )(q[:, None, :], cmp, cmp, msel)


def _nsa_attn_decode_kernel(pt_ref, top_ref, *refs, n_top, n_hist_blocks):
    k_refs, v_refs = refs[:n_top], refs[n_top:2 * n_top]
    (q_ref, oc_ref, knew_ref, vnew_ref, kw_ref, vw_ref, kwnew_ref, vwnew_ref, gl_ref, o_ref) = refs[2 * n_top:]
    i, gg = pl.program_id(0), pl.program_id(1)
    r_heads, hd = NSA_HPG, NSA_HD
    q = q_ref[0]
    qb = jnp.concatenate([q[:, r * hd:(r + 1) * hd] for r in range(r_heads)], axis=0)

    def attend(keys, vals, mask, k_new, v_new):
        s = jnp.where(mask, _dot_nt(qb, keys) * NSA_SCALE, NEG)
        s_new = jnp.sum(qb.astype(f32) * k_new.astype(f32), axis=-1, keepdims=True) * NSA_SCALE
        m = jnp.maximum(jnp.max(s, axis=-1, keepdims=True), s_new)
        e = jnp.where(mask, jnp.exp(s - m), 0.0)
        e_new = jnp.exp(s_new - m)
        num = _dot(e.astype(bf16), vals) + e_new.astype(bf16).astype(f32) * v_new.astype(f32)
        return num / (jnp.sum(e, axis=-1, keepdims=True) + e_new)

    g = NSA_KV

    def rows_of(ref):
        v = ref[0]
        return v.reshape(v.shape[0] * g, hd).astype(bf16)

    def own_group(n_rows):
        return lax.broadcasted_iota(jnp.int32, (1, n_rows * g), 1) % g == gg

    sb = k_refs[0].shape[1]
    keys = jnp.concatenate([rows_of(r) for r in k_refs], axis=0)
    vals = jnp.concatenate([rows_of(r) for r in v_refs], axis=0)
    blk_of_lane = lax.broadcasted_iota(jnp.int32, (1, n_top * sb * g), 1) // (sb * g)
    sel_of_lane = jnp.zeros((1, n_top * sb * g), jnp.int32)
    for n in range(n_top):
        sel_of_lane = jnp.where(blk_of_lane == n, top_ref[i, gg, n], sel_of_lane)
    o_s = attend(keys, vals, (sel_of_lane < n_hist_blocks) & own_group(n_top * sb), knew_ref[0], vnew_ref[0])

    wlen = kw_ref.shape[1]
    wmask = (lax.broadcasted_iota(jnp.int32, (1, wlen * g), 1) >= g) & own_group(wlen)
    o_w = attend(rows_of(kw_ref), rows_of(vw_ref), wmask, kwnew_ref[0], vwnew_ref[0])

    gates = jax.nn.sigmoid(gl_ref[0])
    o_c = oc_ref[0, 0]
    outs = []
    for r in range(r_heads):
        outs.append(gates[:, 3 * r:3 * r + 1] * o_c[r:r + 1] + gates[:, 3 * r + 1:3 * r + 2] * o_s[r:r + 1]
                    + gates[:, 3 * r + 2:3 * r + 3] * o_w[r:r + 1])
    o_ref[0] = jnp.concatenate(outs, axis=1).astype(o_ref.dtype)


def nsa_attn_decode(q, o_c, top, kvb, gl, pool, page_table, win, win_base, t):
    b = q.shape[0]
    g = NSA_KV
    n_top = top.shape[-1]
    page = pool.shape[1]
    assert t % SEL_BLOCK == 0 and page % SEL_BLOCK == 0 and win.shape[1] == WINDOW
    n_hist_blocks = t // SEL_BLOCK
    bpp = page // SEL_BLOCK
    gw = NSA_HPG * NSA_HD

    def blk_spec(n, slot):
        def imap(i, gg, pt, tp):
            j = jnp.minimum(tp[i, gg, n], n_hist_blocks - 1)
            return (pt[i, j // bpp], j % bpp, slot, 0, 0)
        return pl.BlockSpec((1, SEL_BLOCK, None, g, NSA_HD), imap)

    new_spec = lambda slot: pl.BlockSpec((1, 1, NSA_HD), lambda i, gg, pt, tp: (i, 0, slot * g + gg))
    win_spec = lambda slot: pl.BlockSpec((1, WINDOW, None, g, NSA_HD),
                                         lambda i, gg, pt, tp: (win_base + i, 0, slot, 0, 0))
    kvb3 = kvb[:, None, :]
    grid_spec = pltpu.PrefetchScalarGridSpec(
        num_scalar_prefetch=2,
        grid=(b, g),
        in_specs=[blk_spec(n, 2) for n in range(n_top)] + [blk_spec(n, 3) for n in range(n_top)] + [
            pl.BlockSpec((1, 1, gw), lambda i, gg, pt, tp: (i, 0, gg)),
            pl.BlockSpec((1, 1, NSA_HPG, NSA_HD), lambda i, gg, pt, tp: (i, gg, 0, 0)),
            new_spec(2), new_spec(3), win_spec(0), win_spec(1), new_spec(4), new_spec(5),
            pl.BlockSpec((1, 1, LANES), lambda i, gg, pt, tp: (i, 0, gg)),
        ],
        out_specs=pl.BlockSpec((1, 1, gw), lambda i, gg, pt, tp: (i, 0, gg)),
    )
    o = pl.pallas_call(
        functools.partial(_nsa_attn_decode_kernel, n_top=n_top, n_hist_blocks=n_hist_blocks),
        grid_spec=grid_spec,
        out_shape=jax.ShapeDtypeStruct((b, 1, NSA_HEADS * NSA_HD), bf16),
        compiler_params=_params("parallel", "parallel"),
        name="nsa_attn_decode",
    )(page_table, top, *([pool] * (2 * n_top)), q[:, None, :], o_c, kvb3, kvb3, win, win, kvb3, kvb3,
      gl[:, None, :])
    return o[:, 0, :]


def _retention_mixer(x, pos, batch, w, states, base):
    cos, sin = _rope_tables(pos, RET_DK)
    tn = PROJ_TILE
    n_qk = 2 * RET_HEADS * RET_DK
    flags = jnp.ones((n_qk // tn,), jnp.int32)
    qk, = mm_rope(x, w["in"], (0, n_qk), cos, sin, flags, [bf16], hd=RET_DK, tm=PROJ_TILE, tn=tn)
    vg = mm(x, w["in"], (n_qk, 2 * RET_HEADS * RET_DV), bf16, tm=PROJ_TILE, tn=PROJ_TILE)
    if states is None:
        return retention_prompt(qk, vg, w["gn"], batch)
    return retention_decode(qk, vg, w["gn"], states, base)


def _nsa_projections(x, pos, w):
    cos, sin = _rope_tables(pos, NSA_HD)
    tn = NSA_KV * NSA_HD
    n_q = NSA_HEADS * NSA_HD
    n_kv = 6 * NSA_KV * NSA_HD
    q, = mm_rope(x, w["in"], (0, n_q), cos, sin, jnp.ones((n_q // tn,), jnp.int32), [bf16],
                 hd=NSA_HD, tm=PROJ_TILE, tn=tn)
    kvf, kvb = mm_rope(x, w["in"], (n_q, n_kv), cos, sin, jnp.array([1, 0] * 3, jnp.int32), [f32, bf16],
                       hd=NSA_HD, tm=PROJ_TILE, tn=tn)
    gl = mm(x, w["gl"], (0, w["gl"].shape[1]), f32, tm=PROJ_TILE, tn=tn)
    return q, kvf, kvb, gl


def _layer_tail(x, h, p, w, tm_moe):
    x1 = mm_ln(h, w["out"], x, w["ln_g"][0:1], w["ln_b"][0:1], tm=512, tk=2048)
    ys, d1, d2, info = moe_layer(x1, w["router_w"], w["router_b"], w["moe_gate"], w["moe_up"], w["moe_down"],
                                 w["moe_base"], tm=tm_moe)
    return ln_ple(x1, ys, d1, d2, info, w["ln_g"][1:2], w["ln_b"][1:2], p, w["ple_gate"], w["ple_proj"], tm=256)


def kernel(x_prompt, x_sample, state_ret, cache_nsa_kv, state_nsa_win, page_table, p_prompt, p_sample, ret_w_in, ret_w_out, ret_gn_g, nsa_w_in, nsa_w_out, nsa_cmp_pos, nsa_cmp_w1, nsa_cmp_w2, ln_g, ln_b, router_w, router_b, moe_w_gate, moe_w_up, moe_w_down, ple_w_gate, ple_w_proj):
    bp, tp, d = x_prompt.shape
    bs, ts, _ = x_sample.shape
    assert ts == 1
    n_pool, page = cache_nsa_kv.shape[1], cache_nsa_kv.shape[2]
    past = page_table.shape[1] * page
    g, hd = NSA_KV, NSA_HD
    kv_cols = N_KV_SLOTS * g * hd

    xp = x_prompt.reshape(bp * tp, d)
    xs = x_sample.reshape(bs * ts, d)
    pos_p = jnp.arange(tp, dtype=jnp.int32)
    pos_s = jnp.full((bs,), past, jnp.int32)
    states = state_ret.reshape((-1,) + state_ret.shape[2:])
    pool = cache_nsa_kv.reshape((-1,) + cache_nsa_kv.shape[2:])
    wins = state_nsa_win.reshape((-1,) + state_nsa_win.shape[2:])

    moe_gate = moe_w_gate.astype(bf16).reshape((-1,) + moe_w_gate.shape[2:])
    moe_up = moe_w_up.astype(bf16).reshape((-1,) + moe_w_up.shape[2:])
    moe_down = moe_w_down.astype(bf16).reshape((-1,) + moe_w_down.shape[2:])

    ret_p, ret_s, kv_p, kv_s, win_p, win_s = [], [], [], [], [], []
    for i in range(DEPTH):
        j = i // 2
        w = {
            "ln_g": ln_g[i], "ln_b": ln_b[i],
            "router_w": router_w, "router_b": router_b[None, :],
            "moe_gate": moe_gate, "moe_up": moe_up, "moe_down": moe_down, "moe_base": i * N_EXPERTS,
            "ple_gate": ple_w_gate[i].astype(bf16), "ple_proj": ple_w_proj[i].astype(bf16),
        }
        if i % 2 == 0:
            w.update({"in": ret_w_in[j].astype(bf16), "out": ret_w_out[j].astype(bf16), "gn": ret_gn_g[j][None, :]})
            hp, sp = _retention_mixer(xp, pos_p, bp, w, None, 0)
            hs, ss = _retention_mixer(xs, pos_s, bs, w, states, j * bs)
            ret_p.append(sp)
            ret_s.append(ss)
        else:
            nq = NSA_HEADS * hd
            w_in = nsa_w_in[j]
            gl = w_in[:, nq + 6 * g * hd:].reshape(d, g, NSA_HPG * 3)
            gl = jnp.pad(gl, ((0, 0), (0, 0), (0, LANES - NSA_HPG * 3))).reshape(d, g * LANES)
            w.update({"in": w_in.astype(bf16), "gl": gl.astype(bf16), "out": nsa_w_out[j].astype(bf16)})
            pe = nsa_cmp_pos[j]
            w1 = nsa_cmp_w1[j].reshape(2, CMP_BLOCK * hd, CMP_HIDDEN).astype(bf16)
            w2 = nsa_cmp_w2[j].astype(bf16)
            q, kvf, kvb, glp = _nsa_projections(xp, pos_p, w)
            cmp = compress_prompt(kvf, pe, w1, w2, bp)
            hp = nsa_attn_prompt(q, cmp, kvb, glp, bp)
            kv_p.append(kvf[:, :kv_cols].reshape(bp, tp, N_KV_SLOTS, g, hd))
            keep = min(WINDOW, tp)
            win_p.append(kvf.reshape(bp, tp, -1)[:, tp - keep:, kv_cols:].reshape(bp, keep, 2, g, hd))
            q, kvf, kvb, gls = _nsa_projections(xs, pos_s, w)
            pt = page_table + j * n_pool
            cmp = compress_history(pool, pt, pe, w1, w2)
            o_c, top = nsa_select_decode(q, cmp, past)
            top = top[:, :, :min(N_SEL, past // SEL_BLOCK + 1)]
            hs = nsa_attn_decode(q, o_c, top, kvb, gls, pool, pt, wins, j * bs, past)
            kv_s.append(kvf[:, :kv_cols].reshape(bs, ts, N_KV_SLOTS, g, hd))
            new_win = kvf[:, kv_cols:].reshape(bs, ts, 2, g, hd)
            win_s.append(jnp.concatenate([state_nsa_win[j][:, ts:], new_win], axis=1))
        xp = _layer_tail(xp, hp, p_prompt[i].reshape(bp * tp, -1), w, MOE_TILE)
        xs = _layer_tail(xs, hs, p_sample[i].reshape(bs * ts, -1), w, 16)

    return (xp.reshape(bp, tp, d), xs.reshape(bs, ts, d), jnp.stack(ret_p), jnp.stack(ret_s),
            jnp.stack(kv_p), jnp.stack(kv_s), jnp.stack(win_p), jnp.stack(win_s))
```

```python
import functools
import math

import jax
import jax.numpy as jnp
from jax import lax
from jax.experimental import pallas as pl
from jax.experimental.pallas import tpu as pltpu

f32 = jnp.float32
bf16 = jnp.bfloat16

D_MODEL = 2048
DEPTH = 4
RET_HEADS = 8
RET_DK = D_MODEL // RET_HEADS
RET_DV = 2 * RET_DK
RET_CHUNK = 128
NSA_HEADS = 16
NSA_HD = D_MODEL // NSA_HEADS
NSA_KV = 4
NSA_HPG = NSA_HEADS // NSA_KV
CMP_BLOCK = 32
CMP_STRIDE = 16
CMP_HIDDEN = 4 * NSA_HD
SEL_BLOCK = 64
N_SEL = 16
WINDOW = 512
Q_BLOCK = 128
N_KV_SLOTS = 4
N_EXPERTS = 16
N_GROUPS = 4
EXPERTS_PER_GROUP = N_EXPERTS // N_GROUPS
D_EXPERT = 1408
PLE_DIM = 256
ROPE_THETA = 10000.0
LN_EPS = 1e-5
DN_ALPHA = (2 * DEPTH) ** 0.25
NEG = -1e30
FORCE_SCORE = float(NSA_HPG + 1)
NSA_SCALE = NSA_HD ** -0.5

V7X_VMEM_BYTES = 64 * 1024 * 1024
VMEM_LIMIT = V7X_VMEM_BYTES - 8 * 1024 * 1024
LANES = 128
MOE_TILE = 256
PROJ_TILE = 1024
RET_HEADS_PER_STEP = 8
NSA_GROUPS_PER_STEP = 2


def _params(*sem):
    return pltpu.CompilerParams(dimension_semantics=sem, vmem_limit_bytes=VMEM_LIMIT)


def _layer_norm(v, g, b):
    mu = jnp.mean(v, axis=-1, keepdims=True)
    d = v - mu
    var = jnp.mean(d * d, axis=-1, keepdims=True)
    return d * lax.rsqrt(var + LN_EPS) * g + b


def _dot(a, b):
    return jnp.dot(a, b, preferred_element_type=f32)


def _dot_nt(a, b):
    return lax.dot_general(a, b, (((1,), (1,)), ((), ())), preferred_element_type=f32)


def _dot_tn(a, b):
    return lax.dot_general(a, b, (((0,), (0,)), ((), ())), preferred_element_type=f32)


def _rotate_half(v, hd):
    pieces = []
    for c in range(0, v.shape[1], hd):
        if hd == 2 * LANES:
            pieces += [v[:, c + LANES:c + hd], v[:, c:c + LANES]]
        else:
            pieces.append(pltpu.roll(v[:, c:c + hd], hd // 2, 1))
    return jnp.concatenate(pieces, axis=1)


def _mm_rope_kernel(flags_ref, x_ref, w_ref, cos_ref, sin_ref, *o_refs, hd):
    acc = _dot(x_ref[...].astype(bf16), w_ref[...])
    roped = flags_ref[pl.program_id(1)] == 1

    @pl.when(roped)
    def _():
        reps = acc.shape[1] // hd
        cos = jnp.concatenate([cos_ref[...]] * reps, axis=1)
        sin = jnp.concatenate([sin_ref[...]] * reps, axis=1)
        v = acc * cos + _rotate_half(acc, hd) * sin
        for o in o_refs:
            o[...] = v.astype(o.dtype)

    @pl.when(jnp.logical_not(roped))
    def _():
        for o in o_refs:
            o[...] = acc.astype(o.dtype)


def mm_rope(x, w, cols, cos, sin, flags, out_dtypes, *, hd, tm, tn):
    n, k = x.shape
    c0, m = cols
    t = cos.shape[0]
    tm = min(tm, n)
    assert n % tm == 0 and m % tn == 0 and c0 % tn == 0 and t % tm == 0 and tn % hd == 0
    tper = t // tm
    j0 = c0 // tn
    grid_spec = pltpu.PrefetchScalarGridSpec(
        num_scalar_prefetch=1,
        grid=(n // tm, m // tn),
        in_specs=[
            pl.BlockSpec((tm, k), lambda i, j, f: (i, 0)),
            pl.BlockSpec((k, tn), lambda i, j, f: (0, j0 + j)),
            pl.BlockSpec((tm, hd), lambda i, j, f: (i % tper, 0)),
            pl.BlockSpec((tm, hd), lambda i, j, f: (i % tper, 0)),
        ],
        out_specs=[pl.BlockSpec((tm, tn), lambda i, j, f: (i, j)) for _ in out_dtypes],
    )
    return pl.pallas_call(
        functools.partial(_mm_rope_kernel, hd=hd),
        grid_spec=grid_spec,
        out_shape=[jax.ShapeDtypeStruct((n, m), d) for d in out_dtypes],
        compiler_params=_params("parallel", "arbitrary"),
        name="mm_rope",
    )(flags, x, w, cos, sin)


def _mm_kernel(x_ref, w_ref, o_ref):
    o_ref[...] = _dot(x_ref[...].astype(bf16), w_ref[...]).astype(o_ref.dtype)


def mm(x, w, cols, out_dtype, *, tm, tn):
    n, k = x.shape
    c0, m = cols
    tm = min(tm, n)
    tn = min(tn, m)
    assert n % tm == 0 and m % tn == 0 and c0 % tn == 0
    j0 = c0 // tn
    return pl.pallas_call(
        _mm_kernel,
        grid=(n // tm, m // tn),
        in_specs=[pl.BlockSpec((tm, k), lambda i, j: (i, 0)), pl.BlockSpec((k, tn), lambda i, j: (0, j0 + j))],
        out_specs=pl.BlockSpec((tm, tn), lambda i, j: (i, j)),
        out_shape=jax.ShapeDtypeStruct((n, m), out_dtype),
        compiler_params=_params("parallel", "arbitrary"),
        name="mm",
    )(x, w)


def _mm_ln_kernel(a_ref, w_ref, x_ref, g_ref, b_ref, o_ref, acc_ref):
    kk = pl.program_id(1)

    @pl.when(kk == 0)
    def _():
        acc_ref[...] = jnp.zeros_like(acc_ref)

    acc_ref[...] += _dot(a_ref[...].astype(bf16), w_ref[...])

    @pl.when(kk == pl.num_programs(1) - 1)
    def _():
        o_ref[...] = _layer_norm(DN_ALPHA * x_ref[...] + acc_ref[...], g_ref[...], b_ref[...])


def _mm_ln_single_kernel(a_ref, w_ref, x_ref, g_ref, b_ref, o_ref):
    acc = _dot(a_ref[...].astype(bf16), w_ref[...])
    o_ref[...] = _layer_norm(DN_ALPHA * x_ref[...] + acc, g_ref[...], b_ref[...])


def mm_ln(a, w, x, g, b, *, tm, tk):
    n, k = a.shape
    d = w.shape[1]
    tm = min(tm, n)
    tk = min(tk, k)
    assert n % tm == 0 and k % tk == 0
    if tk == k:
        return pl.pallas_call(
            _mm_ln_single_kernel,
            grid=(n // tm,),
            in_specs=[
                pl.BlockSpec((tm, k), lambda i: (i, 0)),
                pl.BlockSpec((k, d), lambda i: (0, 0)),
                pl.BlockSpec((tm, d), lambda i: (i, 0)),
                pl.BlockSpec((1, d), lambda i: (0, 0)),
                pl.BlockSpec((1, d), lambda i: (0, 0)),
            ],
            out_specs=pl.BlockSpec((tm, d), lambda i: (i, 0)),
            out_shape=jax.ShapeDtypeStruct((n, d), f32),
            compiler_params=_params("parallel"),
            name="mm_ln",
        )(a, w, x, g, b)
    return pl.pallas_call(
        _mm_ln_kernel,
        grid=(n // tm, k // tk),
        in_specs=[
            pl.BlockSpec((tm, tk), lambda i, kk: (i, kk)),
            pl.BlockSpec((tk, d), lambda i, kk: (kk, 0)),
            pl.BlockSpec((tm, d), lambda i, kk: (i, 0)),
            pl.BlockSpec((1, d), lambda i, kk: (0, 0)),
            pl.BlockSpec((1, d), lambda i, kk: (0, 0)),
        ],
        out_specs=pl.BlockSpec((tm, d), lambda i, kk: (i, 0)),
        out_shape=jax.ShapeDtypeStruct((n, d), f32),
        scratch_shapes=[pltpu.VMEM((tm, d), f32)],
        compiler_params=_params("parallel", "arbitrary"),
        name="mm_ln",
    )(a, w, x, g, b)


def _retention_kernel(q_ref, k_ref, v_ref, g_ref, gn_ref, intra_ref, qdec_ref, kdec_ref, o_ref, s_ref, state):
    c = pl.program_id(2)

    @pl.when(c == 0)
    def _():
        state[...] = jnp.zeros_like(state)

    for hh in range(state.shape[0]):
        kcols = slice(hh * RET_DK, (hh + 1) * RET_DK)
        vcols = slice(hh * RET_DV, (hh + 1) * RET_DV)
        q = q_ref[:, kcols]
        k = k_ref[:, kcols] * (RET_DK ** -0.5)
        v = v_ref[:, vcols]
        qdec = qdec_ref[hh]
        kdec = kdec_ref[hh]
        cdec = qdec[-1:, :]
        s_prev = state[hh]
        att = _dot_nt(q, k) * intra_ref[hh]
        o = _dot(att.astype(bf16), v) + _dot(q, s_prev.astype(bf16)) * qdec
        kd = (k.astype(f32) * kdec).astype(bf16)
        s_new = s_prev * cdec + _dot_tn(kd, v)
        state[hh] = s_new

        mu = jnp.mean(o, axis=-1, keepdims=True)
        d = o - mu
        var = jnp.mean(d * d, axis=-1, keepdims=True)
        o = d * lax.rsqrt(var + LN_EPS) * gn_ref[:, vcols]
        o_ref[:, vcols] = (jax.nn.silu(g_ref[:, vcols].astype(f32)) * o).astype(o_ref.dtype)

    @pl.when(c == pl.num_programs(2) - 1)
    def _():
        s_ref[0] = state[...]


def _retention_decay_tables(chunk):
    h = jnp.arange(RET_HEADS, dtype=f32)
    log_gamma = jnp.log1p(-jnp.exp2(-5.0 - h))
    idx = jnp.arange(chunk, dtype=f32)
    rel = idx[:, None] - idx[None, :]
    intra = jnp.where(rel >= 0, jnp.exp(jnp.maximum(rel, 0.0)[None] * log_gamma[:, None, None]), 0.0)
    qdec = jnp.exp((idx[None, :] + 1.0) * log_gamma[:, None])[..., None]
    kdec = jnp.exp((chunk - 1.0 - idx[None, :]) * log_gamma[:, None])[..., None]
    return intra, qdec, kdec


def retention_prompt(qk, vg, gn_g, batch):
    n = qk.shape[0]
    t = n // batch
    chunk = math.gcd(t, RET_CHUNK)
    nch = t // chunk
    h = RET_HEADS
    hps = RET_HEADS_PER_STEP
    hb = h // hps
    intra, qdec, kdec = _retention_decay_tables(chunk)
    row = lambda b, hh, c: b * nch + c
    return pl.pallas_call(
        _retention_kernel,
        grid=(batch, hb, nch),
        in_specs=[
            pl.BlockSpec((chunk, hps * RET_DK), lambda b, hh, c: (row(b, hh, c), hh)),
            pl.BlockSpec((chunk, hps * RET_DK), lambda b, hh, c: (row(b, hh, c), hb + hh)),
            pl.BlockSpec((chunk, hps * RET_DV), lambda b, hh, c: (row(b, hh, c), hh)),
            pl.BlockSpec((chunk, hps * RET_DV), lambda b, hh, c: (row(b, hh, c), hb + hh)),
            pl.BlockSpec((1, hps * RET_DV), lambda b, hh, c: (0, hh)),
            pl.BlockSpec((hps, chunk, chunk), lambda b, hh, c: (hh, 0, 0)),
            pl.BlockSpec((hps, chunk, 1), lambda b, hh, c: (hh, 0, 0)),
            pl.BlockSpec((hps, chunk, 1), lambda b, hh, c: (hh, 0, 0)),
        ],
        out_specs=[
            pl.BlockSpec((chunk, hps * RET_DV), lambda b, hh, c: (row(b, hh, c), hh)),
            pl.BlockSpec((1, hps, RET_DK, RET_DV), lambda b, hh, c: (b, hh, 0, 0)),
        ],
        out_shape=[
            jax.ShapeDtypeStruct((n, h * RET_DV), bf16),
            jax.ShapeDtypeStruct((batch, h, RET_DK, RET_DV), f32),
        ],
        scratch_shapes=[pltpu.VMEM((hps, RET_DK, RET_DV), f32)],
        compiler_params=_params("parallel", "parallel", "arbitrary"),
        name="retention_prompt",
    )(qk, qk, vg, vg, gn_g, intra, qdec, kdec)


def _rope_tables(pos, hd):
    half = hd // 2
    inv = ROPE_THETA ** (-jnp.arange(half, dtype=f32) / half)
    ang = pos.astype(f32)[:, None] * inv[None, :]
    cos, sin = jnp.cos(ang), jnp.sin(ang)
    return jnp.concatenate([cos, cos], axis=1), jnp.concatenate([-sin, sin], axis=1)


def _compress_units(load_unit_row, pe, w1_ref):
    xa, xb = [], []
    for l in range(CMP_STRIDE):
        xl = load_unit_row(l)
        xa.append((xl + pe[l:l + 1]).astype(bf16))
        xb.append((xl + pe[CMP_STRIDE + l:CMP_STRIDE + l + 1]).astype(bf16))
    half = CMP_STRIDE * NSA_HD
    a = _dot(jnp.concatenate(xa, axis=1), w1_ref[0, :half])
    b = _dot(jnp.concatenate(xb, axis=1), w1_ref[0, half:])
    return a, b


def _compress_kernel(kv_ref, pe_ref, w1_ref, w2_ref, o_ref):
    nu = kv_ref.shape[0] // CMP_STRIDE
    a, b = _compress_units(lambda l: kv_ref[pl.ds(l, nu, stride=CMP_STRIDE), :], pe_ref[0], w1_ref)
    hid = a + pltpu.roll(b, nu - 1, 0)
    o_ref[0, 0, 0] = _dot(jax.nn.gelu(hid).astype(bf16), w2_ref[0]).astype(o_ref.dtype)


def compress_prompt(kvf, pe, w1, w2, batch):
    n = kvf.shape[0]
    t = n // batch
    nu = t // CMP_STRIDE
    g = NSA_KV
    return pl.pallas_call(
        _compress_kernel,
        grid=(batch, 2, g),
        in_specs=[
            pl.BlockSpec((t, NSA_HD), lambda b, s, gg: (b, s * g + gg)),
            pl.BlockSpec((1, CMP_BLOCK, NSA_HD), lambda b, s, gg: (s, 0, 0)),
            pl.BlockSpec((1, CMP_BLOCK * NSA_HD, CMP_HIDDEN), lambda b, s, gg: (s, 0, 0)),
            pl.BlockSpec((1, CMP_HIDDEN, NSA_HD), lambda b, s, gg: (s, 0, 0)),
        ],
        out_specs=pl.BlockSpec((1, 1, 1, nu, NSA_HD), lambda b, s, gg: (b, s, gg, 0, 0)),
        out_shape=jax.ShapeDtypeStruct((batch, 2, g, nu, NSA_HD), bf16),
        compiler_params=_params("parallel", "parallel", "parallel"),
        name="compress_prompt",
    )(kvf, pe, w1, w2)


def _select_blocks(sel, n_top, axis):
    ns = sel.shape[axis]
    jidx = lax.broadcasted_iota(jnp.int32, sel.shape, axis)
    chosen = jnp.zeros(sel.shape, f32)
    for _ in range(n_top):
        mx = jnp.max(sel, axis=axis, keepdims=True)
        idx = jnp.min(jnp.where(sel == mx, jidx, ns), axis=axis, keepdims=True)
        hit = jidx == idx
        chosen = jnp.where(hit, 1.0, chosen)
        sel = jnp.where(hit, -jnp.inf, sel)
    return chosen


def _masked_softmax(s, mask):
    s = jnp.where(mask, s, NEG)
    e = jnp.exp(s - jnp.max(s, axis=-1, keepdims=True))
    return jnp.where(mask, e / jnp.sum(e, axis=-1, keepdims=True), 0.0)


def _nsa_attn_kernel(q_ref, kc_ref, vc_ref, ks_ref, vs_ref, kw_ref, vw_ref, gl_ref, msel_ref, exp_ref,
                     o_ref, vs_t, vw_t, vc_t, *, n_top):
    qi = pl.program_id(2)
    blk = q_ref.shape[0]
    r_heads, hd = NSA_HPG, NSA_HD
    t_len = ks_ref.shape[0]
    nc = kc_ref.shape[3]
    ck = 8 * blk
    subs = range(vs_t.shape[0])
    gw = r_heads * hd

    def cols(sub):
        return slice(sub * hd, (sub + 1) * hd)

    def transposed(ref_rows):
        return ref_rows.astype(f32).T.astype(bf16)

    @pl.when(qi == 0)
    def _():
        def tr(c, carry):
            off = pl.multiple_of(c * blk, blk)
            for sub in subs:
                vs_t[sub, 0:hd, pl.ds(off, blk)] = transposed(vs_ref[pl.ds(off, blk), cols(sub)])
                vw_t[sub, 0:hd, pl.ds(off, blk)] = transposed(vw_ref[pl.ds(off, blk), cols(sub)])
            return carry

        lax.fori_loop(0, t_len // blk, tr, 0)
        vs_t[:, hd:, :] = jnp.ones((len(subs), vs_t.shape[1] - hd, t_len), bf16)
        vw_t[:, hd:, :] = jnp.ones((len(subs), vw_t.shape[1] - hd, t_len), bf16)
        for sub in subs:
            for c in range(nc // blk):
                vc_t[sub, :, c * blk:(c + 1) * blk] = transposed(vc_ref[0, 0, sub, c * blk:(c + 1) * blk, :])

    def head(x, r):
        return x[:, r * blk:(r + 1) * blk]

    def per_head(fn):
        return jnp.concatenate([fn(r) for r in range(r_heads)], axis=1)

    t_q = qi * blk + lax.broadcasted_iota(jnp.int32, (1, blk), 1)
    span = WINDOW + blk
    w_off = pl.multiple_of(jnp.clip(qi * blk - WINDOW, 0, t_len - span), blk)

    def front(sub):
        q = q_ref[:, sub * gw:(sub + 1) * gw].astype(f32) * (NSA_SCALE * math.log2(math.e))
        q_t = per_head(lambda r: q[:, r * hd:(r + 1) * hd].T).astype(bf16)

        z = _dot(kc_ref[0, 0, sub], q_t)
        c_end = lax.broadcasted_iota(jnp.int32, (nc, 1), 0) * CMP_STRIDE + (CMP_BLOCK - 1)
        mc = c_end <= t_q
        z = per_head(lambda r: jnp.where(mc, head(z, r), NEG))
        e = jnp.exp2(z - jnp.max(z, axis=0, keepdims=True))
        p = e * (1.0 / jnp.sum(e, axis=0, keepdims=True))
        p = per_head(lambda r: jnp.where(mc, head(p, r), 0.0))
        oc_t = _dot(vc_t[sub], p.astype(bf16))

        kpos = w_off + lax.broadcasted_iota(jnp.int32, (span, 1), 0)
        in_win = (kpos <= t_q) & (kpos > t_q - WINDOW)
        z = _dot(kw_ref[pl.ds(w_off, span), cols(sub)], q_t)
        z = per_head(lambda r: jnp.where(in_win, head(z, r), NEG))
        e = jnp.exp2(z - jnp.max(z, axis=0, keepdims=True))
        pv = _dot(vw_t[sub, :, pl.ds(w_off, span)], e.astype(bf16))
        ow_t = pv[0:hd] * (1.0 / pv[hd:hd + 1])

        imp = head(p, 0)
        for r in range(1, r_heads):
            imp = imp + head(p, r)
        sel = jnp.dot(msel_ref[...], imp, precision=lax.Precision.HIGHEST, preferred_element_type=f32)
        jidx = lax.broadcasted_iota(jnp.int32, sel.shape, 0)
        cur = lax.shift_right_logical(t_q, int(math.log2(SEL_BLOCK)))
        forced = (jidx == 0) | (jidx == cur) | (jidx == cur - 1)
        sel = jnp.where(jidx * SEL_BLOCK <= t_q, jnp.where(forced, FORCE_SCORE, sel), NEG)
        chosen = _select_blocks(sel, n_top, 0).astype(bf16)
        return q_t, oc_t, ow_t, chosen

    fronts = [front(sub) for sub in subs]

    def body(i, carry):
        off = pl.multiple_of(i * ck, ck)
        causal = off + lax.broadcasted_iota(jnp.int32, (ck, 1), 0) <= t_q
        new = []
        for sub in subs:
            m, acc = carry[sub]
            q_t, chosen = fronts[sub][0], fronts[sub][3]
            z = _dot(ks_ref[pl.ds(off, ck), cols(sub)], q_t)
            msk = (_dot(exp_ref[pl.ds(off, ck), :], chosen) > 0.5) & causal
            z = per_head(lambda r: jnp.where(msk, head(z, r), NEG))
            m_new = jnp.maximum(m, jnp.max(z, axis=0, keepdims=True))
            alpha = jnp.exp2(m - m_new)
            e = jnp.exp2(z - m_new)
            new.append((m_new, acc * alpha + _dot(vs_t[sub, :, pl.ds(off, ck)], e.astype(bf16))))
        return tuple(new)

    width = r_heads * blk
    init = tuple((jnp.full((1, width), NEG, f32), jnp.zeros((vs_t.shape[1], width), f32)) for _ in subs)
    flashed = lax.fori_loop(0, qi // (ck // blk) + 1, body, init)

    outs = []
    for sub in subs:
        _, oc_t, ow_t, _ = fronts[sub]
        acc = flashed[sub][1]
        os_t = acc[0:hd] * (1.0 / acc[hd:hd + 1])
        g_t = jax.nn.sigmoid(gl_ref[:, sub * LANES:(sub + 1) * LANES]).T
        for r in range(r_heads):
            o_r = (g_t[3 * r:3 * r + 1] * head(oc_t, r) + g_t[3 * r + 1:3 * r + 2] * head(os_t, r)
                   + g_t[3 * r + 2:3 * r + 3] * head(ow_t, r))
            outs.append(o_r.T)
    o_ref[...] = jnp.concatenate(outs, axis=1).astype(o_ref.dtype)


def _selection_constants(nu, ns, nkeys):
    c = jnp.arange(nu)[None, :]
    j = jnp.arange(ns)[:, None]
    per_sel = SEL_BLOCK // CMP_STRIDE
    rb = CMP_BLOCK // CMP_STRIDE
    msel = sum(((c + r) // per_sel == j).astype(f32) for r in range(rb)) / rb
    msel = jnp.where(c < nu - rb + 1, msel, 0.0)
    expand = (jnp.arange(nkeys)[:, None] // SEL_BLOCK == jnp.arange(ns)[None, :]).astype(bf16)
    return msel, expand


def nsa_attn_prompt(q, cmp, kvb, gl, batch):
    n = q.shape[0]
    t = n // batch
    blk = math.gcd(t, Q_BLOCK)
    nq = t // blk
    g = NSA_KV
    nu = t // CMP_STRIDE
    ns = t // SEL_BLOCK
    msel, expand = _selection_constants(nu, ns, t)
    gw = NSA_HPG * NSA_HD
    ones_rows = 16
    gps = NSA_GROUPS_PER_STEP
    gb = g // gps
    kv_spec = lambda slot: pl.BlockSpec((t, gps * NSA_HD), lambda b, gg, qi: (b, slot * gb + gg))
    cmp_spec = lambda s: pl.BlockSpec((1, 1, gps, nu, NSA_HD), lambda b, gg, qi: (b, s, gg, 0, 0))
    return pl.pallas_call(
        functools.partial(_nsa_attn_kernel, n_top=min(N_SEL, ns)),
        grid=(batch, gb, nq),
        in_specs=[
            pl.BlockSpec((blk, gps * gw), lambda b, gg, qi: (b * nq + qi, gg)),
            cmp_spec(0), cmp_spec(1),
            kv_spec(2), kv_spec(3), kv_spec(4), kv_spec(5),
            pl.BlockSpec((blk, gps * LANES), lambda b, gg, qi: (b * nq + qi, gg)),
            pl.BlockSpec((ns, nu), lambda b, gg, qi: (0, 0)),
            pl.BlockSpec((t, ns), lambda b, gg, qi: (0, 0)),
        ],
        out_specs=pl.BlockSpec((blk, gps * gw), lambda b, gg, qi: (b * nq + qi, gg)),
        out_shape=jax.ShapeDtypeStruct((n, NSA_HEADS * NSA_HD), bf16),
        scratch_shapes=[pltpu.VMEM((gps, NSA_HD + ones_rows, t), bf16),
                        pltpu.VMEM((gps, NSA_HD + ones_rows, t), bf16), pltpu.VMEM((gps, NSA_HD, nu), bf16)],
        compiler_params=_params("parallel", "parallel", "arbitrary"),
        name="nsa_attn_prompt",
    )(q, cmp, cmp, kvb, kvb, kvb, kvb, gl, msel, expand)


def _top2(vals, lane):
    width = vals.shape[-1]
    m1 = jnp.max(vals, axis=-1, keepdims=True)
    i1 = jnp.min(jnp.where(vals == m1, lane, width), axis=-1, keepdims=True)
    rest = jnp.where(lane == i1, -2.0, vals)
    m2 = jnp.max(rest, axis=-1, keepdims=True)
    i2 = jnp.min(jnp.where(rest == m2, lane, width), axis=-1, keepdims=True)
    return m1, i1, m2, i2


def _router_kernel(x_ref, w_ref, b_ref, info_ref, cnt_ref, carry):
    @pl.when(pl.program_id(0) == 0)
    def _():
        carry[...] = jnp.zeros_like(carry)

    logits = jnp.dot(x_ref[...], w_ref[...], precision=lax.Precision.HIGHEST, preferred_element_type=f32)
    logits = logits + b_ref[...]
    e = jnp.exp(logits - jnp.max(logits, axis=-1, keepdims=True))
    aff = e / jnp.sum(e, axis=-1, keepdims=True)
    tm = aff.shape[0]
    lane = lax.broadcasted_iota(jnp.int32, aff.shape, 1)
    lane_grp = lax.shift_right_logical(lane, int(math.log2(EXPERTS_PER_GROUP)))

    best, grp = None, None
    for gidx in range(N_GROUPS):
        m1, _, m2, _ = _top2(jnp.where(lane_grp == gidx, aff, -1.0), lane)
        score = m1 + m2
        if gidx == 0:
            best, grp = score, jnp.zeros_like(lane[:, :1])
        else:
            better = score > best
            grp = jnp.where(better, gidx, grp)
            best = jnp.where(better, score, best)
    m1, i1, m2, i2 = _top2(jnp.where(lane_grp == grp, aff, -1.0), lane)
    den = m1 + m2

    hot1 = (lane == i1).astype(f32)
    hot2 = (lane == i2).astype(f32)
    both = hot1 + hot2
    row = lax.broadcasted_iota(jnp.int32, (tm, tm), 0)
    col = lax.broadcasted_iota(jnp.int32, (tm, tm), 1)
    before = _dot((col < row).astype(bf16), both.astype(bf16)) + carry[...]
    rank1 = jnp.sum(hot1 * before, axis=-1, keepdims=True)
    rank2 = jnp.sum(hot2 * before, axis=-1, keepdims=True)
    carry[...] += jnp.sum(both, axis=0, keepdims=True)
    cnt_ref[...] = carry[...]

    cols = (i1.astype(f32), i2.astype(f32), m1 / den, m2 / den, rank1, rank2)
    info = jnp.zeros(aff.shape, f32)
    for c, v in enumerate(cols):
        info = jnp.where(lane == c, v, info)
    info_ref[...] = info


def moe_router(x, router_w, router_b, *, tm):
    n, d = x.shape
    tm = min(tm, n)
    assert n % tm == 0
    return pl.pallas_call(
        _router_kernel,
        grid=(n // tm,),
        in_specs=[
            pl.BlockSpec((tm, d), lambda i: (i, 0)),
            pl.BlockSpec((d, N_EXPERTS), lambda i: (0, 0)),
            pl.BlockSpec((1, N_EXPERTS), lambda i: (0, 0)),
        ],
        out_specs=[pl.BlockSpec((tm, N_EXPERTS), lambda i: (i, 0)), pl.BlockSpec((1, N_EXPERTS), lambda i: (0, 0))],
        out_shape=[jax.ShapeDtypeStruct((n, N_EXPERTS), f32), jax.ShapeDtypeStruct((1, N_EXPERTS), f32)],
        scratch_shapes=[pltpu.VMEM((1, N_EXPERTS), f32)],
        compiler_params=_params("arbitrary"),
        name="moe_router",
    )(x, router_w, router_b)


def _row_gather(src_hbm, dst, sem, index_of, n_rows, straight_line=False):
    def body(r, carry):
        pltpu.make_async_copy(src_hbm.at[pl.ds(index_of(r), 1), :], dst.at[pl.ds(r, 1), :], sem).start()
        return carry

    if straight_line:
        for r in range(n_rows):
            body(r, 0)
    else:
        lax.fori_loop(0, n_rows, body, 0, unroll=8)


def _row_gather_wait(src_hbm, dst, sem):
    pltpu.make_async_copy(src_hbm.at[pl.ds(0, dst.shape[0]), :], dst, sem).wait()


def _moe_kernel(te_ref, nt_ref, src_ref, x_hbm, wg_ref, wu_ref, wd_ref, o_ref, xbuf, sem):
    t = pl.program_id(0)
    n_live = nt_ref[0]
    tm = xbuf.shape[1]

    @pl.when(t == 0)
    def _():
        _row_gather(x_hbm, xbuf.at[0], sem.at[0], lambda r: src_ref[r], tm)

    @pl.when(t < n_live)
    def _():
        slot = t % 2
        _row_gather_wait(x_hbm, xbuf.at[slot], sem.at[slot])
        nxt = jnp.minimum(t + 1, n_live - 1)
        _row_gather(x_hbm, xbuf.at[1 - slot], sem.at[1 - slot], lambda r: src_ref[nxt * tm + r], tm,
                    straight_line=True)
        x = xbuf[slot].astype(bf16)
        h = jax.nn.silu(_dot(x, wg_ref[0])) * _dot(x, wu_ref[0])
        o_ref[...] = _dot(h.astype(bf16), wd_ref[0])

    @pl.when(t == n_live)
    def _():
        _row_gather_wait(x_hbm, xbuf.at[t % 2], sem.at[t % 2])

    @pl.when(t >= n_live)
    def _():
        o_ref[...] = jnp.zeros_like(o_ref)


def moe_experts(x, src, tile_expert, n_live, w_gate, w_up, w_down, e_base, *, tm):
    p = src.shape[0]
    d = x.shape[1]
    fdim = w_gate.shape[2]
    grid_spec = pltpu.PrefetchScalarGridSpec(
        num_scalar_prefetch=3,
        grid=(p // tm,),
        in_specs=[
            pl.BlockSpec(memory_space=pl.ANY),
            pl.BlockSpec((1, d, fdim), lambda i, te, nt, sr: (e_base + te[i], 0, 0)),
            pl.BlockSpec((1, d, fdim), lambda i, te, nt, sr: (e_base + te[i], 0, 0)),
            pl.BlockSpec((1, fdim, d), lambda i, te, nt, sr: (e_base + te[i], 0, 0)),
        ],
        out_specs=pl.BlockSpec((tm, d), lambda i, te, nt, sr: (i, 0)),
        scratch_shapes=[pltpu.VMEM((2, tm, d), f32), pltpu.SemaphoreType.DMA((2,))],
    )
    return pl.pallas_call(
        _moe_kernel,
        grid_spec=grid_spec,
        out_shape=jax.ShapeDtypeStruct((p, d), f32),
        compiler_params=_params("arbitrary"),
        name="moe_experts",
    )(tile_expert, n_live, src, x, w_gate, w_up, w_down)


def moe_layer(x, router_w, router_b, w_gate, w_up, w_down, e_base, *, tm):
    n = x.shape[0]
    info, counts = moe_router(x, router_w, router_b, tm=512)
    e1, e2 = info[:, 0].astype(jnp.int32), info[:, 1].astype(jnp.int32)
    r1, r2 = info[:, 4].astype(jnp.int32), info[:, 5].astype(jnp.int32)
    cnt = counts[0].astype(jnp.int32)
    padded = (cnt + tm - 1) // tm * tm
    ends = jnp.cumsum(padded)
    starts = ends - padded
    d1, d2 = starts[e1] + r1, starts[e2] + r2
    n_tiles = -(-2 * n // tm) + N_EXPERTS + 1
    p = n_tiles * tm
    tok = jnp.arange(n, dtype=jnp.int32)
    src = jnp.zeros((p,), jnp.int32).at[jnp.concatenate([d1, d2])].set(jnp.concatenate([tok, tok]))
    n_live = ends[-1:] // tm
    tile_ix = jnp.arange(n_tiles, dtype=jnp.int32)
    tile_start = jnp.minimum(tile_ix, n_live[0] - 1) * tm
    tile_expert = jnp.sum((ends[None, :] <= tile_start[:, None]).astype(jnp.int32), axis=1)
    ys = moe_experts(x, src, tile_expert, n_live.astype(jnp.int32), w_gate, w_up, w_down, e_base, tm=tm)
    return ys, d1, d2, info


def _ln_ple_kernel(d1_ref, d2_ref, x_ref, info_ref, ys_hbm, g_ref, b_ref, p_ref, wg_ref, wp_ref, o_ref, ybuf, sem):
    i = pl.program_id(0)
    tm = x_ref.shape[0]

    last = pl.num_programs(0) - 1

    def start(tile, slot, straight_line):
        _row_gather(ys_hbm, ybuf.at[slot, 0], sem.at[slot], lambda r: d1_ref[tile * tm + r], tm, straight_line)
        _row_gather(ys_hbm, ybuf.at[slot, 1], sem.at[slot], lambda r: d2_ref[tile * tm + r], tm, straight_line)

    def wait(slot):
        _row_gather_wait(ys_hbm, ybuf.at[slot, 0], sem.at[slot])
        _row_gather_wait(ys_hbm, ybuf.at[slot, 1], sem.at[slot])

    @pl.when(i == 0)
    def _():
        start(0, 0, False)

    slot = i % 2
    wait(slot)
    start(jnp.minimum(i + 1, last), 1 - slot, True)
    info = info_ref[...]
    y = info[:, 2:3] * ybuf[slot, 0] + info[:, 3:4] * ybuf[slot, 1]
    x2 = _layer_norm(DN_ALPHA * x_ref[...] + y, g_ref[...], b_ref[...])
    gate = jax.nn.sigmoid(_dot(x2.astype(bf16), wg_ref[...]))
    o_ref[...] = x2 + gate * _dot(p_ref[...].astype(bf16), wp_ref[...])

    @pl.when(i == last)
    def _():
        wait(1 - slot)


def ln_ple(x, ys, d1, d2, info, g, b, p, w_gate, w_proj, *, tm):
    n, d = x.shape
    tm = min(tm, n)
    assert n % tm == 0
    pd = p.shape[1]
    row = lambda i, a, c: (i, 0)
    fixed = lambda i, a, c: (0, 0)
    grid_spec = pltpu.PrefetchScalarGridSpec(
        num_scalar_prefetch=2,
        grid=(n // tm,),
        in_specs=[
            pl.BlockSpec((tm, d), row), pl.BlockSpec((tm, info.shape[1]), row), pl.BlockSpec(memory_space=pl.ANY),
            pl.BlockSpec((1, d), fixed), pl.BlockSpec((1, d), fixed),
            pl.BlockSpec((tm, pd), row), pl.BlockSpec((d, d), fixed), pl.BlockSpec((pd, d), fixed),
        ],
        out_specs=pl.BlockSpec((tm, d), row),
        scratch_shapes=[pltpu.VMEM((2, 2, tm, d), f32), pltpu.SemaphoreType.DMA((2,))],
    )
    return pl.pallas_call(
        _ln_ple_kernel,
        grid_spec=grid_spec,
        out_shape=jax.ShapeDtypeStruct((n, d), f32),
        compiler_params=_params("arbitrary"),
        name="ln_ple",
    )(d1, d2, x, info, ys, g, b, p, w_gate, w_proj)


def _retention_decode_kernel(q_ref, k_ref, v_ref, g_ref, gn_ref, dec_ref, s_ref, o_ref, so_ref):
    q = q_ref[0]
    k = k_ref[0] * (RET_DK ** -0.5)
    v = v_ref[0]
    gamma = dec_ref[0]
    s_prev = s_ref[0, 0]
    att = jnp.sum(q.astype(f32) * k.astype(f32), axis=-1, keepdims=True)
    rows = 8
    q8 = jnp.broadcast_to(q, (rows, RET_DK))
    first = (lax.broadcasted_iota(jnp.int32, (rows, 1), 0) == 0).astype(f32)
    k8 = (jnp.broadcast_to(k.astype(f32), (rows, RET_DK)) * first).astype(bf16)
    v8 = jnp.broadcast_to(v, (rows, RET_DV))
    o = att.astype(bf16).astype(f32) * v.astype(f32) + _dot(q8, s_prev.astype(bf16))[:1] * gamma
    so_ref[0, 0] = s_prev * gamma + _dot_tn(k8, v8)
    mu = jnp.mean(o, axis=-1, keepdims=True)
    d = o - mu
    var = jnp.mean(d * d, axis=-1, keepdims=True)
    o = d * lax.rsqrt(var + LN_EPS) * gn_ref[...]
    o_ref[0] = (jax.nn.silu(g_ref[0].astype(f32)) * o).astype(o_ref.dtype)


def retention_decode(qk, vg, gn_g, states, base):
    b = qk.shape[0]
    h = RET_HEADS
    _, qdec, _ = _retention_decay_tables(1)
    qk3, vg3 = qk[:, None, :], vg[:, None, :]
    o, s = pl.pallas_call(
        _retention_decode_kernel,
        grid=(b, h),
        in_specs=[
            pl.BlockSpec((1, 1, RET_DK), lambda i, hh: (i, 0, hh)),
            pl.BlockSpec((1, 1, RET_DK), lambda i, hh: (i, 0, h + hh)),
            pl.BlockSpec((1, 1, RET_DV), lambda i, hh: (i, 0, hh)),
            pl.BlockSpec((1, 1, RET_DV), lambda i, hh: (i, 0, h + hh)),
            pl.BlockSpec((1, RET_DV), lambda i, hh: (0, hh)),
            pl.BlockSpec((1, 1, 1), lambda i, hh: (hh, 0, 0)),
            pl.BlockSpec((1, 1, RET_DK, RET_DV), lambda i, hh: (base + i, hh, 0, 0)),
        ],
        out_specs=[
            pl.BlockSpec((1, 1, RET_DV), lambda i, hh: (i, 0, hh)),
            pl.BlockSpec((1, 1, RET_DK, RET_DV), lambda i, hh: (i, hh, 0, 0)),
        ],
        out_shape=[jax.ShapeDtypeStruct((b, 1, h * RET_DV), bf16), jax.ShapeDtypeStruct((b, h, RET_DK, RET_DV), f32)],
        compiler_params=_params("parallel", "parallel"),
        name="retention_decode",
    )(qk3, qk3, vg3, vg3, gn_g, qdec, states)
    return o[:, 0, :], s


HIST_PAGES_PER_STEP = 32


def _compress_hist_kernel(pt_ref, *refs, n_pages):
    page_refs = refs[:n_pages]
    pe_ref, w1_ref, w2_ref, o_ref, a_buf, o_buf = refs[n_pages:]
    g = NSA_KV
    upp = page_refs[0].shape[1] // CMP_STRIDE
    rows = n_pages * upp * g
    pad = a_buf.shape[0] - rows

    def load(l):
        return jnp.concatenate([pr[0, pl.ds(l, upp, stride=CMP_STRIDE), :, :].reshape(upp * g, NSA_HD)
                                for pr in page_refs], axis=0)

    a, b = _compress_units(load, pe_ref[0], w1_ref)

    @pl.when(pl.program_id(2) == 0)
    def _():
        a_buf[0:pad, :] = jnp.zeros((pad, a_buf.shape[1]), f32)

    @pl.when(pl.program_id(2) > 0)
    def _():
        a_buf[0:pad, :] = a_buf[rows:rows + pad, :]

    a_buf[pad:pad + rows, :] = a
    a_prev = a_buf[pad - g:pad - g + rows, :]
    o_buf[...] = _dot(jax.nn.gelu(a_prev + b).astype(bf16), w2_ref[0])
    for gg in range(g):
        o_ref[0, 0, gg] = o_buf[pl.ds(gg, rows // g, stride=g), :].astype(o_ref.dtype)


def compress_history(pool, page_table, pe, w1, w2):
    b, ppb = page_table.shape
    page = pool.shape[1]
    g = NSA_KV
    n_pages = min(HIST_PAGES_PER_STEP, ppb)
    assert ppb % n_pages == 0 and page % CMP_STRIDE == 0
    upp = page // CMP_STRIDE
    seg = n_pages * upp
    nu = ppb * upp
    sublanes = 8
    page_spec = lambda k: pl.BlockSpec(
        (1, page, None, g, NSA_HD), lambda i, s, ch, pt: (pt[i, ch * n_pages + k], 0, s, 0, 0))
    grid_spec = pltpu.PrefetchScalarGridSpec(
        num_scalar_prefetch=1,
        grid=(b, 2, ppb // n_pages),
        in_specs=[page_spec(k) for k in range(n_pages)] + [
            pl.BlockSpec((1, CMP_BLOCK, NSA_HD), lambda i, s, ch, pt: (s, 0, 0)),
            pl.BlockSpec((1, CMP_BLOCK * NSA_HD, CMP_HIDDEN), lambda i, s, ch, pt: (s, 0, 0)),
            pl.BlockSpec((1, CMP_HIDDEN, NSA_HD), lambda i, s, ch, pt: (s, 0, 0)),
        ],
        out_specs=pl.BlockSpec((1, 1, g, seg, NSA_HD), lambda i, s, ch, pt: (i, s, 0, ch, 0)),
        scratch_shapes=[pltpu.VMEM((seg * g + sublanes, CMP_HIDDEN), f32), pltpu.VMEM((seg * g, NSA_HD), f32)],
    )
    return pl.pallas_call(
        functools.partial(_compress_hist_kernel, n_pages=n_pages),
        grid_spec=grid_spec,
        out_shape=jax.ShapeDtypeStruct((b, 2, g, nu, NSA_HD), bf16),
        compiler_params=_params("parallel", "parallel", "arbitrary"),
        name="compress_history",
    )(page_table, *([pool] * n_pages), pe, w1, w2)


def _nsa_select_decode_kernel(q_ref, kc_ref, vc_ref, msel_ref, oc_ref, top_ref, *, t, ns, n_top):
    g, r_heads, hd = NSA_KV, NSA_HPG, NSA_HD
    q = q_ref[0]
    nu = kc_ref.shape[3]
    u = lax.broadcasted_iota(jnp.int32, (1, nu), 1)
    visible = (u >= 1) & (u * CMP_STRIDE + (CMP_STRIDE - 1) <= t)
    imps = []
    for gg in range(g):
        heads = [q[:, (gg * r_heads + r) * hd:(gg * r_heads + r + 1) * hd] for r in range(r_heads)]
        qb = jnp.concatenate(heads, axis=0)
        p = _masked_softmax(_dot_nt(qb, kc_ref[0, 0, gg]) * NSA_SCALE, visible)
        oc_ref[0, gg] = _dot(p.astype(bf16), vc_ref[0, 0, gg])
        imps.append(jnp.sum(p, axis=0, keepdims=True))
    imp = jnp.concatenate(imps, axis=0)
    sel = jnp.dot(imp, msel_ref[...], precision=lax.Precision.HIGHEST, preferred_element_type=f32)
    jidx = lax.broadcasted_iota(jnp.int32, sel.shape, 1)
    cur = t // SEL_BLOCK
    forced = (jidx == 0) | (jidx == cur) | (jidx == cur - 1)
    sel = jnp.where(jidx * SEL_BLOCK <= t, jnp.where(forced, FORCE_SCORE, sel), NEG)
    sel = jnp.where(jidx < ns, sel, -jnp.inf)
    width = sel.shape[-1]
    lane = lax.broadcasted_iota(jnp.int32, (g, top_ref.shape[-1]), 1)
    top = jnp.zeros(lane.shape, jnp.int32)
    for it in range(n_top):
        mx = jnp.max(sel, axis=-1, keepdims=True)
        idx = jnp.min(jnp.where(sel == mx, jidx, width), axis=-1, keepdims=True)
        top = jnp.where(lane == it, idx, top)
        sel = jnp.where(jidx == idx, -jnp.inf, sel)
    top_ref[0] = top


def nsa_select_decode(q, cmp, t):
    b = q.shape[0]
    g = NSA_KV
    nu = cmp.shape[3]
    ns = (t + 1 + SEL_BLOCK - 1) // SEL_BLOCK
    ns_pad = -(-ns // LANES) * LANES
    per_sel = SEL_BLOCK // CMP_STRIDE
    u = jnp.arange(nu)[:, None]
    j = jnp.arange(ns_pad)[None, :]
    rb = CMP_BLOCK // CMP_STRIDE
    msel = sum(((u - 1 + r) // per_sel == j).astype(f32) for r in range(rb)) / rb
    msel = jnp.where(u >= 1, msel, 0.0)
    gw = NSA_HPG * NSA_HD
    n_top = min(N_SEL, ns)
    cmp_spec = lambda s: pl.BlockSpec((1, 1, g, nu, NSA_HD), lambda i: (i, s, 0, 0, 0))
    return pl.pallas_call(
        functools.partial(_nsa_select_decode_kernel, t=t, ns=ns, n_top=n_top),
        grid=(b,),
        in_specs=[
            pl.BlockSpec((1, 1, g * gw), lambda i: (i, 0, 0)),
            cmp_spec(0), cmp_spec(1),
            pl.BlockSpec((nu, ns_pad), lambda i: (0, 0)),
        ],
        out_specs=[
            pl.BlockSpec((1, g, NSA_HPG, NSA_HD), lambda i: (i, 0, 0, 0)),
            pl.BlockSpec((1, g, LANES), lambda i: (i, 0, 0)),
        ],
        out_shape=[jax.ShapeDtypeStruct((b, g, NSA_HPG, NSA_HD), f32), jax.ShapeDtypeStruct((b, g, LANES), jnp.int32)],
        compiler_params=_params("parallel"),
        name="nsa_select_decode",
    )(q[:, None, :], cmp, cmp, msel)


def _nsa_attn_decode_kernel(pt_ref, top_ref, *refs, n_top, n_hist_blocks):
    k_refs, v_refs = refs[:n_top], refs[n_top:2 * n_top]
    (q_ref, oc_ref, knew_ref, vnew_ref, kw_ref, vw_ref, kwnew_ref, vwnew_ref, gl_ref, o_ref) = refs[2 * n_top:]
    i, gg = pl.program_id(0), pl.program_id(1)
    r_heads, hd = NSA_HPG, NSA_HD
    q = q_ref[0]
    qb = jnp.concatenate([q[:, r * hd:(r + 1) * hd] for r in range(r_heads)], axis=0)

    def attend(keys, vals, mask, k_new, v_new):
        s = jnp.where(mask, _dot_nt(qb, keys) * NSA_SCALE, NEG)
        s_new = jnp.sum(qb.astype(f32) * k_new.astype(f32), axis=-1, keepdims=True) * NSA_SCALE
        m = jnp.maximum(jnp.max(s, axis=-1, keepdims=True), s_new)
        e = jnp.where(mask, jnp.exp(s - m), 0.0)
        e_new = jnp.exp(s_new - m)
        num = _dot(e.astype(bf16), vals) + e_new.astype(bf16).astype(f32) * v_new.astype(f32)
        return num / (jnp.sum(e, axis=-1, keepdims=True) + e_new)

    g = NSA_KV

    def rows_of(ref):
        v = ref[0]
        return v.reshape(v.shape[0] * g, hd).astype(bf16)

    def own_group(n_rows):
        return lax.broadcasted_iota(jnp.int32, (1, n_rows * g), 1) % g == gg

    sb = k_refs[0].shape[1]
    keys = jnp.concatenate([rows_of(r) for r in k_refs], axis=0)
    vals = jnp.concatenate([rows_of(r) for r in v_refs], axis=0)
    blk_of_lane = lax.broadcasted_iota(jnp.int32, (1, n_top * sb * g), 1) // (sb * g)
    sel_of_lane = jnp.zeros((1, n_top * sb * g), jnp.int32)
    for n in range(n_top):
        sel_of_lane = jnp.where(blk_of_lane == n, top_ref[i, gg, n], sel_of_lane)
    o_s = attend(keys, vals, (sel_of_lane < n_hist_blocks) & own_group(n_top * sb), knew_ref[0], vnew_ref[0])

    wlen = kw_ref.shape[1]
    wmask = (lax.broadcasted_iota(jnp.int32, (1, wlen * g), 1) >= g) & own_group(wlen)
    o_w = attend(rows_of(kw_ref), rows_of(vw_ref), wmask, kwnew_ref[0], vwnew_ref[0])

    gates = jax.nn.sigmoid(gl_ref[0])
    o_c = oc_ref[0, 0]
    outs = []
    for r in range(r_heads):
        outs.append(gates[:, 3 * r:3 * r + 1] * o_c[r:r + 1] + gates[:, 3 * r + 1:3 * r + 2] * o_s[r:r + 1]
                    + gates[:, 3 * r + 2:3 * r + 3] * o_w[r:r + 1])
    o_ref[0] = jnp.concatenate(outs, axis=1).astype(o_ref.dtype)


def nsa_attn_decode(q, o_c, top, kvb, gl, pool, page_table, win, win_base, t):
    b = q.shape[0]
    g = NSA_KV
    n_top = top.shape[-1]
    page = pool.shape[1]
    assert t % SEL_BLOCK == 0 and page % SEL_BLOCK == 0 and win.shape[1] == WINDOW
    n_hist_blocks = t // SEL_BLOCK
    bpp = page // SEL_BLOCK
    gw = NSA_HPG * NSA_HD

    def blk_spec(n, slot):
        def imap(i, gg, pt, tp):
            j = jnp.minimum(tp[i, gg, n], n_hist_blocks - 1)
            return (pt[i, j // bpp], j % bpp, slot, 0, 0)
        return pl.BlockSpec((1, SEL_BLOCK, None, g, NSA_HD), imap)

    new_spec = lambda slot: pl.BlockSpec((1, 1, NSA_HD), lambda i, gg, pt, tp: (i, 0, slot * g + gg))
    win_spec = lambda slot: pl.BlockSpec((1, WINDOW, None, g, NSA_HD),
                                         lambda i, gg, pt, tp: (win_base + i, 0, slot, 0, 0))
    kvb3 = kvb[:, None, :]
    grid_spec = pltpu.PrefetchScalarGridSpec(
        num_scalar_prefetch=2,
        grid=(b, g),
        in_specs=[blk_spec(n, 2) for n in range(n_top)] + [blk_spec(n, 3) for n in range(n_top)] + [
            pl.BlockSpec((1, 1, gw), lambda i, gg, pt, tp: (i, 0, gg)),
            pl.BlockSpec((1, 1, NSA_HPG, NSA_HD), lambda i, gg, pt, tp: (i, gg, 0, 0)),
            new_spec(2), new_spec(3), win_spec(0), win_spec(1), new_spec(4), new_spec(5),
            pl.BlockSpec((1, 1, LANES), lambda i, gg, pt, tp: (i, 0, gg)),
        ],
        out_specs=pl.BlockSpec((1, 1, gw), lambda i, gg, pt, tp: (i, 0, gg)),
    )
    o = pl.pallas_call(
        functools.partial(_nsa_attn_decode_kernel, n_top=n_top, n_hist_blocks=n_hist_blocks),
        grid_spec=grid_spec,
        out_shape=jax.ShapeDtypeStruct((b, 1, NSA_HEADS * NSA_HD), bf16),
        compiler_params=_params("parallel", "parallel"),
        name="nsa_attn_decode",
    )(page_table, top, *([pool] * (2 * n_top)), q[:, None, :], o_c, kvb3, kvb3, win, win, kvb3, kvb3,
      gl[:, None, :])
    return o[:, 0, :]


def _retention_mixer(x, pos, batch, w, states, base):
    cos, sin = _rope_tables(pos, RET_DK)
    tn = PROJ_TILE
    n_qk = 2 * RET_HEADS * RET_DK
    flags = jnp.ones((n_qk // tn,), jnp.int32)
    qk, = mm_rope(x, w["in"], (0, n_qk), cos, sin, flags, [bf16], hd=RET_DK, tm=PROJ_TILE, tn=tn)
    vg = mm(x, w["in"], (n_qk, 2 * RET_HEADS * RET_DV), bf16, tm=PROJ_TILE, tn=PROJ_TILE)
    if states is None:
        return retention_prompt(qk, vg, w["gn"], batch)
    return retention_decode(qk, vg, w["gn"], states, base)


def _nsa_projections(x, pos, w):
    cos, sin = _rope_tables(pos, NSA_HD)
    tn = NSA_KV * NSA_HD
    n_q = NSA_HEADS * NSA_HD
    n_kv = 6 * NSA_KV * NSA_HD
    q, = mm_rope(x, w["in"], (0, n_q), cos, sin, jnp.ones((n_q // tn,), jnp.int32), [bf16],
                 hd=NSA_HD, tm=PROJ_TILE, tn=tn)
    kvf, kvb = mm_rope(x, w["in"], (n_q, n_kv), cos, sin, jnp.array([1, 0] * 3, jnp.int32), [f32, bf16],
                       hd=NSA_HD, tm=PROJ_TILE, tn=tn)
    gl = mm(x, w["gl"], (0, w["gl"].shape[1]), f32, tm=PROJ_TILE, tn=tn)
    return q, kvf, kvb, gl


def _layer_tail(x, h, p, w, tm_moe):
    x1 = mm_ln(h, w["out"], x, w["ln_g"][0:1], w["ln_b"][0:1], tm=512, tk=2048)
    ys, d1, d2, info = moe_layer(x1, w["router_w"], w["router_b"], w["moe_gate"], w["moe_up"], w["moe_down"],
                                 w["moe_base"], tm=tm_moe)
    return ln_ple(x1, ys, d1, d2, info, w["ln_g"][1:2], w["ln_b"][1:2], p, w["ple_gate"], w["ple_proj"], tm=256)


def kernel(x_prompt, x_sample, state_ret, cache_nsa_kv, state_nsa_win, page_table, p_prompt, p_sample, ret_w_in, ret_w_out, ret_gn_g, nsa_w_in, nsa_w_out, nsa_cmp_pos, nsa_cmp_w1, nsa_cmp_w2, ln_g, ln_b, router_w, router_b, moe_w_gate, moe_w_up, moe_w_down, ple_w_gate, ple_w_proj):
    bp, tp, d = x_prompt.shape
    bs, ts, _ = x_sample.shape
    assert ts == 1
    n_pool, page = cache_nsa_kv.shape[1], cache_nsa_kv.shape[2]
    past = page_table.shape[1] * page
    g, hd = NSA_KV, NSA_HD
    kv_cols = N_KV_SLOTS * g * hd

    xp = x_prompt.reshape(bp * tp, d)
    xs = x_sample.reshape(bs * ts, d)
    pos_p = jnp.arange(tp, dtype=jnp.int32)
    pos_s = jnp.full((bs,), past, jnp.int32)
    states = state_ret.reshape((-1,) + state_ret.shape[2:])
    pool = cache_nsa_kv.reshape((-1,) + cache_nsa_kv.shape[2:])
    wins = state_nsa_win.reshape((-1,) + state_nsa_win.shape[2:])

    moe_gate = moe_w_gate.astype(bf16).reshape((-1,) + moe_w_gate.shape[2:])
    moe_up = moe_w_up.astype(bf16).reshape((-1,) + moe_w_up.shape[2:])
    moe_down = moe_w_down.astype(bf16).reshape((-1,) + moe_w_down.shape[2:])

    ret_p, ret_s, kv_p, kv_s, win_p, win_s = [], [], [], [], [], []
    for i in range(DEPTH):
        j = i // 2
        w = {
            "ln_g": ln_g[i], "ln_b": ln_b[i],
            "router_w": router_w, "router_b": router_b[None, :],
            "moe_gate": moe_gate, "moe_up": moe_up, "moe_down": moe_down, "moe_base": i * N_EXPERTS,
            "ple_gate": ple_w_gate[i].astype(bf16), "ple_proj": ple_w_proj[i].astype(bf16),
        }
        if i % 2 == 0:
            w.update({"in": ret_w_in[j].astype(bf16), "out": ret_w_out[j].astype(bf16), "gn": ret_gn_g[j][None, :]})
            hp, sp = _retention_mixer(xp, pos_p, bp, w, None, 0)
            hs, ss = _retention_mixer(xs, pos_s, bs, w, states, j * bs)
            ret_p.append(sp)
            ret_s.append(ss)
        else:
            nq = NSA_HEADS * hd
            w_in = nsa_w_in[j]
            gl = w_in[:, nq + 6 * g * hd:].reshape(d, g, NSA_HPG * 3)
            gl = jnp.pad(gl, ((0, 0), (0, 0), (0, LANES - NSA_HPG * 3))).reshape(d, g * LANES)
            w.update({"in": w_in.astype(bf16), "gl": gl.astype(bf16), "out": nsa_w_out[j].astype(bf16)})
            pe = nsa_cmp_pos[j]
            w1 = nsa_cmp_w1[j].reshape(2, CMP_BLOCK * hd, CMP_HIDDEN).astype(bf16)
            w2 = nsa_cmp_w2[j].astype(bf16)
            q, kvf, kvb, glp = _nsa_projections(xp, pos_p, w)
            cmp = compress_prompt(kvf, pe, w1, w2, bp)
            hp = nsa_attn_prompt(q, cmp, kvb, glp, bp)
            kv_p.append(kvf[:, :kv_cols].reshape(bp, tp, N_KV_SLOTS, g, hd))
            keep = min(WINDOW, tp)
            win_p.append(kvf.reshape(bp, tp, -1)[:, tp - keep:, kv_cols:].reshape(bp, keep, 2, g, hd))
            q, kvf, kvb, gls = _nsa_projections(xs, pos_s, w)
            pt = page_table + j * n_pool
            cmp = compress_history(pool, pt, pe, w1, w2)
            o_c, top = nsa_select_decode(q, cmp, past)
            top = top[:, :, :min(N_SEL, past // SEL_BLOCK + 1)]
            hs = nsa_attn_decode(q, o_c, top, kvb, gls, pool, pt, wins, j * bs, past)
            kv_s.append(kvf[:, :kv_cols].reshape(bs, ts, N_KV_SLOTS, g, hd))
            new_win = kvf[:, kv_cols:].reshape(bs, ts, 2, g, hd)
            win_s.append(jnp.concatenate([state_nsa_win[j][:, ts:], new_win], axis=1))
        xp = _layer_tail(xp, hp, p_prompt[i].reshape(bp * tp, -1), w, MOE_TILE)
        xs = _layer_tail(xs, hs, p_sample[i].reshape(bs * ts, -1), w, 16)

    return (xp.reshape(bp, tp, d), xs.reshape(bs, ts, d), jnp.stack(ret_p), jnp.stack(ret_s),
            jnp.stack(kv_p), jnp.stack(kv_s), jnp.stack(win_p), jnp.stack(win_s))
```
